```python
import math
import jax, jax.numpy as jnp
from jax import lax
import numpy as np

D_MODEL = 4096
BATCH = 4
SEQ = 4096
DEPTH = 1

N_META = 16
N_HEADS = D_MODEL // 256
HEAD_DIM = 64
QK_W = N_HEADS * 2 * HEAD_DIM
ATTN_W = N_HEADS * 2 * HEAD_DIM
CONV_CH = D_MODEL // 2
CONV_K = 31
IN_W = 2 * QK_W + ATTN_W + 2 * CONV_CH + 2 * D_MODEL
SPLITS = (QK_W, 2 * QK_W, 2 * QK_W + ATTN_W, 2 * QK_W + ATTN_W + CONV_CH, 2 * QK_W + ATTN_W + 2 * CONV_CH)
N_EXPERTS = 32
TOP_K = 4
D_EXPERT = 3 * D_MODEL // 8
SWIGLU_LIMIT = 7.0
SWIGLU_ALPHA = 1.702
Q_BLOCK = 128
MOE_BLOCK = 128
EPS = 1e-5

kernel_name = "gated_diffattn_conformer_moe_block"


def rms_norm(x, g):
    xf = x.astype(jnp.float32)
    y = xf * lax.rsqrt(jnp.mean(xf * xf, axis=-1, keepdims=True) + EPS) * g.astype(jnp.float32)
    return y.astype(x.dtype)


def layer_norm(x, g, b):
    xf = x.astype(jnp.float32)
    mu = jnp.mean(xf, axis=-1, keepdims=True)
    var = jnp.mean(jnp.square(xf - mu), axis=-1, keepdims=True)
    y = (xf - mu) * lax.rsqrt(var + EPS) * g.astype(jnp.float32) + b.astype(jnp.float32)
    return y.astype(x.dtype)


def diff_attention(q, k, v, lam):
    B, L = q.shape[:2]
    n_pad = (-L) % Q_BLOCK
    pad = ((0, 0), (n_pad, 0), (0, 0), (0, 0))
    q = jnp.pad(q, pad + ((0, 0),))
    k = jnp.pad(k, pad + ((0, 0),))
    v = jnp.pad(v, pad)
    Lp = L + n_pad
    scale = HEAD_DIM ** -0.5
    neg = jnp.finfo(jnp.float32).min
    outs = []
    for i in range(Lp // Q_BLOCK):
        q0, q1 = i * Q_BLOCK, (i + 1) * Q_BLOCK
        qb, kb, vb = q[:, q0:q1], k[:, :q1], v[:, :q1]
        s = jnp.einsum('bqhmd,bkhmd->bhmqk', qb, kb).astype(jnp.float32) * scale
        qpos = jnp.arange(q0, q1)[:, None]
        kpos = jnp.arange(q1)[None, :]
        mask = (kpos <= qpos) & (kpos >= n_pad)
        p = jax.nn.softmax(jnp.where(mask, s, neg), axis=-1)
        a = p[:, :, 0] - lam * p[:, :, 1]
        outs.append(jnp.einsum('bhqk,bkhe->bqhe', a, vb.astype(jnp.float32)))
    return jnp.concatenate(outs, axis=1)[:, n_pad:]


def conformer_conv(c_a, c_g, conv_w, conv_b, ln_g, ln_b, w_pw2, b_pw2):
    u = c_a * jax.nn.sigmoid(c_g)
    y = lax.conv_general_dilated(
        u, conv_w[:, None, :].astype(u.dtype), window_strides=(1,), padding=[(CONV_K - 1, 0)],
        dimension_numbers=('NWC', 'WIO', 'NWC'), feature_group_count=u.shape[-1])
    y = y + conv_b
    y = jax.nn.silu(layer_norm(y, ln_g, ln_b))
    return y @ w_pw2 + b_pw2


def moe_ffn(h, layer, w_router, b_router, w_gate_up, b_gate_up, w_down, b_down):
    T, D = h.shape
    TK = T * TOP_K
    logits = (h @ w_router[layer] + b_router[layer]).astype(jnp.float32)
    top_logit, top_e = lax.top_k(logits, TOP_K)
    gate = jax.nn.softmax(top_logit, axis=-1)
    flat_e = top_e.reshape(-1)
    order = jnp.argsort(flat_e)
    sorted_e = flat_e[order]
    counts = jnp.bincount(flat_e, length=N_EXPERTS)
    padded = (counts + MOE_BLOCK - 1) // MOE_BLOCK * MOE_BLOCK
    pad_end = jnp.cumsum(padded)
    pad_start = pad_end - padded
    grp_start = jnp.cumsum(counts) - counts
    dest = pad_start[sorted_e] + jnp.arange(TK) - grp_start[sorted_e]
    n_blocks = -(-(TK + N_EXPERTS * (MOE_BLOCK - 1)) // MOE_BLOCK)
    n_slots = n_blocks * MOE_BLOCK
    slot_tok = jnp.full((n_slots,), T, jnp.int32).at[dest].set((order // TOP_K).astype(jnp.int32))
    slot_gate = jnp.zeros((n_slots,), jnp.float32).at[dest].set(gate.reshape(-1)[order])
    block_e = jnp.minimum(jnp.searchsorted(pad_end, jnp.arange(n_blocks) * MOE_BLOCK, side='right'),
                          N_EXPERTS - 1)
    h_pad = jnp.concatenate([h, jnp.zeros((1, D), h.dtype)], axis=0)

    def expert_block(args):
        tok, e = args
        xb = h_pad[tok]
        gu = xb @ w_gate_up[layer, e] + b_gate_up[layer, e]
        g, u = gu[:, :D_EXPERT], gu[:, D_EXPERT:]
        g = jnp.minimum(g, SWIGLU_LIMIT)
        u = jnp.clip(u, -SWIGLU_LIMIT, SWIGLU_LIMIT)
        act = (u + 1) * (g * jax.nn.sigmoid(SWIGLU_ALPHA * g))
        return act @ w_down[layer, e] + b_down[layer, e]

    ys = lax.map(expert_block, (slot_tok.reshape(n_blocks, MOE_BLOCK), block_e))
    ys = ys.reshape(n_slots, D).astype(jnp.float32) * slot_gate[:, None]
    out = jnp.zeros((T + 1, D), jnp.float32).at[slot_tok].add(ys)[:T]
    return out.astype(h.dtype)


def setup_inputs(seed: int = 0) -> dict:
    key = jax.random.key(seed)
    ks = jax.random.split(key, 26)
    nrm = jax.random.normal
    f32 = jnp.float32
    return {
        "x": nrm(ks[0], (BATCH, SEQ, D_MODEL), f32),
        "meta_tokens": nrm(ks[1], (N_META, D_MODEL), f32),
        "norm_mix_g": 1.0 + 0.02 * nrm(ks[2], (DEPTH, D_MODEL), f32),
        "w_in": nrm(ks[3], (DEPTH, D_MODEL, IN_W), f32) * D_MODEL ** -0.5,
        "b_gate": 0.02 * nrm(ks[4], (DEPTH, 2 * D_MODEL), f32),
        "lambda_q1": 0.1 * nrm(ks[5], (DEPTH, HEAD_DIM), f32),
        "lambda_k1": 0.1 * nrm(ks[6], (DEPTH, HEAD_DIM), f32),
        "lambda_q2": 0.1 * nrm(ks[7], (DEPTH, HEAD_DIM), f32),
        "lambda_k2": 0.1 * nrm(ks[8], (DEPTH, HEAD_DIM), f32),
        "head_norm_g": 1.0 + 0.02 * nrm(ks[9], (DEPTH, 2 * HEAD_DIM), f32),
        "w_attn_out": nrm(ks[10], (DEPTH, ATTN_W, D_MODEL), f32) * ATTN_W ** -0.5,
        "conv_w": nrm(ks[11], (DEPTH, CONV_K, CONV_CH), f32) * CONV_K ** -0.5,
        "conv_b": 0.02 * nrm(ks[12], (DEPTH, CONV_CH), f32),
        "conv_ln_g": 1.0 + 0.02 * nrm(ks[13], (DEPTH, CONV_CH), f32),
        "conv_ln_b": 0.02 * nrm(ks[14], (DEPTH, CONV_CH), f32),
        "w_conv_out": nrm(ks[15], (DEPTH, CONV_CH, D_MODEL), f32) * CONV_CH ** -0.5,
        "b_conv_out": 0.02 * nrm(ks[16], (DEPTH, D_MODEL), f32),
        "w_out": nrm(ks[17], (DEPTH, D_MODEL, D_MODEL), f32) * D_MODEL ** -0.5,
        "norm_ffn_g": 1.0 + 0.02 * nrm(ks[18], (DEPTH, D_MODEL), f32),
        "w_router": nrm(ks[19], (DEPTH, D_MODEL, N_EXPERTS), f32) * D_MODEL ** -0.5,
        "b_router": 0.01 * nrm(ks[20], (DEPTH, N_EXPERTS), f32),
        "w_gate_up": nrm(ks[21], (DEPTH, N_EXPERTS, D_MODEL, 2 * D_EXPERT), f32) * D_MODEL ** -0.5,
        "b_gate_up": 0.02 * nrm(ks[22], (DEPTH, N_EXPERTS, 2 * D_EXPERT), f32),
        "w_down": nrm(ks[23], (DEPTH, N_EXPERTS, D_EXPERT, D_MODEL), f32) * D_EXPERT ** -0.5,
        "b_down": 0.02 * nrm(ks[24], (DEPTH, N_EXPERTS, D_MODEL), f32),
        "final_norm_g": 1.0 + 0.02 * nrm(ks[25], (D_MODEL,), f32),
    }


def reference(x, meta_tokens, norm_mix_g, w_in, b_gate, lambda_q1, lambda_k1, lambda_q2, lambda_k2,
              head_norm_g, w_attn_out, conv_w, conv_b, conv_ln_g, conv_ln_b, w_conv_out, b_conv_out,
              w_out, norm_ffn_g, w_router, b_router, w_gate_up, b_gate_up, w_down, b_down,
              final_norm_g):
    B, S, _ = x.shape
    L = N_META + S
    meta = jnp.broadcast_to(meta_tokens[None].astype(x.dtype), (B, N_META, D_MODEL))
    h = jnp.concatenate([meta, x], axis=1)
    for layer in range(DEPTH):
        u = rms_norm(h, norm_mix_g[layer])
        proj = u @ w_in[layer]
        q, k, v, c_a, c_g, gates = jnp.split(proj, SPLITS, axis=-1)
        q = q.reshape(B, L, N_HEADS, 2, HEAD_DIM)
        k = k.reshape(B, L, N_HEADS, 2, HEAD_DIM)
        v = v.reshape(B, L, N_HEADS, 2 * HEAD_DIM)
        lam_init = 0.8 - 0.6 * math.exp(-0.3 * layer)
        lam = (jnp.exp(jnp.sum(lambda_q1[layer].astype(jnp.float32) * lambda_k1[layer].astype(jnp.float32)))
               - jnp.exp(jnp.sum(lambda_q2[layer].astype(jnp.float32) * lambda_k2[layer].astype(jnp.float32)))
               + lam_init)
        o = diff_attention(q, k, v, lam)
        o = rms_norm(o, head_norm_g[layer]) * (1.0 - lam_init)
        y_attn = o.reshape(B, L, ATTN_W).astype(x.dtype) @ w_attn_out[layer]
        y_conv = conformer_conv(c_a, c_g, conv_w[layer], conv_b[layer], conv_ln_g[layer],
                                conv_ln_b[layer], w_conv_out[layer], b_conv_out[layer])
        g = jax.nn.sigmoid((gates + b_gate[layer]).astype(jnp.float32))
        merged = (g[..., :D_MODEL] * y_attn + g[..., D_MODEL:] * y_conv).astype(x.dtype)
        h = h + merged @ w_out[layer]
        u = rms_norm(h, norm_ffn_g[layer])
        y = moe_ffn(u.reshape(B * L, D_MODEL), layer, w_router, b_router, w_gate_up, b_gate_up,
                    w_down, b_down)
        h = h + y.reshape(B, L, D_MODEL)
    h = rms_norm(h, final_norm_g)
    return h[:, N_META:]
```

```python
import functools
import math

import numpy as np
import jax
import jax.numpy as jnp
from jax import lax
from jax.experimental import pallas as pl
from jax.experimental.pallas import tpu as pltpu

N_META = 16
HEAD_DIM = 64
N_EXPERTS = 32
TOP_K = 4
CONV_K = 31
EPS = 1e-5
SWIGLU_LIMIT = 7.0
SWIGLU_ALPHA = 1.702
ROW_ALIGN = 128
CONV_HALO = 32
MOE_TILE = 256
VMEM_LIMIT = 56 * 1024 * 1024

F32 = jnp.float32
BF16 = jnp.bfloat16


def _cparams(n_axes):
    return pltpu.CompilerParams(dimension_semantics=("arbitrary",) * n_axes,
                                vmem_limit_bytes=VMEM_LIMIT)


def _rmsnorm_kernel(x_ref, g_ref, o_ref):
    x = x_ref[...]
    ms = jnp.mean(x * x, axis=-1, keepdims=True)
    o_ref[...] = (x * lax.rsqrt(ms + EPS) * g_ref[...]).astype(o_ref.dtype)


def _rmsnorm(x, g, tm):
    m, d = x.shape
    return pl.pallas_call(
        _rmsnorm_kernel,
        grid=(m // tm,),
        in_specs=[pl.BlockSpec((tm, d), lambda i: (i, 0)),
                  pl.BlockSpec((1, d), lambda i: (0, 0))],
        out_specs=pl.BlockSpec((tm, d), lambda i: (i, 0)),
        out_shape=jax.ShapeDtypeStruct((m, d), BF16),
        compiler_params=_cparams(1),
        name="rmsnorm",
    )(x, g.reshape(1, d))


def _matmul_kernel(a_ref, w_ref, o_ref):
    o_ref[...] = jnp.dot(a_ref[...], w_ref[...], preferred_element_type=F32).astype(o_ref.dtype)


def _matmul_res_kernel(a_ref, w_ref, r_ref, o_ref):
    acc = jnp.dot(a_ref[...], w_ref[...], preferred_element_type=F32)
    o_ref[...] = (acc + r_ref[...]).astype(o_ref.dtype)


def _matmul(a, w, out_dtype, tm, tn, res=None, name="matmul"):
    m, k = a.shape
    n = w.shape[1]
    in_specs = [pl.BlockSpec((tm, k), lambda j, i: (i, 0)),
                pl.BlockSpec((k, tn), lambda j, i: (0, j))]
    args = [a, w]
    kern = _matmul_kernel
    if res is not None:
        in_specs.append(pl.BlockSpec((tm, tn), lambda j, i: (i, j)))
        args.append(res)
        kern = _matmul_res_kernel
    return pl.pallas_call(
        kern,
        grid=(n // tn, m // tm),
        in_specs=in_specs,
        out_specs=pl.BlockSpec((tm, tn), lambda j, i: (i, j)),
        out_shape=jax.ShapeDtypeStruct((m, n), out_dtype),
        compiler_params=_cparams(2),
        name=name,
    )(*args)


def _attn_kernel(lq1_ref, lk1_ref, lq2_ref, lk2_ref, hg_ref, q_ref, k_ref, v_ref, o_ref,
                 m_sc, l_sc, acc_sc, *, tq, n_pad, lam_init):
    qi = pl.program_id(2)
    neg = float(jnp.finfo(jnp.float32).min)
    lam = (jnp.exp(jnp.sum(lq1_ref[...] * lk1_ref[...], axis=-1, keepdims=True))
           - jnp.exp(jnp.sum(lq2_ref[...] * lk2_ref[...], axis=-1, keepdims=True)) + lam_init)

    q = q_ref[...] * jnp.asarray(HEAD_DIM ** -0.5, BF16)
    lane = lax.broadcasted_iota(jnp.int32, q.shape, 1)
    zero = jnp.zeros_like(q)
    qs = jnp.concatenate([jnp.where(lane < HEAD_DIM, q, zero), jnp.where(lane >= HEAD_DIM, q, zero)], axis=0)

    m_sc[...] = jnp.full(m_sc.shape, -jnp.inf, F32)
    l_sc[...] = jnp.zeros(l_sc.shape, F32)
    acc_sc[...] = jnp.zeros(acc_sc.shape, F32)

    def process(kj, masked):
        k0 = kj * tq
        if not isinstance(k0, int):
            k0 = pl.multiple_of(k0, tq)
        kb = k_ref[pl.ds(k0, tq), :]
        vb = v_ref[pl.ds(k0, tq), :]
        s = lax.dot_general(qs, kb, (((1,), (1,)), ((), ())), preferred_element_type=F32)
        if masked:
            row = lax.broadcasted_iota(jnp.int32, s.shape, 0)
            qpos = qi * tq + jnp.where(row >= tq, row - tq, row)
            kpos = kj * tq + lax.broadcasted_iota(jnp.int32, s.shape, 1)
            s = jnp.where((kpos <= qpos) & (kpos >= n_pad), s, neg)
        m_prev = m_sc[...]
        m_new = jnp.maximum(m_prev, jnp.max(s, axis=-1, keepdims=True))
        alpha = jnp.exp(m_prev - m_new)
        p = jnp.exp(s - m_new)
        l_sc[...] = alpha * l_sc[...] + jnp.sum(p, axis=-1, keepdims=True)
        acc_sc[...] = alpha * acc_sc[...] + jnp.dot(p.astype(BF16), vb, preferred_element_type=F32)
        m_sc[...] = m_new

    @pl.when(qi > 0)
    def _():
        process(0, True)

    def body(kj, c):
        process(kj, False)
        return c
    lax.fori_loop(1, qi, body, 0)
    process(qi, True)

    o12 = acc_sc[...] / l_sc[...]
    o = o12[:tq] - lam * o12[tq:]
    ms = jnp.mean(o * o, axis=-1, keepdims=True)
    o_ref[...] = (o * lax.rsqrt(ms + EPS) * hg_ref[...] * (1.0 - lam_init)).astype(o_ref.dtype)


def _attention(proj, lq1, lk1, lq2, lk2, head_g, *, batch, lp, n_heads, tq, lam_init, n_pad):
    hw = 2 * HEAD_DIM
    nq = lp // tq
    koff = n_heads
    voff = 2 * n_heads
    vec = lambda a: a.reshape(1, -1).astype(F32)
    small = lambda n: pl.BlockSpec((1, n), lambda b, h, qi: (0, 0))
    return pl.pallas_call(
        functools.partial(_attn_kernel, tq=tq, n_pad=n_pad, lam_init=lam_init),
        grid=(batch, n_heads, nq),
        in_specs=[small(HEAD_DIM), small(HEAD_DIM), small(HEAD_DIM), small(HEAD_DIM), small(hw),
                  pl.BlockSpec((tq, hw), lambda b, h, qi: (b * nq + qi, h)),
                  pl.BlockSpec((lp, hw), lambda b, h, qi: (b, koff + h)),
                  pl.BlockSpec((lp, hw), lambda b, h, qi: (b, voff + h))],
        out_specs=pl.BlockSpec((tq, hw), lambda b, h, qi: (b * nq + qi, h)),
        out_shape=jax.ShapeDtypeStruct((batch * lp, n_heads * hw), BF16),
        scratch_shapes=[pltpu.VMEM((2 * tq, 1), F32), pltpu.VMEM((2 * tq, 1), F32),
                        pltpu.VMEM((2 * tq, hw), F32)],
        compiler_params=_cparams(3),
        name="diff_attention",
    )(vec(lq1), vec(lk1), vec(lq2), vec(lk2), vec(head_g), proj, proj, proj)


def _conv_kernel(ca_ref, cg_ref, ca_h_ref, cg_h_ref, w_ref, b_ref, lg_ref, lb_ref, z_ref, ext_sc, *, tm):
    i = pl.program_id(0)
    glu = lambda a, g: a.astype(F32) * jax.nn.sigmoid(g.astype(F32))
    halo = glu(ca_h_ref[...], cg_h_ref[...])
    ext_sc[0:CONV_HALO, :] = jnp.where(i > 0, halo, jnp.zeros_like(halo))
    ext_sc[CONV_HALO:CONV_HALO + tm, :] = glu(ca_ref[...], cg_ref[...])
    w = w_ref[...]
    acc = jnp.zeros((tm, w.shape[1]), F32) + b_ref[...]
    base = CONV_HALO - (CONV_K - 1)
    for j in range(CONV_K):
        acc = acc + w[j:j + 1, :] * ext_sc[base + j:base + j + tm, :]
    mu = jnp.mean(acc, axis=-1, keepdims=True)
    d = acc - mu
    var = jnp.mean(d * d, axis=-1, keepdims=True)
    y = d * lax.rsqrt(var + EPS) * lg_ref[...] + lb_ref[...]
    z_ref[...] = (y * jax.nn.sigmoid(y)).astype(z_ref.dtype)


def _conv_branch(proj, conv_w, conv_b, ln_g, ln_b, *, ca_blk, cg_blk, tm):
    m = proj.shape[0]
    c = conv_w.shape[1]
    hb = tm // CONV_HALO
    row = lambda a: a.reshape(1, c).astype(F32)
    vec = pl.BlockSpec((1, c), lambda i: (0, 0))
    return pl.pallas_call(
        functools.partial(_conv_kernel, tm=tm),
        grid=(m // tm,),
        in_specs=[pl.BlockSpec((tm, c), lambda i: (i, ca_blk)),
                  pl.BlockSpec((tm, c), lambda i: (i, cg_blk)),
                  pl.BlockSpec((CONV_HALO, c), lambda i: (jnp.maximum(i * hb - 1, 0), ca_blk)),
                  pl.BlockSpec((CONV_HALO, c), lambda i: (jnp.maximum(i * hb - 1, 0), cg_blk)),
                  pl.BlockSpec((CONV_K, c), lambda i: (0, 0)),
                  vec, vec, vec],
        out_specs=pl.BlockSpec((tm, c), lambda i: (i, 0)),
        out_shape=jax.ShapeDtypeStruct((m, c), BF16),
        scratch_shapes=[pltpu.VMEM((CONV_HALO + tm, c), F32)],
        compiler_params=_cparams(1),
        name="conformer_conv",
    )(proj, proj, proj, proj, conv_w.astype(F32), row(conv_b), row(ln_g), row(ln_b))


def _merge_kernel(o_ref, z_ref, wa_ref, wc_ref, bc_ref, g1_ref, g2_ref, bg1_ref, bg2_ref, out_ref):
    ya = jnp.dot(o_ref[...], wa_ref[...], preferred_element_type=F32)
    yc = jnp.dot(z_ref[...], wc_ref[...], preferred_element_type=F32) + bc_ref[...]
    g1 = jax.nn.sigmoid(g1_ref[...].astype(F32) + bg1_ref[...])
    g2 = jax.nn.sigmoid(g2_ref[...].astype(F32) + bg2_ref[...])
    out_ref[...] = (g1 * ya + g2 * yc).astype(out_ref.dtype)


def _merge(o, z, proj, wa, wc, bc, bg, *, gate_col, tm, tn):
    m, ka = o.shape
    kc = z.shape[1]
    d = wa.shape[1]
    g1_blk = gate_col // tn
    g2_blk = (gate_col + d) // tn
    nb = d // tn
    bg2 = bg.reshape(1, 2 * d).astype(F32)
    return pl.pallas_call(
        _merge_kernel,
        grid=(nb, m // tm),
        in_specs=[pl.BlockSpec((tm, ka), lambda j, i: (i, 0)),
                  pl.BlockSpec((tm, kc), lambda j, i: (i, 0)),
                  pl.BlockSpec((ka, tn), lambda j, i: (0, j)),
                  pl.BlockSpec((kc, tn), lambda j, i: (0, j)),
                  pl.BlockSpec((1, tn), lambda j, i: (0, j)),
                  pl.BlockSpec((tm, tn), lambda j, i: (i, g1_blk + j)),
                  pl.BlockSpec((tm, tn), lambda j, i: (i, g2_blk + j)),
                  pl.BlockSpec((1, tn), lambda j, i: (0, j)),
                  pl.BlockSpec((1, tn), lambda j, i: (0, nb + j))],
        out_specs=pl.BlockSpec((tm, tn), lambda j, i: (i, j)),
        out_shape=jax.ShapeDtypeStruct((m, d), BF16),
        compiler_params=_cparams(2),
        name="mixer_merge",
    )(o, z, wa, wc, bc.reshape(1, d).astype(F32), proj, proj, bg2, bg2)


def _router_kernel(h_ref, g_ref, wr_ref, br_ref, valid_ref,
                   up_ref, e_ref, gate_ref, rank_ref, cnt_ref, carry_sc, *, tm):
    i = pl.program_id(0)

    @pl.when(i == 0)
    def _():
        carry_sc[...] = jnp.zeros(carry_sc.shape, F32)

    h = h_ref[...]
    ms = jnp.mean(h * h, axis=-1, keepdims=True)
    u = h * lax.rsqrt(ms + EPS) * g_ref[...]

    half = u.shape[1] // 2
    bits = lax.bitcast_convert_type(u.astype(BF16).astype(F32), jnp.uint32)
    up_ref[...] = (bits[:, half:] & jnp.uint32(0xFFFF0000)) | (bits[:, :half] >> 16)

    logits = lax.dot_general(wr_ref[...], u, (((1,), (1,)), ((), ())),
                             precision=lax.Precision.HIGHEST,
                             preferred_element_type=F32) + br_ref[...]
    n_e = logits.shape[0]
    eiota = lax.broadcasted_iota(jnp.int32, logits.shape, 0).astype(F32)
    work = logits
    sel = jnp.zeros(logits.shape, jnp.bool_)
    top_l, top_e = [], []
    for _ in range(TOP_K):
        mx = jnp.max(work, axis=0, keepdims=True)
        idx = jnp.min(jnp.where(work == mx, eiota, float(n_e)), axis=0, keepdims=True)
        hit = eiota == idx
        top_l.append(mx)
        top_e.append(idx)
        sel = sel | hit
        work = jnp.where(hit, -jnp.inf, work)
    ex = [jnp.exp(t - top_l[0]) for t in top_l]
    den = ex[0] + ex[1] + ex[2] + ex[3]
    gate_ref[...] = jnp.concatenate([e / den for e in ex], axis=0)
    e_ref[...] = jnp.concatenate(top_e, axis=0).astype(jnp.int32)

    selv = jnp.where(sel & (valid_ref[...] > 0.0), 1.0, 0.0)
    before = (lax.broadcasted_iota(jnp.int32, (tm, tm), 0)
              < lax.broadcasted_iota(jnp.int32, (tm, tm), 1)).astype(BF16)
    rank_all = jnp.dot(selv.astype(BF16), before, preferred_element_type=F32) + carry_sc[...]
    ranks = [jnp.sum(jnp.where(eiota == idx, rank_all, 0.0), axis=0, keepdims=True) for idx in top_e]
    rank_ref[...] = jnp.concatenate(ranks, axis=0).astype(jnp.int32)
    carry = carry_sc[...] + jnp.sum(selv, axis=1, keepdims=True)
    carry_sc[...] = carry
    cnt_ref[...] = jnp.broadcast_to(carry, cnt_ref.shape).astype(jnp.int32)


def _router(h2, g, w_router, b_router, valid, *, tm):
    m, d = h2.shape
    n_e = w_router.shape[1]
    tok = lambda dt: jax.ShapeDtypeStruct((TOP_K, m), dt)
    tok_spec = pl.BlockSpec((TOP_K, tm), lambda i: (0, i))
    return pl.pallas_call(
        functools.partial(_router_kernel, tm=tm),
        grid=(m // tm,),
        in_specs=[pl.BlockSpec((tm, d), lambda i: (i, 0)),
                  pl.BlockSpec((1, d), lambda i: (0, 0)),
                  pl.BlockSpec((n_e, d), lambda i: (0, 0)),
                  pl.BlockSpec((n_e, 1), lambda i: (0, 0)),
                  pl.BlockSpec((1, tm), lambda i: (0, i))],
        out_specs=[pl.BlockSpec((tm, d // 2), lambda i: (i, 0)),
                   tok_spec, tok_spec, tok_spec,
                   pl.BlockSpec((n_e, 128), lambda i: (0, 0))],
        out_shape=[jax.ShapeDtypeStruct((m, d // 2), jnp.uint32),
                   tok(jnp.int32), tok(F32), tok(jnp.int32),
                   jax.ShapeDtypeStruct((n_e, 128), jnp.int32)],
        scratch_shapes=[pltpu.VMEM((n_e, 1), F32)],
        compiler_params=_cparams(1),
        name="router",
    )(h2, g.reshape(1, d).astype(F32), w_router.T.astype(F32), b_router.reshape(n_e, 1).astype(F32), valid)


def _dispatch_kernel(dest_ref, u_ref, xs_in_ref, xs_ref, sem, *, tm):
    del xs_in_ref
    i = pl.program_id(0)

    def row_copy(r, k):
        d = dest_ref[(i * tm + r) * TOP_K + k]
        return pltpu.make_async_copy(u_ref.at[pl.ds(r, 1), :], xs_ref.at[pl.ds(d, 1), :], sem)

    def start(r, c):
        for k in range(TOP_K):
            row_copy(r, k).start()
        return c

    def wait(r, c):
        for k in range(TOP_K):
            row_copy(r, k).wait()
        return c

    lax.fori_loop(0, tm, start, 0)
    lax.fori_loop(0, tm, wait, 0)


def _dispatch(dest_flat, u_packed, n_rows, *, tm):
    m, w = u_packed.shape
    xs0 = jnp.zeros((n_rows, w), jnp.uint32)
    grid_spec = pltpu.PrefetchScalarGridSpec(
        num_scalar_prefetch=1,
        grid=(m // tm,),
        in_specs=[pl.BlockSpec((tm, w), lambda i, dest: (i, 0)),
                  pl.BlockSpec(memory_space=pl.ANY)],
        out_specs=pl.BlockSpec(memory_space=pl.ANY),
        scratch_shapes=[pltpu.SemaphoreType.DMA(())],
    )
    return pl.pallas_call(
        functools.partial(_dispatch_kernel, tm=tm),
        grid_spec=grid_spec,
        out_shape=jax.ShapeDtypeStruct((n_rows, w), jnp.uint32),
        input_output_aliases={2: 0},
        compiler_params=_cparams(1),
        name="moe_dispatch",
    )(dest_flat, u_packed, xs0)


def _gate_up_kernel(te_ref, nu_ref, xs_ref, wg_ref, wu_ref, bg_ref, bu_ref, act_ref):
    r = pl.program_id(1)

    @pl.when(r < nu_ref[0])
    def _():
        w = xs_ref[...]
        half = w.shape[1]
        lo = lax.bitcast_convert_type(w << 16, F32).astype(BF16)
        hi = lax.bitcast_convert_type(w & jnp.uint32(0xFFFF0000), F32).astype(BF16)
        g = (jnp.dot(lo, wg_ref[:half, :], preferred_element_type=F32)
             + jnp.dot(hi, wg_ref[half:, :], preferred_element_type=F32) + bg_ref[...])
        u = (jnp.dot(lo, wu_ref[:half, :], preferred_element_type=F32)
             + jnp.dot(hi, wu_ref[half:, :], preferred_element_type=F32) + bu_ref[...])
        g = jnp.minimum(g, SWIGLU_LIMIT)
        u = jnp.clip(u, -SWIGLU_LIMIT, SWIGLU_LIMIT)
        act_ref[...] = ((u + 1.0) * (g * jax.nn.sigmoid(SWIGLU_ALPHA * g))).astype(act_ref.dtype)

    @pl.when(r >= nu_ref[0])
    def _():
        act_ref[...] = jnp.zeros(act_ref.shape, act_ref.dtype)


def _gate_up(tile_e, n_used, xs, w_gu, b_gu, *, n_tiles, tf):
    n_e, d, f2 = w_gu.shape
    f = f2 // 2
    nj = f // tf
    rr = lambda r, nu: jnp.minimum(r, nu[0] - 1)
    grid_spec = pltpu.PrefetchScalarGridSpec(
        num_scalar_prefetch=2,
        grid=(nj, n_tiles),
        in_specs=[pl.BlockSpec((MOE_TILE, d // 2), lambda j, r, te, nu: (rr(r, nu), 0)),
                  pl.BlockSpec((None, d, tf), lambda j, r, te, nu: (te[rr(r, nu)], 0, j)),
                  pl.BlockSpec((None, d, tf), lambda j, r, te, nu: (te[rr(r, nu)], 0, nj + j)),
                  pl.BlockSpec((None, 1, tf), lambda j, r, te, nu: (te[rr(r, nu)], 0, j)),
                  pl.BlockSpec((None, 1, tf), lambda j, r, te, nu: (te[rr(r, nu)], 0, nj + j))],
        out_specs=pl.BlockSpec((MOE_TILE, tf), lambda j, r, te, nu: (r, j)),
    )
    return pl.pallas_call(
        _gate_up_kernel,
        grid_spec=grid_spec,
        out_shape=jax.ShapeDtypeStruct((n_tiles * MOE_TILE, f), BF16),
        compiler_params=_cparams(2),
        name="moe_gate_up",
    )(tile_e, n_used, xs, w_gu, w_gu, b_gu, b_gu)


def _down_kernel(te_ref, nu_ref, act_ref, wd_ref, bd_ref, y_ref):
    r = pl.program_id(1)

    @pl.when(r < nu_ref[0])
    def _():
        y_ref[...] = jnp.dot(act_ref[...], wd_ref[...], preferred_element_type=F32) + bd_ref[...]

    @pl.when(r >= nu_ref[0])
    def _():
        y_ref[...] = jnp.zeros(y_ref.shape, y_ref.dtype)


def _down(tile_e, n_used, act, w_d, b_d, *, n_tiles, tn):
    n_e, f, d = w_d.shape
    rr = lambda r, nu: jnp.minimum(r, nu[0] - 1)
    grid_spec = pltpu.PrefetchScalarGridSpec(
        num_scalar_prefetch=2,
        grid=(d // tn, n_tiles),
        in_specs=[pl.BlockSpec((MOE_TILE, f), lambda j, r, te, nu: (rr(r, nu), 0)),
                  pl.BlockSpec((None, f, tn), lambda j, r, te, nu: (te[rr(r, nu)], 0, j)),
                  pl.BlockSpec((None, 1, tn), lambda j, r, te, nu: (te[rr(r, nu)], 0, j))],
        out_specs=pl.BlockSpec((MOE_TILE, tn), lambda j, r, te, nu: (r, j)),
    )
    return pl.pallas_call(
        _down_kernel,
        grid_spec=grid_spec,
        out_shape=jax.ShapeDtypeStruct((n_tiles * MOE_TILE, d), F32),
        compiler_params=_cparams(2),
        name="moe_down",
    )(tile_e, n_used, act, w_d, b_d)


def _combine_kernel(dest_ref, h_ref, gate_ref, g_ref, ys_ref, out_ref, buf, sem, *, tm, lp, row0):
    b = pl.program_id(0)
    si = pl.program_id(1)
    t0 = b * lp + row0 + si * tm

    def row_copy(r, k):
        d = dest_ref[(t0 + r) * TOP_K + k]
        return pltpu.make_async_copy(ys_ref.at[pl.ds(d, 1), :], buf.at[k, pl.ds(r, 1), :], sem)

    def start(r, c):
        for k in range(TOP_K):
            row_copy(r, k).start()
        return c

    def wait(r, c):
        for k in range(TOP_K):
            row_copy(r, k).wait()
        return c

    lax.fori_loop(0, tm, start, 0)
    lax.fori_loop(0, tm, wait, 0)

    gate = gate_ref[...]
    acc = jnp.zeros(h_ref.shape, F32)
    for k in range(TOP_K):
        acc = acc + gate[:, k:k + 1] * buf[k]
    h = h_ref[...] + acc
    ms = jnp.mean(h * h, axis=-1, keepdims=True)
    out_ref[...] = (h * lax.rsqrt(ms + EPS) * g_ref[...]).astype(out_ref.dtype)


def _combine(dest_flat, h2, gate, g, ys, *, batch, seq, lp, row0, tm):
    d = h2.shape[1]
    nb_b = lp // tm
    nb0 = row0 // tm
    grid_spec = pltpu.PrefetchScalarGridSpec(
        num_scalar_prefetch=1,
        grid=(batch, seq // tm),
        in_specs=[pl.BlockSpec((tm, d), lambda b, s, dest: (b * nb_b + nb0 + s, 0)),
                  pl.BlockSpec((tm, TOP_K), lambda b, s, dest: (b * nb_b + nb0 + s, 0)),
                  pl.BlockSpec((1, d), lambda b, s, dest: (0, 0)),
                  pl.BlockSpec(memory_space=pl.ANY)],
        out_specs=pl.BlockSpec((None, tm, d), lambda b, s, dest: (b, s, 0)),
        scratch_shapes=[pltpu.VMEM((TOP_K, tm, d), F32), pltpu.SemaphoreType.DMA(())],
    )
    return pl.pallas_call(
        functools.partial(_combine_kernel, tm=tm, lp=lp, row0=row0),
        grid_spec=grid_spec,
        out_shape=jax.ShapeDtypeStruct((batch, seq, d), F32),
        compiler_params=_cparams(2),
        name="moe_combine",
    )(dest_flat, h2, gate, g.reshape(1, d).astype(F32), ys)


def _pick(pref, n):
    t = pref
    while n % t:
        t //= 2
    return t


def kernel(x, meta_tokens, norm_mix_g, w_in, b_gate, lambda_q1, lambda_k1, lambda_q2, lambda_k2, head_norm_g, w_attn_out, conv_w, conv_b, conv_ln_g, conv_ln_b, w_conv_out, b_conv_out, w_out, norm_ffn_g, w_router, b_router, w_gate_up, b_gate_up, w_down, b_down, final_norm_g):
    batch, seq, d = x.shape
    depth = w_in.shape[0]
    assert depth == 1 and seq % ROW_ALIGN == 0 and N_META <= ROW_ALIGN
    n_heads = d // 256
    hw = 2 * HEAD_DIM
    qk_w = n_heads * hw
    conv_ch = conv_w.shape[2]
    n_pad = ROW_ALIGN - N_META
    lp = n_pad + N_META + seq
    tp = batch * lp
    f = w_down.shape[2]
    layer = 0
    lam_init = 0.8 - 0.6 * math.exp(-0.3 * layer)

    h0 = jnp.concatenate([jnp.zeros((batch, n_pad, d), F32),
                          jnp.broadcast_to(meta_tokens[None].astype(F32), (batch, N_META, d)),
                          x], axis=1).reshape(tp, d)

    tm_big = _pick(512, tp)
    u = _rmsnorm(h0, norm_mix_g[layer].astype(F32), _pick(256, tp))
    proj = _matmul(u, w_in[layer].astype(BF16), BF16, tm_big, _pick(1024, w_in.shape[2]), name="in_proj")

    tq = 384 if lp % 384 == 0 else ROW_ALIGN
    o = _attention(proj, lambda_q1[layer], lambda_k1[layer], lambda_q2[layer], lambda_k2[layer],
                   head_norm_g[layer], batch=batch, lp=lp, n_heads=n_heads, tq=tq, lam_init=lam_init,
                   n_pad=n_pad)
    ca_col = 3 * qk_w
    z = _conv_branch(proj, conv_w[layer], conv_b[layer], conv_ln_g[layer], conv_ln_b[layer],
                     ca_blk=ca_col // conv_ch, cg_blk=ca_col // conv_ch + 1, tm=_pick(256, tp))
    merged = _merge(o, z, proj, w_attn_out[layer].astype(BF16), w_conv_out[layer].astype(BF16),
                    b_conv_out[layer], b_gate[layer], gate_col=ca_col + 2 * conv_ch,
                    tm=tm_big, tn=_pick(512, d))
    h2 = _matmul(merged, w_out[layer].astype(BF16), F32, tm_big, _pick(1024, d), res=h0, name="out_proj")

    pos = np.arange(tp) % lp
    valid_np = pos >= n_pad
    valid = jnp.asarray(valid_np.astype(np.float32).reshape(1, tp))
    u_packed, top_e, gate_t, rank_t, cnt = _router(h2, norm_ffn_g[layer], w_router[layer], b_router[layer],
                                                   valid, tm=_pick(256, tp))
    counts = cnt[:, 0]
    padded = (counts + MOE_TILE - 1) // MOE_TILE * MOE_TILE
    pad_end = jnp.cumsum(padded)
    pad_start = pad_end - padded
    n_real = int(valid_np.sum()) * TOP_K
    n_tiles = -(-(n_real + N_EXPERTS * (MOE_TILE - 1)) // MOE_TILE)
    n_slots = n_tiles * MOE_TILE
    dump = n_slots + (np.cumsum(~valid_np) - 1)[None, :] * TOP_K + np.arange(TOP_K)[:, None]
    dest_t = jnp.where(jnp.asarray(valid_np)[None, :], pad_start[top_e] + rank_t, jnp.asarray(dump, jnp.int32))
    dest_flat = dest_t.T.reshape(-1).astype(jnp.int32)
    n_dump = int((~valid_np).sum()) * TOP_K
    n_used = (pad_end[-1] // MOE_TILE).astype(jnp.int32).reshape(1)
    tile_e = jnp.minimum(jnp.searchsorted(pad_end, jnp.arange(n_tiles, dtype=jnp.int32) * MOE_TILE, side="right"),
                         N_EXPERTS - 1).astype(jnp.int32)

    xs = _dispatch(dest_flat, u_packed, n_slots + n_dump, tm=ROW_ALIGN)
    act = _gate_up(tile_e, n_used, xs, w_gate_up[layer].astype(BF16),
                   b_gate_up[layer].reshape(N_EXPERTS, 1, 2 * f).astype(F32), n_tiles=n_tiles, tf=_pick(512, f))
    ys = _down(tile_e, n_used, act, w_down[layer].astype(BF16),
               b_down[layer].reshape(N_EXPERTS, 1, d).astype(F32), n_tiles=n_tiles, tn=_pick(1024, d))
    return _combine(dest_flat, h2, gate_t.T, final_norm_g, ys, batch=batch, seq=seq, lp=lp,
                    row0=ROW_ALIGN, tm=ROW_ALIGN)
```

```python
import functools
import math

import numpy as np
import jax
import jax.numpy as jnp
from jax import lax
from jax.experimental import pallas as pl
from jax.experimental.pallas import tpu as pltpu

N_META = 16
HEAD_DIM = 64
N_EXPERTS = 32
TOP_K = 4
CONV_K = 31
EPS = 1e-5
SWIGLU_LIMIT = 7.0
SWIGLU_ALPHA = 1.702
ROW_ALIGN = 128
CONV_HALO = 32
MOE_TILE = 256
VMEM_LIMIT = 56 * 1024 * 1024

F32 = jnp.float32
BF16 = jnp.bfloat16


def _cparams(n_axes):
    return pltpu.CompilerParams(dimension_semantics=("arbitrary",) * n_axes,
                                vmem_limit_bytes=VMEM_LIMIT)


def _rmsnorm_kernel(x_ref, g_ref, o_ref):
    x = x_ref[...]
    ms = jnp.mean(x * x, axis=-1, keepdims=True)
    o_ref[...] = (x * lax.rsqrt(ms + EPS) * g_ref[...]).astype(o_ref.dtype)


def _rmsnorm(x, g, tm):
    m, d = x.shape
    return pl.pallas_call(
        _rmsnorm_kernel,
        grid=(m // tm,),
        in_specs=[pl.BlockSpec((tm, d), lambda i: (i, 0)),
                  pl.BlockSpec((1, d), lambda i: (0, 0))],
        out_specs=pl.BlockSpec((tm, d), lambda i: (i, 0)),
        out_shape=jax.ShapeDtypeStruct((m, d), BF16),
        compiler_params=_cparams(1),
        name="rmsnorm",
    )(x, g.reshape(1, d))


def _cast_weight_once(w_ref, wb_sc):
    @pl.when(pl.program_id(1) == 0)
    def _():
        wb_sc[...] = w_ref[...].astype(BF16)


def _matmul_kernel(a_ref, w_ref, o_ref, wb_sc):
    _cast_weight_once(w_ref, wb_sc)
    o_ref[...] = jnp.dot(a_ref[...], wb_sc[...], preferred_element_type=F32).astype(o_ref.dtype)


def _matmul_res_kernel(a_ref, w_ref, r_ref, o_ref, wb_sc):
    _cast_weight_once(w_ref, wb_sc)
    acc = jnp.dot(a_ref[...], wb_sc[...], preferred_element_type=F32)
    o_ref[...] = (acc + r_ref[...]).astype(o_ref.dtype)


def _matmul(a, w, out_dtype, tm, tn, res=None, name="matmul"):
    m, k = a.shape
    n = w.shape[1]
    in_specs = [pl.BlockSpec((tm, k), lambda j, i: (i, 0)),
                pl.BlockSpec((k, tn), lambda j, i: (0, j))]
    args = [a, w]
    kern = _matmul_kernel
    if res is not None:
        in_specs.append(pl.BlockSpec((tm, tn), lambda j, i: (i, j)))
        args.append(res)
        kern = _matmul_res_kernel
    return pl.pallas_call(
        kern,
        grid=(n // tn, m // tm),
        in_specs=in_specs,
        out_specs=pl.BlockSpec((tm, tn), lambda j, i: (i, j)),
        out_shape=jax.ShapeDtypeStruct((m, n), out_dtype),
        scratch_shapes=[pltpu.VMEM((k, tn), BF16)],
        compiler_params=_cparams(2),
        name=name,
    )(*args)


def _attn_kernel(lq1_ref, lk1_ref, lq2_ref, lk2_ref, hg_ref, bias_ref, q_ref, k_ref, vt_ref, o_ref,
                 m_sc, l_sc, acc_sc, *, tq, lam_init):
    qi = pl.program_id(2)
    lam = (jnp.exp(jnp.sum(lq1_ref[...] * lk1_ref[...], axis=-1, keepdims=True))
           - jnp.exp(jnp.sum(lq2_ref[...] * lk2_ref[...], axis=-1, keepdims=True)) + lam_init)

    q = q_ref[...] * jnp.asarray(HEAD_DIM ** -0.5, BF16)
    lane = lax.broadcasted_iota(jnp.int32, q.shape, 1)
    zero = jnp.zeros_like(q)
    q12 = jnp.concatenate([jnp.where(lane < HEAD_DIM, q, zero), jnp.where(lane >= HEAD_DIM, q, zero)], axis=0)

    m_sc[...] = jnp.full(m_sc.shape, -jnp.inf, F32)
    l_sc[...] = jnp.zeros(l_sc.shape, F32)
    acc_sc[...] = jnp.zeros(acc_sc.shape, F32)

    def body(kj, c):
        k0 = pl.multiple_of(kj * tq, tq)
        kb = k_ref[pl.ds(k0, tq), :]
        vtb = vt_ref[:, pl.ds(k0, tq)]
        s = lax.dot_general(kb, q12, (((1,), (1,)), ((), ())), preferred_element_type=F32)
        kind = (kj == 0).astype(jnp.int32) + 2 * (kj == qi).astype(jnp.int32)
        bias = bias_ref[kind]
        s = s + jnp.concatenate([bias, bias], axis=1)
        m_prev = m_sc[...]
        m_new = jnp.maximum(m_prev, jnp.max(s, axis=0, keepdims=True))
        alpha = jnp.exp(m_prev - m_new)
        p = jnp.exp(s - m_new)
        l_sc[...] = alpha * l_sc[...] + jnp.sum(p.reshape(tq // 8, 8, 2 * tq), axis=0)
        acc_sc[...] = alpha * acc_sc[...] + jnp.dot(vtb, p.astype(BF16), preferred_element_type=F32)
        m_sc[...] = m_new
        return c

    lax.fori_loop(0, qi + 1, body, 0)

    o12 = acc_sc[...] / jnp.sum(l_sc[...], axis=0, keepdims=True)
    o = o12[:, :tq] - lam * o12[:, tq:]
    ms = jnp.mean(o * o, axis=0, keepdims=True)
    o = o * lax.rsqrt(ms + EPS) * hg_ref[...] * (1.0 - lam_init)
    o_ref[...] = o.T.astype(o_ref.dtype)


def _attn_bias(tq, n_pad):
    neg = np.float32(np.finfo(np.float32).min)
    r = np.arange(tq)[:, None]
    c = np.arange(tq)[None, :]
    pad = np.broadcast_to(r < n_pad, (tq, tq))
    future = r > c
    tiles = [np.zeros((tq, tq), bool), pad, future, pad | future]
    return jnp.asarray(np.stack([np.where(t, neg, np.float32(0)) for t in tiles]).astype(np.float32))


def _attention(proj, vt, lq1, lk1, lq2, lk2, head_g, *, batch, lp, n_heads, tq, lam_init, n_pad):
    hw = 2 * HEAD_DIM
    nq = lp // tq
    koff = n_heads
    vec = lambda a: a.reshape(1, -1).astype(F32)
    small = lambda n: pl.BlockSpec((1, n), lambda b, h, qi: (0, 0))
    return pl.pallas_call(
        functools.partial(_attn_kernel, tq=tq, lam_init=lam_init),
        grid=(batch, n_heads, nq),
        in_specs=[small(HEAD_DIM), small(HEAD_DIM), small(HEAD_DIM), small(HEAD_DIM),
                  pl.BlockSpec((hw, 1), lambda b, h, qi: (0, 0)),
                  pl.BlockSpec((4, tq, tq), lambda b, h, qi: (0, 0, 0)),
                  pl.BlockSpec((tq, hw), lambda b, h, qi: (b * nq + qi, h)),
                  pl.BlockSpec((lp, hw), lambda b, h, qi: (b, koff + h)),
                  pl.BlockSpec((None, None, hw, lp), lambda b, h, qi: (b, h, 0, 0))],
        out_specs=pl.BlockSpec((tq, hw), lambda b, h, qi: (b * nq + qi, h)),
        out_shape=jax.ShapeDtypeStruct((batch * lp, n_heads * hw), BF16),
        scratch_shapes=[pltpu.VMEM((1, 2 * tq), F32), pltpu.VMEM((8, 2 * tq), F32),
                        pltpu.VMEM((hw, 2 * tq), F32)],
        compiler_params=_cparams(3),
        name="diff_attention",
    )(vec(lq1), vec(lk1), vec(lq2), vec(lk2), head_g.reshape(hw, 1).astype(F32), _attn_bias(tq, n_pad),
      proj, proj, vt)


def _conv_kernel(ca_ref, cg_ref, ca_h_ref, cg_h_ref, w_ref, b_ref, lg_ref, lb_ref, z_ref, ext_sc, *, tm):
    i = pl.program_id(0)
    glu = lambda a, g: a.astype(F32) * jax.nn.sigmoid(g.astype(F32))
    halo = glu(ca_h_ref[...], cg_h_ref[...])
    ext_sc[0:CONV_HALO, :] = jnp.where(i > 0, halo, jnp.zeros_like(halo))
    ext_sc[CONV_HALO:CONV_HALO + tm, :] = glu(ca_ref[...], cg_ref[...])
    w = w_ref[...]
    acc = jnp.zeros((tm, w.shape[1]), F32) + b_ref[...]
    base = CONV_HALO - (CONV_K - 1)
    for j in range(CONV_K):
        acc = acc + w[j:j + 1, :] * ext_sc[base + j:base + j + tm, :]
    mu = jnp.mean(acc, axis=-1, keepdims=True)
    d = acc - mu
    var = jnp.mean(d * d, axis=-1, keepdims=True)
    y = d * lax.rsqrt(var + EPS) * lg_ref[...] + lb_ref[...]
    z_ref[...] = (y * jax.nn.sigmoid(y)).astype(z_ref.dtype)


def _conv_branch(proj, conv_w, conv_b, ln_g, ln_b, *, ca_blk, cg_blk, tm):
    m = proj.shape[0]
    c = conv_w.shape[1]
    hb = tm // CONV_HALO
    row = lambda a: a.reshape(1, c).astype(F32)
    vec = pl.BlockSpec((1, c), lambda i: (0, 0))
    return pl.pallas_call(
        functools.partial(_conv_kernel, tm=tm),
        grid=(m // tm,),
        in_specs=[pl.BlockSpec((tm, c), lambda i: (i, ca_blk)),
                  pl.BlockSpec((tm, c), lambda i: (i, cg_blk)),
                  pl.BlockSpec((CONV_HALO, c), lambda i: (jnp.maximum(i * hb - 1, 0), ca_blk)),
                  pl.BlockSpec((CONV_HALO, c), lambda i: (jnp.maximum(i * hb - 1, 0), cg_blk)),
                  pl.BlockSpec((CONV_K, c), lambda i: (0, 0)),
                  vec, vec, vec],
        out_specs=pl.BlockSpec((tm, c), lambda i: (i, 0)),
        out_shape=jax.ShapeDtypeStruct((m, c), BF16),
        scratch_shapes=[pltpu.VMEM((CONV_HALO + tm, c), F32)],
        compiler_params=_cparams(1),
        name="conformer_conv",
    )(proj, proj, proj, proj, conv_w.astype(F32), row(conv_b), row(ln_g), row(ln_b))


def _merge_kernel(o_ref, z_ref, wa_ref, wc_ref, bc_ref, g1_ref, g2_ref, bg1_ref, bg2_ref, out_ref,
                  wab_sc, wcb_sc):
    _cast_weight_once(wa_ref, wab_sc)
    _cast_weight_once(wc_ref, wcb_sc)
    ya = jnp.dot(o_ref[...], wab_sc[...], preferred_element_type=F32)
    yc = jnp.dot(z_ref[...], wcb_sc[...], preferred_element_type=F32) + bc_ref[...]
    g1 = jax.nn.sigmoid(g1_ref[...].astype(F32) + bg1_ref[...])
    g2 = jax.nn.sigmoid(g2_ref[...].astype(F32) + bg2_ref[...])
    out_ref[...] = (g1 * ya + g2 * yc).astype(out_ref.dtype)


def _merge(o, z, proj, wa, wc, bc, bg, *, gate_col, tm, tn):
    m, ka = o.shape
    kc = z.shape[1]
    d = wa.shape[1]
    g1_blk = gate_col // tn
    g2_blk = (gate_col + d) // tn
    nb = d // tn
    bg2 = bg.reshape(1, 2 * d).astype(F32)
    return pl.pallas_call(
        _merge_kernel,
        grid=(nb, m // tm),
        in_specs=[pl.BlockSpec((tm, ka), lambda j, i: (i, 0)),
                  pl.BlockSpec((tm, kc), lambda j, i: (i, 0)),
                  pl.BlockSpec((ka, tn), lambda j, i: (0, j)),
                  pl.BlockSpec((kc, tn), lambda j, i: (0, j)),
                  pl.BlockSpec((1, tn), lambda j, i: (0, j)),
                  pl.BlockSpec((tm, tn), lambda j, i: (i, g1_blk + j)),
                  pl.BlockSpec((tm, tn), lambda j, i: (i, g2_blk + j)),
                  pl.BlockSpec((1, tn), lambda j, i: (0, j)),
                  pl.BlockSpec((1, tn), lambda j, i: (0, nb + j))],
        out_specs=pl.BlockSpec((tm, tn), lambda j, i: (i, j)),
        out_shape=jax.ShapeDtypeStruct((m, d), BF16),
        scratch_shapes=[pltpu.VMEM((ka, tn), BF16), pltpu.VMEM((kc, tn), BF16)],
        compiler_params=_cparams(2),
        name="mixer_merge",
    )(o, z, wa, wc, bc.reshape(1, d).astype(F32), proj, proj, bg2, bg2)


def _router_kernel(h_ref, g_ref, wr_ref, br_ref, valid_ref,
                   up_ref, e_ref, gate_ref, rank_ref, cnt_ref, carry_sc, *, tm):
    i = pl.program_id(0)

    @pl.when(i == 0)
    def _():
        carry_sc[...] = jnp.zeros(carry_sc.shape, F32)

    h = h_ref[...]
    ms = jnp.mean(h * h, axis=-1, keepdims=True)
    u = h * lax.rsqrt(ms + EPS) * g_ref[...]

    half = u.shape[1] // 2
    bits = lax.bitcast_convert_type(u.astype(BF16).astype(F32), jnp.uint32)
    up_ref[...] = (bits[:, half:] & jnp.uint32(0xFFFF0000)) | (bits[:, :half] >> 16)

    logits = lax.dot_general(wr_ref[...], u, (((1,), (1,)), ((), ())),
                             precision=lax.Precision.HIGHEST,
                             preferred_element_type=F32) + br_ref[...]
    n_e = logits.shape[0]
    eiota = lax.broadcasted_iota(jnp.int32, logits.shape, 0).astype(F32)
    work = logits
    sel = jnp.zeros(logits.shape, jnp.bool_)
    top_l, top_e = [], []
    for _ in range(TOP_K):
        mx = jnp.max(work, axis=0, keepdims=True)
        idx = jnp.min(jnp.where(work == mx, eiota, float(n_e)), axis=0, keepdims=True)
        hit = eiota == idx
        top_l.append(mx)
        top_e.append(idx)
        sel = sel | hit
        work = jnp.where(hit, -jnp.inf, work)
    ex = [jnp.exp(t - top_l[0]) for t in top_l]
    den = ex[0] + ex[1] + ex[2] + ex[3]
    gate_ref[...] = jnp.concatenate([e / den for e in ex], axis=0)
    e_ref[...] = jnp.concatenate(top_e, axis=0).astype(jnp.int32)

    selv = jnp.where(sel & (valid_ref[...] > 0.0), 1.0, 0.0)
    before = (lax.broadcasted_iota(jnp.int32, (tm, tm), 0)
              < lax.broadcasted_iota(jnp.int32, (tm, tm), 1)).astype(BF16)
    rank_all = jnp.dot(selv.astype(BF16), before, preferred_element_type=F32) + carry_sc[...]
    ranks = [jnp.sum(jnp.where(eiota == idx, rank_all, 0.0), axis=0, keepdims=True) for idx in top_e]
    rank_ref[...] = jnp.concatenate(ranks, axis=0).astype(jnp.int32)
    carry = carry_sc[...] + jnp.sum(selv, axis=1, keepdims=True)
    carry_sc[...] = carry
    cnt_ref[...] = jnp.broadcast_to(carry, cnt_ref.shape).astype(jnp.int32)


def _router(h2, g, w_router, b_router, valid, *, tm):
    m, d = h2.shape
    n_e = w_router.shape[1]
    tok = lambda dt: jax.ShapeDtypeStruct((TOP_K, m), dt)
    tok_spec = pl.BlockSpec((TOP_K, tm), lambda i: (0, i))
    return pl.pallas_call(
        functools.partial(_router_kernel, tm=tm),
        grid=(m // tm,),
        in_specs=[pl.BlockSpec((tm, d), lambda i: (i, 0)),
                  pl.BlockSpec((1, d), lambda i: (0, 0)),
                  pl.BlockSpec((n_e, d), lambda i: (0, 0)),
                  pl.BlockSpec((n_e, 1), lambda i: (0, 0)),
                  pl.BlockSpec((1, tm), lambda i: (0, i))],
        out_specs=[pl.BlockSpec((tm, d // 2), lambda i: (i, 0)),
                   tok_spec, tok_spec, tok_spec,
                   pl.BlockSpec((n_e, 128), lambda i: (0, 0))],
        out_shape=[jax.ShapeDtypeStruct((m, d // 2), jnp.uint32),
                   tok(jnp.int32), tok(F32), tok(jnp.int32),
                   jax.ShapeDtypeStruct((n_e, 128), jnp.int32)],
        scratch_shapes=[pltpu.VMEM((n_e, 1), F32)],
        compiler_params=_cparams(1),
        name="router",
    )(h2, g.reshape(1, d).astype(F32), w_router.T.astype(F32), b_router.reshape(n_e, 1).astype(F32), valid)


def _dispatch_kernel(dest_ref, u_ref, xs_in_ref, xs_ref, sem, *, tm):
    del xs_in_ref
    i = pl.program_id(0)

    def row_copy(r, k):
        d = dest_ref[(i * tm + r) * TOP_K + k]
        return pltpu.make_async_copy(u_ref.at[pl.ds(r, 1), :], xs_ref.at[pl.ds(d, 1), :], sem)

    def start(r, c):
        for k in range(TOP_K):
            row_copy(r, k).start()
        return c

    def wait(r, c):
        for k in range(TOP_K):
            row_copy(r, k).wait()
        return c

    lax.fori_loop(0, tm, start, 0)
    lax.fori_loop(0, tm, wait, 0)


def _dispatch(dest_flat, u_packed, n_rows, *, tm):
    m, w = u_packed.shape
    xs0 = jnp.zeros((n_rows, w), jnp.uint32)
    grid_spec = pltpu.PrefetchScalarGridSpec(
        num_scalar_prefetch=1,
        grid=(m // tm,),
        in_specs=[pl.BlockSpec((tm, w), lambda i, dest: (i, 0)),
                  pl.BlockSpec(memory_space=pl.ANY)],
        out_specs=pl.BlockSpec(memory_space=pl.ANY),
        scratch_shapes=[pltpu.SemaphoreType.DMA(())],
    )
    return pl.pallas_call(
        functools.partial(_dispatch_kernel, tm=tm),
        grid_spec=grid_spec,
        out_shape=jax.ShapeDtypeStruct((n_rows, w), jnp.uint32),
        input_output_aliases={2: 0},
        compiler_params=_cparams(1),
        name="moe_dispatch",
    )(dest_flat, u_packed, xs0)


def _new_expert(te_ref, nu_ref, r):
    return (r < nu_ref[0]) & ((r == 0) | (te_ref[r] != te_ref[jnp.maximum(r - 1, 0)]))


def _gate_up_kernel(te_ref, nu_ref, xs_ref, wg_ref, wu_ref, bg_ref, bu_ref, act_ref, wgb_sc, wub_sc):
    r = pl.program_id(1)

    @pl.when(_new_expert(te_ref, nu_ref, r))
    def _():
        wgb_sc[...] = wg_ref[...].astype(BF16)
        wub_sc[...] = wu_ref[...].astype(BF16)

    @pl.when(r < nu_ref[0])
    def _():
        w = xs_ref[...]
        half = w.shape[1]
        lo = lax.bitcast_convert_type(w << 16, F32).astype(BF16)
        hi = lax.bitcast_convert_type(w & jnp.uint32(0xFFFF0000), F32).astype(BF16)
        g = (jnp.dot(lo, wgb_sc[:half, :], preferred_element_type=F32)
             + jnp.dot(hi, wgb_sc[half:, :], preferred_element_type=F32) + bg_ref[...])
        u = (jnp.dot(lo, wub_sc[:half, :], preferred_element_type=F32)
             + jnp.dot(hi, wub_sc[half:, :], preferred_element_type=F32) + bu_ref[...])
        g = jnp.minimum(g, SWIGLU_LIMIT)
        u = jnp.clip(u, -SWIGLU_LIMIT, SWIGLU_LIMIT)
        act_ref[...] = ((u + 1.0) * (g * jax.nn.sigmoid(SWIGLU_ALPHA * g))).astype(act_ref.dtype)

    @pl.when(r >= nu_ref[0])
    def _():
        act_ref[...] = jnp.zeros(act_ref.shape, act_ref.dtype)


def _gate_up(tile_e, n_used, xs, w_gu, b_gu, *, n_tiles, tf):
    n_e, d, f2 = w_gu.shape
    f = f2 // 2
    nj = f // tf
    rr = lambda r, nu: jnp.minimum(r, nu[0] - 1)
    grid_spec = pltpu.PrefetchScalarGridSpec(
        num_scalar_prefetch=2,
        grid=(nj, n_tiles),
        in_specs=[pl.BlockSpec((MOE_TILE, d // 2), lambda j, r, te, nu: (rr(r, nu), 0)),
                  pl.BlockSpec((None, d, tf), lambda j, r, te, nu: (te[rr(r, nu)], 0, j)),
                  pl.BlockSpec((None, d, tf), lambda j, r, te, nu: (te[rr(r, nu)], 0, nj + j)),
                  pl.BlockSpec((None, 1, tf), lambda j, r, te, nu: (te[rr(r, nu)], 0, j)),
                  pl.BlockSpec((None, 1, tf), lambda j, r, te, nu: (te[rr(r, nu)], 0, nj + j))],
        out_specs=pl.BlockSpec((MOE_TILE, tf), lambda j, r, te, nu: (r, j)),
        scratch_shapes=[pltpu.VMEM((d, tf), BF16), pltpu.VMEM((d, tf), BF16)],
    )
    return pl.pallas_call(
        _gate_up_kernel,
        grid_spec=grid_spec,
        out_shape=jax.ShapeDtypeStruct((n_tiles * MOE_TILE, f), BF16),
        compiler_params=_cparams(2),
        name="moe_gate_up",
    )(tile_e, n_used, xs, w_gu, w_gu, b_gu, b_gu)


def _down_kernel(te_ref, nu_ref, act_ref, wd_ref, bd_ref, y_ref, wdb_sc):
    r = pl.program_id(1)

    @pl.when(_new_expert(te_ref, nu_ref, r))
    def _():
        wdb_sc[...] = wd_ref[...].astype(BF16)

    @pl.when(r < nu_ref[0])
    def _():
        y_ref[...] = jnp.dot(act_ref[...], wdb_sc[...], preferred_element_type=F32) + bd_ref[...]

    @pl.when(r >= nu_ref[0])
    def _():
        y_ref[...] = jnp.zeros(y_ref.shape, y_ref.dtype)


def _down(tile_e, n_used, act, w_d, b_d, *, n_tiles, tn):
    n_e, f, d = w_d.shape
    rr = lambda r, nu: jnp.minimum(r, nu[0] - 1)
    grid_spec = pltpu.PrefetchScalarGridSpec(
        num_scalar_prefetch=2,
        grid=(d // tn, n_tiles),
        in_specs=[pl.BlockSpec((MOE_TILE, f), lambda j, r, te, nu: (rr(r, nu), 0)),
                  pl.BlockSpec((None, f, tn), lambda j, r, te, nu: (te[rr(r, nu)], 0, j)),
                  pl.BlockSpec((None, 1, tn), lambda j, r, te, nu: (te[rr(r, nu)], 0, j))],
        out_specs=pl.BlockSpec((MOE_TILE, tn), lambda j, r, te, nu: (r, j)),
        scratch_shapes=[pltpu.VMEM((f, tn), BF16)],
    )
    return pl.pallas_call(
        _down_kernel,
        grid_spec=grid_spec,
        out_shape=jax.ShapeDtypeStruct((n_tiles * MOE_TILE, d), F32),
        compiler_params=_cparams(2),
        name="moe_down",
    )(tile_e, n_used, act, w_d, b_d)


def _combine_kernel(dest_ref, h_ref, gate_ref, g_ref, ys_ref, out_ref, buf, sem, *, tm, lp, row0):
    b = pl.program_id(0)
    si = pl.program_id(1)
    t0 = b * lp + row0 + si * tm

    def row_copy(r, k):
        d = dest_ref[(t0 + r) * TOP_K + k]
        return pltpu.make_async_copy(ys_ref.at[pl.ds(d, 1), :], buf.at[k, pl.ds(r, 1), :], sem)

    def start(r, c):
        for k in range(TOP_K):
            row_copy(r, k).start()
        return c

    def wait(r, c):
        for k in range(TOP_K):
            row_copy(r, k).wait()
        return c

    lax.fori_loop(0, tm, start, 0)
    lax.fori_loop(0, tm, wait, 0)

    gate = gate_ref[...]
    acc = jnp.zeros(h_ref.shape, F32)
    for k in range(TOP_K):
        acc = acc + gate[:, k:k + 1] * buf[k]
    h = h_ref[...] + acc
    ms = jnp.mean(h * h, axis=-1, keepdims=True)
    out_ref[...] = (h * lax.rsqrt(ms + EPS) * g_ref[...]).astype(out_ref.dtype)


def _combine(dest_flat, h2, gate, g, ys, *, batch, seq, lp, row0, tm):
    d = h2.shape[1]
    nb_b = lp // tm
    nb0 = row0 // tm
    grid_spec = pltpu.PrefetchScalarGridSpec(
        num_scalar_prefetch=1,
        grid=(batch, seq // tm),
        in_specs=[pl.BlockSpec((tm, d), lambda b, s, dest: (b * nb_b + nb0 + s, 0)),
                  pl.BlockSpec((tm, TOP_K), lambda b, s, dest: (b * nb_b + nb0 + s, 0)),
                  pl.BlockSpec((1, d), lambda b, s, dest: (0, 0)),
                  pl.BlockSpec(memory_space=pl.ANY)],
        out_specs=pl.BlockSpec((None, tm, d), lambda b, s, dest: (b, s, 0)),
        scratch_shapes=[pltpu.VMEM((TOP_K, tm, d), F32), pltpu.SemaphoreType.DMA(())],
    )
    return pl.pallas_call(
        functools.partial(_combine_kernel, tm=tm, lp=lp, row0=row0),
        grid_spec=grid_spec,
        out_shape=jax.ShapeDtypeStruct((batch, seq, d), F32),
        compiler_params=_cparams(2),
        name="moe_combine",
    )(dest_flat, h2, gate, g.reshape(1, d).astype(F32), ys)


def _pick(pref, n):
    t = pref
    while n % t:
        t //= 2
    return t


def kernel(x, meta_tokens, norm_mix_g, w_in, b_gate, lambda_q1, lambda_k1, lambda_q2, lambda_k2, head_norm_g, w_attn_out, conv_w, conv_b, conv_ln_g, conv_ln_b, w_conv_out, b_conv_out, w_out, norm_ffn_g, w_router, b_router, w_gate_up, b_gate_up, w_down, b_down, final_norm_g):
    batch, seq, d = x.shape
    depth = w_in.shape[0]
    assert depth == 1 and seq % ROW_ALIGN == 0 and N_META <= ROW_ALIGN
    n_heads = d // 256
    hw = 2 * HEAD_DIM
    qk_w = n_heads * hw
    conv_ch = conv_w.shape[2]
    n_pad = ROW_ALIGN - N_META
    lp = n_pad + N_META + seq
    tp = batch * lp
    f = w_down.shape[2]
    layer = 0
    lam_init = 0.8 - 0.6 * math.exp(-0.3 * layer)

    h0 = jnp.concatenate([jnp.zeros((batch, n_pad, d), F32),
                          jnp.broadcast_to(meta_tokens[None].astype(F32), (batch, N_META, d)),
                          x], axis=1).reshape(tp, d)

    tm_big = _pick(512, tp)
    u = _rmsnorm(h0, norm_mix_g[layer].astype(F32), _pick(256, tp))
    proj = _matmul(u, w_in[layer], BF16, tm_big, _pick(512, w_in.shape[2]), name="in_proj")

    tq = 384 if lp % 384 == 0 else ROW_ALIGN
    vt = proj[:, 2 * qk_w:3 * qk_w].reshape(batch, lp, n_heads, hw).transpose(0, 2, 3, 1)
    o = _attention(proj, vt, lambda_q1[layer], lambda_k1[layer], lambda_q2[layer], lambda_k2[layer],
                   head_norm_g[layer], batch=batch, lp=lp, n_heads=n_heads, tq=tq, lam_init=lam_init,
                   n_pad=n_pad)
    ca_col = 3 * qk_w
    z = _conv_branch(proj, conv_w[layer], conv_b[layer], conv_ln_g[layer], conv_ln_b[layer],
                     ca_blk=ca_col // conv_ch, cg_blk=ca_col // conv_ch + 1, tm=_pick(256, tp))
    merged = _merge(o, z, proj, w_attn_out[layer], w_conv_out[layer],
                    b_conv_out[layer], b_gate[layer], gate_col=ca_col + 2 * conv_ch,
                    tm=tm_big, tn=_pick(512, d))
    h2 = _matmul(merged, w_out[layer], F32, tm_big, _pick(512, d), res=h0, name="out_proj")

    pos = np.arange(tp) % lp
    valid_np = pos >= n_pad
    valid = jnp.asarray(valid_np.astype(np.float32).reshape(1, tp))
    u_packed, top_e, gate_t, rank_t, cnt = _router(h2, norm_ffn_g[layer], w_router[layer], b_router[layer],
                                                   valid, tm=_pick(256, tp))
    counts = cnt[:, 0]
    padded = (counts + MOE_TILE - 1) // MOE_TILE * MOE_TILE
    e_ids = np.arange(N_EXPERTS)
    pad_end = jnp.sum(jnp.where(jnp.asarray(e_ids[None, :] <= e_ids[:, None]), padded[None, :], 0), axis=1)
    pad_start = pad_end - padded
    start_tok = jnp.sum(jnp.where(top_e[:, :, None] == jnp.asarray(e_ids, jnp.int32), pad_start, 0), axis=-1)
    n_real = int(valid_np.sum()) * TOP_K
    n_tiles = -(-(n_real + N_EXPERTS * (MOE_TILE - 1)) // MOE_TILE)
    n_slots = n_tiles * MOE_TILE
    dump = n_slots + (np.cumsum(~valid_np) - 1)[None, :] * TOP_K + np.arange(TOP_K)[:, None]
    dest_t = jnp.where(jnp.asarray(valid_np)[None, :], start_tok + rank_t, jnp.asarray(dump, jnp.int32))
    dest_flat = dest_t.T.reshape(-1).astype(jnp.int32)
    n_dump = int((~valid_np).sum()) * TOP_K
    n_used = (pad_end[-1] // MOE_TILE).astype(jnp.int32).reshape(1)
    tile_start = jnp.asarray(np.arange(n_tiles, dtype=np.int32) * MOE_TILE)
    tile_e = jnp.minimum(jnp.sum((pad_end[None, :] <= tile_start[:, None]).astype(jnp.int32), axis=1),
                         N_EXPERTS - 1).astype(jnp.int32)

    xs = _dispatch(dest_flat, u_packed, n_slots + n_dump, tm=ROW_ALIGN)
    act = _gate_up(tile_e, n_used, xs, w_gate_up[layer],
                   b_gate_up[layer].reshape(N_EXPERTS, 1, 2 * f).astype(F32), n_tiles=n_tiles, tf=_pick(512, f))
    ys = _down(tile_e, n_used, act, w_down[layer],
               b_down[layer].reshape(N_EXPERTS, 1, d).astype(F32), n_tiles=n_tiles, tn=_pick(2048, d))
    return _combine(dest_flat, h2, gate_t.T, final_norm_g, ys, batch=batch, seq=seq, lp=lp,
                    row0=ROW_ALIGN, tm=ROW_ALIGN)
```

```python
import functools
import math

import numpy as np
import jax
import jax.numpy as jnp
from jax import lax
from jax.experimental import pallas as pl
from jax.experimental.pallas import tpu as pltpu

N_META = 16
HEAD_DIM = 64
N_EXPERTS = 32
TOP_K = 4
CONV_K = 31
EPS = 1e-5
SWIGLU_LIMIT = 7.0
SWIGLU_ALPHA = 1.702
SUBLANES = 8
LANES = 128
ROW_ALIGN = 128
CONV_HALO = 32
MOE_TILE = 256
VMEM_LIMIT = 56 * 1024 * 1024

F32 = jnp.float32
BF16 = jnp.bfloat16


def _cparams(n_axes):
    return pltpu.CompilerParams(dimension_semantics=("arbitrary",) * n_axes,
                                vmem_limit_bytes=VMEM_LIMIT)


def _embed_norm_kernel(x_ref, meta_ref, g_ref, h_ref, u_ref, *, n_pad):
    i = pl.program_id(1)

    @pl.when(i == 0)
    def _():
        h_ref[0:n_pad, :] = jnp.zeros((n_pad, h_ref.shape[1]), F32)
        h_ref[n_pad:, :] = meta_ref[...]

    @pl.when(i > 0)
    def _():
        h_ref[...] = x_ref[...]

    h = h_ref[...]
    ms = jnp.mean(h * h, axis=-1, keepdims=True)
    u_ref[...] = (h * lax.rsqrt(ms + EPS) * g_ref[...]).astype(u_ref.dtype)


def _embed_norm(x, meta, g, *, n_pad):
    batch, seq, d = x.shape
    tm = ROW_ALIGN
    nb = (n_pad + N_META + seq) // tm
    out_spec = pl.BlockSpec((tm, d), lambda b, i: (b * nb + i, 0))
    return pl.pallas_call(
        functools.partial(_embed_norm_kernel, n_pad=n_pad),
        grid=(batch, nb),
        in_specs=[pl.BlockSpec((None, tm, d), lambda b, i: (b, jnp.maximum(i - 1, 0), 0)),
                  pl.BlockSpec((N_META, d), lambda b, i: (0, 0)),
                  pl.BlockSpec((1, d), lambda b, i: (0, 0))],
        out_specs=[out_spec, out_spec],
        out_shape=[jax.ShapeDtypeStruct((batch * nb * tm, d), F32),
                   jax.ShapeDtypeStruct((batch * nb * tm, d), BF16)],
        compiler_params=_cparams(2),
        name="embed_norm",
    )(x, meta.astype(F32), g.reshape(1, d).astype(F32))


def _cast_weight_once(w_ref, wb_sc):
    @pl.when(pl.program_id(1) == 0)
    def _():
        wb_sc[...] = w_ref[...].astype(BF16)


def _matmul_kernel(a_ref, w_ref, o_ref, wb_sc):
    _cast_weight_once(w_ref, wb_sc)
    o_ref[...] = jnp.dot(a_ref[...], wb_sc[...], preferred_element_type=F32).astype(o_ref.dtype)


def _matmul_res_kernel(a_ref, w_ref, r_ref, o_ref, wb_sc):
    _cast_weight_once(w_ref, wb_sc)
    acc = jnp.dot(a_ref[...], wb_sc[...], preferred_element_type=F32)
    o_ref[...] = (acc + r_ref[...]).astype(o_ref.dtype)


def _matmul(a, w, out_dtype, tm, tn, res=None, name="matmul"):
    m, k = a.shape
    n = w.shape[1]
    in_specs = [pl.BlockSpec((tm, k), lambda j, i: (i, 0)),
                pl.BlockSpec((k, tn), lambda j, i: (0, j))]
    args = [a, w]
    kern = _matmul_kernel
    if res is not None:
        in_specs.append(pl.BlockSpec((tm, tn), lambda j, i: (i, j)))
        args.append(res)
        kern = _matmul_res_kernel
    return pl.pallas_call(
        kern,
        grid=(n // tn, m // tm),
        in_specs=in_specs,
        out_specs=pl.BlockSpec((tm, tn), lambda j, i: (i, j)),
        out_shape=jax.ShapeDtypeStruct((m, n), out_dtype),
        scratch_shapes=[pltpu.VMEM((k, tn), BF16)],
        compiler_params=_cparams(2),
        name=name,
    )(*args)


def _attn_kernel(lq1_ref, lk1_ref, lq2_ref, lk2_ref, hg_ref, bias_ref, q_ref, k_ref, vt_ref, o_ref,
                 q12_sc, sa_sc, sb_sc, m_sc, l_sc, acc_sc, *, tq, lam_init):
    nq = q_ref.shape[0] // tq
    lam = (jnp.exp(jnp.sum(lq1_ref[...] * lk1_ref[...], axis=-1, keepdims=True))
           - jnp.exp(jnp.sum(lq2_ref[...] * lk2_ref[...], axis=-1, keepdims=True)) + lam_init)

    def q_tile(qi, carry):
        q0 = pl.multiple_of(qi * tq, tq)
        q = q_ref[pl.ds(q0, tq), :] * jnp.asarray(HEAD_DIM ** -0.5, BF16)
        lane = lax.broadcasted_iota(jnp.int32, q.shape, 1)
        zero = jnp.zeros_like(q)
        q12_sc[0:tq, :] = jnp.where(lane < HEAD_DIM, q, zero)
        q12_sc[tq:2 * tq, :] = jnp.where(lane >= HEAD_DIM, q, zero)
        m_sc[...] = jnp.full(m_sc.shape, -jnp.inf, F32)
        l_sc[...] = jnp.zeros(l_sc.shape, F32)
        acc_sc[...] = jnp.zeros(acc_sc.shape, F32)

        def scores(kj, mp, s_ref):
            k0 = pl.multiple_of(kj * tq, tq)
            s = lax.dot_general(k_ref[pl.ds(k0, tq), :], q12_sc[mp * tq:(mp + 1) * tq, :],
                                (((1,), (1,)), ((), ())), preferred_element_type=F32)
            kind = jnp.where(kj == 0, 1, 0) + jnp.where(kj == qi, 2, 0)
            s_ref[...] = s + bias_ref[kind]

        def update(kj, mp, s_ref):
            k0 = pl.multiple_of(kj * tq, tq)
            s = s_ref[...]
            m_prev = m_sc[mp]
            m_new = jnp.maximum(m_prev, jnp.max(s, axis=0, keepdims=True))
            alpha = jnp.exp(m_prev - m_new)
            p = jnp.exp(s - m_new)
            l_sc[mp] = alpha * l_sc[mp] + jnp.sum(p.reshape(tq // 8, 8, tq), axis=0)
            acc_sc[mp] = alpha * acc_sc[mp] + jnp.dot(vt_ref[:, pl.ds(k0, tq)], p.astype(BF16),
                                                      preferred_element_type=F32)
            m_sc[mp] = m_new

        scores(0, 0, sa_sc)

        def body(kj, c):
            scores(kj, 1, sb_sc)
            update(kj, 0, sa_sc)
            scores(jnp.minimum(kj + 1, qi), 0, sa_sc)
            update(kj, 1, sb_sc)
            return c

        lax.fori_loop(0, qi + 1, body, 0)

        o1 = acc_sc[0] / jnp.sum(l_sc[0], axis=0, keepdims=True)
        o2 = acc_sc[1] / jnp.sum(l_sc[1], axis=0, keepdims=True)
        o = o1 - lam * o2
        ms = jnp.mean(o * o, axis=0, keepdims=True)
        o = o * lax.rsqrt(ms + EPS) * hg_ref[...] * (1.0 - lam_init)
        o_ref[pl.ds(q0, tq), :] = o.T.astype(o_ref.dtype)
        return carry

    lax.fori_loop(0, nq, q_tile, 0)


def _attn_bias(tq, n_pad):
    neg = np.float32(np.finfo(np.float32).min)
    r = np.arange(tq)[:, None]
    c = np.arange(tq)[None, :]
    pad = np.broadcast_to(r < n_pad, (tq, tq))
    future = r > c
    tiles = [np.zeros((tq, tq), bool), pad, future, pad | future]
    return jnp.asarray(np.stack([np.where(t, neg, np.float32(0)) for t in tiles]).astype(np.float32))


def _attention(proj, vt, lq1, lk1, lq2, lk2, head_g, *, batch, lp, n_heads, tq, lam_init, n_pad):
    hw = 2 * HEAD_DIM
    assert n_pad <= tq
    koff = n_heads
    vec = lambda a: a.reshape(1, -1).astype(F32)
    small = lambda n: pl.BlockSpec((1, n), lambda b, h: (0, 0))
    return pl.pallas_call(
        functools.partial(_attn_kernel, tq=tq, lam_init=lam_init),
        grid=(batch, n_heads),
        in_specs=[small(HEAD_DIM), small(HEAD_DIM), small(HEAD_DIM), small(HEAD_DIM),
                  pl.BlockSpec((hw, 1), lambda b, h: (0, 0)),
                  pl.BlockSpec((4, tq, tq), lambda b, h: (0, 0, 0)),
                  pl.BlockSpec((lp, hw), lambda b, h: (b, h)),
                  pl.BlockSpec((lp, hw), lambda b, h: (b, koff + h)),
                  pl.BlockSpec((None, None, hw, lp), lambda b, h: (b, h, 0, 0))],
        out_specs=pl.BlockSpec((lp, hw), lambda b, h: (b, h)),
        out_shape=jax.ShapeDtypeStruct((batch * lp, n_heads * hw), BF16),
        scratch_shapes=[pltpu.VMEM((2 * tq, hw), BF16),
                        pltpu.VMEM((tq, tq), F32), pltpu.VMEM((tq, tq), F32),
                        pltpu.VMEM((2, 1, tq), F32), pltpu.VMEM((2, 8, tq), F32),
                        pltpu.VMEM((2, hw, tq), F32)],
        compiler_params=_cparams(2),
        name="diff_attention",
    )(vec(lq1), vec(lk1), vec(lq2), vec(lk2), head_g.reshape(hw, 1).astype(F32), _attn_bias(tq, n_pad),
      proj, proj, vt)


def _conv_kernel(ca_ref, cg_ref, ca_h_ref, cg_h_ref, w_ref, b_ref, lg_ref, lb_ref, z_ref,
                 ext_sc, sh_sc, y_sc, *, tm):
    i = pl.program_id(0)
    n_ch = w_ref.shape[1]
    n_ext = CONV_HALO + tm
    glu = lambda a, g: a.astype(F32) * jax.nn.sigmoid(g.astype(F32))
    halo = glu(ca_h_ref[...], cg_h_ref[...])
    ext_sc[0:CONV_HALO, :] = jnp.where(i > 0, halo, jnp.zeros_like(halo))
    ext_sc[CONV_HALO:n_ext, :] = glu(ca_ref[...], cg_ref[...])
    ext_sc[n_ext:n_ext + SUBLANES, :] = jnp.zeros((SUBLANES, n_ch), F32)
    base = CONV_HALO - (CONV_K - 1)

    def slab(lc, carry):
        l0 = pl.multiple_of(lc * LANES, LANES)
        for rho in range(SUBLANES):
            sh_sc[rho] = ext_sc[rho:rho + n_ext, pl.ds(l0, LANES)]
        acc = jnp.zeros((tm, LANES), F32) + b_ref[:, pl.ds(l0, LANES)]
        for j in range(CONV_K):
            rho = (base + j) % SUBLANES
            a = base + j - rho
            acc = acc + w_ref[j:j + 1, pl.ds(l0, LANES)] * sh_sc[rho, a:a + tm, :]
        y_sc[:, pl.ds(l0, LANES)] = acc
        return carry

    lax.fori_loop(0, n_ch // LANES, slab, 0)
    acc = y_sc[...]
    mu = jnp.mean(acc, axis=-1, keepdims=True)
    d = acc - mu
    var = jnp.mean(d * d, axis=-1, keepdims=True)
    y = d * lax.rsqrt(var + EPS) * lg_ref[...] + lb_ref[...]
    z_ref[...] = (y * jax.nn.sigmoid(y)).astype(z_ref.dtype)


def _conv_branch(proj, conv_w, conv_b, ln_g, ln_b, *, ca_blk, cg_blk, tm):
    m = proj.shape[0]
    c = conv_w.shape[1]
    hb = tm // CONV_HALO
    row = lambda a: a.reshape(1, c).astype(F32)
    vec = pl.BlockSpec((1, c), lambda i: (0, 0))
    return pl.pallas_call(
        functools.partial(_conv_kernel, tm=tm),
        grid=(m // tm,),
        in_specs=[pl.BlockSpec((tm, c), lambda i: (i, ca_blk)),
                  pl.BlockSpec((tm, c), lambda i: (i, cg_blk)),
                  pl.BlockSpec((CONV_HALO, c), lambda i: (jnp.maximum(i * hb - 1, 0), ca_blk)),
                  pl.BlockSpec((CONV_HALO, c), lambda i: (jnp.maximum(i * hb - 1, 0), cg_blk)),
                  pl.BlockSpec((CONV_K, c), lambda i: (0, 0)),
                  vec, vec, vec],
        out_specs=pl.BlockSpec((tm, c), lambda i: (i, 0)),
        out_shape=jax.ShapeDtypeStruct((m, c), BF16),
        scratch_shapes=[pltpu.VMEM((CONV_HALO + tm + SUBLANES, c), F32),
                        pltpu.VMEM((SUBLANES, CONV_HALO + tm, LANES), F32),
                        pltpu.VMEM((tm, c), F32)],
        compiler_params=_cparams(1),
        name="conformer_conv",
    )(proj, proj, proj, proj, conv_w.astype(F32), row(conv_b), row(ln_g), row(ln_b))


def _merge_kernel(o_ref, z_ref, wa_ref, wc_ref, bc_ref, g1_ref, g2_ref, bg1_ref, bg2_ref, out_ref,
                  wab_sc, wcb_sc):
    _cast_weight_once(wa_ref, wab_sc)
    _cast_weight_once(wc_ref, wcb_sc)
    ya = jnp.dot(o_ref[...], wab_sc[...], preferred_element_type=F32)
    yc = jnp.dot(z_ref[...], wcb_sc[...], preferred_element_type=F32) + bc_ref[...]
    g1 = jax.nn.sigmoid(g1_ref[...].astype(F32) + bg1_ref[...])
    g2 = jax.nn.sigmoid(g2_ref[...].astype(F32) + bg2_ref[...])
    out_ref[...] = (g1 * ya + g2 * yc).astype(out_ref.dtype)


def _merge(o, z, proj, wa, wc, bc, bg, *, gate_col, tm, tn):
    m, ka = o.shape
    kc = z.shape[1]
    d = wa.shape[1]
    g1_blk = gate_col // tn
    g2_blk = (gate_col + d) // tn
    nb = d // tn
    bg2 = bg.reshape(1, 2 * d).astype(F32)
    return pl.pallas_call(
        _merge_kernel,
        grid=(nb, m // tm),
        in_specs=[pl.BlockSpec((tm, ka), lambda j, i: (i, 0)),
                  pl.BlockSpec((tm, kc), lambda j, i: (i, 0)),
                  pl.BlockSpec((ka, tn), lambda j, i: (0, j)),
                  pl.BlockSpec((kc, tn), lambda j, i: (0, j)),
                  pl.BlockSpec((1, tn), lambda j, i: (0, j)),
                  pl.BlockSpec((tm, tn), lambda j, i: (i, g1_blk + j)),
                  pl.BlockSpec((tm, tn), lambda j, i: (i, g2_blk + j)),
                  pl.BlockSpec((1, tn), lambda j, i: (0, j)),
                  pl.BlockSpec((1, tn), lambda j, i: (0, nb + j))],
        out_specs=pl.BlockSpec((tm, tn), lambda j, i: (i, j)),
        out_shape=jax.ShapeDtypeStruct((m, d), BF16),
        scratch_shapes=[pltpu.VMEM((ka, tn), BF16), pltpu.VMEM((kc, tn), BF16)],
        compiler_params=_cparams(2),
        name="mixer_merge",
    )(o, z, wa, wc, bc.reshape(1, d).astype(F32), proj, proj, bg2, bg2)


def _router_kernel(h_ref, g_ref, wr_ref, br_ref, valid_ref,
                   up_ref, e_ref, gate_ref, rank_ref, cnt_ref, carry_sc, *, tm):
    i = pl.program_id(0)

    @pl.when(i == 0)
    def _():
        carry_sc[...] = jnp.zeros(carry_sc.shape, F32)

    h = h_ref[...]
    ms = jnp.mean(h * h, axis=-1, keepdims=True)
    u = h * lax.rsqrt(ms + EPS) * g_ref[...]

    half = u.shape[1] // 2
    bits = lax.bitcast_convert_type(u.astype(BF16).astype(F32), jnp.uint32)
    up_ref[...] = (bits[:, half:] & jnp.uint32(0xFFFF0000)) | (bits[:, :half] >> 16)

    logits = lax.dot_general(wr_ref[...], u, (((1,), (1,)), ((), ())),
                             precision=lax.Precision.HIGHEST,
                             preferred_element_type=F32) + br_ref[...]
    n_e = logits.shape[0]
    eiota = lax.broadcasted_iota(jnp.int32, logits.shape, 0).astype(F32)
    work = logits
    sel = jnp.zeros(logits.shape, jnp.bool_)
    top_l, top_e = [], []
    for _ in range(TOP_K):
        mx = jnp.max(work, axis=0, keepdims=True)
        idx = jnp.min(jnp.where(work == mx, eiota, float(n_e)), axis=0, keepdims=True)
        hit = eiota == idx
        top_l.append(mx)
        top_e.append(idx)
        sel = sel | hit
        work = jnp.where(hit, -jnp.inf, work)
    ex = [jnp.exp(t - top_l[0]) for t in top_l]
    den = ex[0] + ex[1] + ex[2] + ex[3]
    gate_ref[...] = jnp.concatenate([e / den for e in ex], axis=0)
    e_ref[...] = jnp.concatenate(top_e, axis=0).astype(jnp.int32)

    selv = jnp.where(sel & (valid_ref[...] > 0.0), 1.0, 0.0)
    before = (lax.broadcasted_iota(jnp.int32, (tm, tm), 0)
              < lax.broadcasted_iota(jnp.int32, (tm, tm), 1)).astype(BF16)
    rank_all = jnp.dot(selv.astype(BF16), before, preferred_element_type=F32) + carry_sc[...]
    ranks = [jnp.sum(jnp.where(eiota == idx, rank_all, 0.0), axis=0, keepdims=True) for idx in top_e]
    rank_ref[...] = jnp.concatenate(ranks, axis=0).astype(jnp.int32)
    carry = carry_sc[...] + jnp.sum(selv, axis=1, keepdims=True)
    carry_sc[...] = carry
    cnt_ref[...] = jnp.broadcast_to(carry, cnt_ref.shape).astype(jnp.int32)


def _router(h2, g, w_router, b_router, valid, *, tm):
    m, d = h2.shape
    n_e = w_router.shape[1]
    tok = lambda dt: jax.ShapeDtypeStruct((TOP_K, m), dt)
    tok_spec = pl.BlockSpec((TOP_K, tm), lambda i: (0, i))
    return pl.pallas_call(
        functools.partial(_router_kernel, tm=tm),
        grid=(m // tm,),
        in_specs=[pl.BlockSpec((tm, d), lambda i: (i, 0)),
                  pl.BlockSpec((1, d), lambda i: (0, 0)),
                  pl.BlockSpec((n_e, d), lambda i: (0, 0)),
                  pl.BlockSpec((n_e, 1), lambda i: (0, 0)),
                  pl.BlockSpec((1, tm), lambda i: (0, i))],
        out_specs=[pl.BlockSpec((tm, d // 2), lambda i: (i, 0)),
                   tok_spec, tok_spec, tok_spec,
                   pl.BlockSpec((n_e, 128), lambda i: (0, 0))],
        out_shape=[jax.ShapeDtypeStruct((m, d // 2), jnp.uint32),
                   tok(jnp.int32), tok(F32), tok(jnp.int32),
                   jax.ShapeDtypeStruct((n_e, 128), jnp.int32)],
        scratch_shapes=[pltpu.VMEM((n_e, 1), F32)],
        compiler_params=_cparams(1),
        name="router",
    )(h2, g.reshape(1, d).astype(F32), w_router.T.astype(F32), b_router.reshape(n_e, 1).astype(F32), valid)


def _dispatch_kernel(dest_ref, te_ref, nu_ref, u_ref, xs_ref, zero_sc, sem, zsem, *, tm, n_tiles):
    i = pl.program_id(0)

    @pl.when(i == 0)
    def _():
        zero_sc[...] = jnp.zeros(zero_sc.shape, zero_sc.dtype)
        nu = nu_ref[0]

        def partly_filled(t):
            nxt = te_ref[jnp.minimum(t + 1, n_tiles - 1)]
            return (t >= nu - 1) | (te_ref[t] != nxt)

        def tile_copy(t):
            return pltpu.make_async_copy(zero_sc, xs_ref.at[pl.ds(t * MOE_TILE, MOE_TILE), :], zsem)

        def start(t, c):
            @pl.when(partly_filled(t))
            def _():
                tile_copy(t).start()
            return c

        def wait(t, c):
            @pl.when(partly_filled(t))
            def _():
                tile_copy(t).wait()
            return c

        lax.fori_loop(0, n_tiles, start, 0)
        lax.fori_loop(0, n_tiles, wait, 0)

    def start_row(r, c):
        for k in range(TOP_K):
            d = dest_ref[(i * tm + r) * TOP_K + k]
            pltpu.make_async_copy(u_ref.at[pl.ds(r, 1), :], xs_ref.at[pl.ds(d, 1), :], sem).start()
        return c

    lax.fori_loop(0, tm, start_row, 0, unroll=8)
    for k in range(TOP_K):
        pltpu.make_async_copy(u_ref, xs_ref.at[pl.ds(0, tm), :], sem).wait()


def _dispatch(dest_flat, tile_e, n_used, u_packed, n_rows, *, tm, n_tiles):
    m, w = u_packed.shape
    grid_spec = pltpu.PrefetchScalarGridSpec(
        num_scalar_prefetch=3,
        grid=(m // tm,),
        in_specs=[pl.BlockSpec((tm, w), lambda i, dest, te, nu: (i, 0))],
        out_specs=pl.BlockSpec(memory_space=pl.ANY),
        scratch_shapes=[pltpu.VMEM((MOE_TILE, w), jnp.uint32),
                        pltpu.SemaphoreType.DMA(()), pltpu.SemaphoreType.DMA(())],
    )
    return pl.pallas_call(
        functools.partial(_dispatch_kernel, tm=tm, n_tiles=n_tiles),
        grid_spec=grid_spec,
        out_shape=jax.ShapeDtypeStruct((n_rows, w), jnp.uint32),
        compiler_params=_cparams(1),
        name="moe_dispatch",
    )(dest_flat, tile_e, n_used, u_packed)


def _new_expert(te_ref, nu_ref, r):
    return (r < nu_ref[0]) & ((r == 0) | (te_ref[r] != te_ref[jnp.maximum(r - 1, 0)]))


def _gate_up_kernel(te_ref, nu_ref, xs_ref, wg_ref, wu_ref, bg_ref, bu_ref, act_ref, wgb_sc, wub_sc):
    r = pl.program_id(1)

    @pl.when(_new_expert(te_ref, nu_ref, r))
    def _():
        wgb_sc[...] = wg_ref[...].astype(BF16)
        wub_sc[...] = wu_ref[...].astype(BF16)

    @pl.when(r < nu_ref[0])
    def _():
        w = xs_ref[...]
        half = w.shape[1]
        lo = lax.bitcast_convert_type(w << 16, F32).astype(BF16)
        hi = lax.bitcast_convert_type(w & jnp.uint32(0xFFFF0000), F32).astype(BF16)
        g = (jnp.dot(lo, wgb_sc[:half, :], preferred_element_type=F32)
             + jnp.dot(hi, wgb_sc[half:, :], preferred_element_type=F32) + bg_ref[...])
        u = (jnp.dot(lo, wub_sc[:half, :], preferred_element_type=F32)
             + jnp.dot(hi, wub_sc[half:, :], preferred_element_type=F32) + bu_ref[...])
        g = jnp.minimum(g, SWIGLU_LIMIT)
        u = jnp.clip(u, -SWIGLU_LIMIT, SWIGLU_LIMIT)
        act_ref[...] = ((u + 1.0) * (g * jax.nn.sigmoid(SWIGLU_ALPHA * g))).astype(act_ref.dtype)

    @pl.when(r >= nu_ref[0])
    def _():
        act_ref[...] = jnp.zeros(act_ref.shape, act_ref.dtype)


def _gate_up(tile_e, n_used, xs, w_gu, b_gu, *, n_tiles, tf):
    n_e, d, f2 = w_gu.shape
    f = f2 // 2
    nj = f // tf
    rr = lambda r, nu: jnp.maximum(jnp.minimum(r, nu[0] - 1), 0)
    grid_spec = pltpu.PrefetchScalarGridSpec(
        num_scalar_prefetch=2,
        grid=(nj, n_tiles),
        in_specs=[pl.BlockSpec((MOE_TILE, d // 2), lambda j, r, te, nu: (rr(r, nu), 0)),
                  pl.BlockSpec((None, d, tf), lambda j, r, te, nu: (te[rr(r, nu)], 0, j)),
                  pl.BlockSpec((None, d, tf), lambda j, r, te, nu: (te[rr(r, nu)], 0, nj + j)),
                  pl.BlockSpec((None, 1, tf), lambda j, r, te, nu: (te[rr(r, nu)], 0, j)),
                  pl.BlockSpec((None, 1, tf), lambda j, r, te, nu: (te[rr(r, nu)], 0, nj + j))],
        out_specs=pl.BlockSpec((MOE_TILE, tf), lambda j, r, te, nu: (r, j)),
        scratch_shapes=[pltpu.VMEM((d, tf), BF16), pltpu.VMEM((d, tf), BF16)],
    )
    return pl.pallas_call(
        _gate_up_kernel,
        grid_spec=grid_spec,
        out_shape=jax.ShapeDtypeStruct((n_tiles * MOE_TILE, f), BF16),
        compiler_params=_cparams(2),
        name="moe_gate_up",
    )(tile_e, n_used, xs, w_gu, w_gu, b_gu, b_gu)


def _down_kernel(te_ref, nu_ref, act_ref, wd_ref, bd_ref, y_ref, wdb_sc):
    r = pl.program_id(1)

    @pl.when(_new_expert(te_ref, nu_ref, r))
    def _():
        wdb_sc[...] = wd_ref[...].astype(BF16)

    @pl.when(r < nu_ref[0])
    def _():
        y_ref[...] = jnp.dot(act_ref[...], wdb_sc[...], preferred_element_type=F32) + bd_ref[...]

    @pl.when(r >= nu_ref[0])
    def _():
        y_ref[...] = jnp.zeros(y_ref.shape, y_ref.dtype)


def _down(tile_e, n_used, act, w_d, b_d, *, n_tiles, tn):
    n_e, f, d = w_d.shape
    rr = lambda r, nu: jnp.maximum(jnp.minimum(r, nu[0] - 1), 0)
    grid_spec = pltpu.PrefetchScalarGridSpec(
        num_scalar_prefetch=2,
        grid=(d // tn, n_tiles),
        in_specs=[pl.BlockSpec((MOE_TILE, f), lambda j, r, te, nu: (rr(r, nu), 0)),
                  pl.BlockSpec((None, f, tn), lambda j, r, te, nu: (te[rr(r, nu)], 0, j)),
                  pl.BlockSpec((None, 1, tn), lambda j, r, te, nu: (te[rr(r, nu)], 0, j))],
        out_specs=pl.BlockSpec((MOE_TILE, tn), lambda j, r, te, nu: (r, j)),
        scratch_shapes=[pltpu.VMEM((f, tn), BF16)],
    )
    return pl.pallas_call(
        _down_kernel,
        grid_spec=grid_spec,
        out_shape=jax.ShapeDtypeStruct((n_tiles * MOE_TILE, d), F32),
        compiler_params=_cparams(2),
        name="moe_down",
    )(tile_e, n_used, act, w_d, b_d)


def _combine_kernel(dest_ref, h_ref, gate_ref, g_ref, ys_ref, out_ref, buf, sem, *, tm, lp, row0, nst):
    s = pl.program_id(0)
    n_steps = pl.num_programs(0)

    def start_gather(step, slot):
        t0 = (step // nst) * lp + row0 + (step % nst) * tm

        def start_row(r, c):
            for k in range(TOP_K):
                d = dest_ref[(t0 + r) * TOP_K + k]
                pltpu.make_async_copy(ys_ref.at[pl.ds(d, 1), :], buf.at[slot, k, pl.ds(r, 1), :],
                                      sem.at[slot]).start()
            return c

        lax.fori_loop(0, tm, start_row, 0, unroll=8)

    @pl.when(s == 0)
    def _():
        start_gather(0, 0)

    @pl.when(s + 1 < n_steps)
    def _():
        start_gather(s + 1, (s + 1) % 2)

    slot = s % 2
    for k in range(TOP_K):
        pltpu.make_async_copy(ys_ref.at[pl.ds(0, tm), :], buf.at[slot, k], sem.at[slot]).wait()

    gate = gate_ref[...]
    acc = jnp.zeros(h_ref.shape, F32)
    for k in range(TOP_K):
        acc = acc + gate[:, k:k + 1] * buf[slot, k]
    h = h_ref[...] + acc
    ms = jnp.mean(h * h, axis=-1, keepdims=True)
    out_ref[...] = (h * lax.rsqrt(ms + EPS) * g_ref[...]).astype(out_ref.dtype)


def _combine(dest_flat, h2, gate, g, ys, *, batch, seq, lp, row0, tm):
    d = h2.shape[1]
    nb_b = lp // tm
    nb0 = row0 // tm
    nst = seq // tm
    blk = lambda s: (s // nst) * nb_b + nb0 + s % nst
    grid_spec = pltpu.PrefetchScalarGridSpec(
        num_scalar_prefetch=1,
        grid=(batch * nst,),
        in_specs=[pl.BlockSpec((tm, d), lambda s, dest: (blk(s), 0)),
                  pl.BlockSpec((tm, TOP_K), lambda s, dest: (blk(s), 0)),
                  pl.BlockSpec((1, d), lambda s, dest: (0, 0)),
                  pl.BlockSpec(memory_space=pl.ANY)],
        out_specs=pl.BlockSpec((None, tm, d), lambda s, dest: (s // nst, s % nst, 0)),
        scratch_shapes=[pltpu.VMEM((2, TOP_K, tm, d), F32), pltpu.SemaphoreType.DMA((2,))],
    )
    return pl.pallas_call(
        functools.partial(_combine_kernel, tm=tm, lp=lp, row0=row0, nst=nst),
        grid_spec=grid_spec,
        out_shape=jax.ShapeDtypeStruct((batch, seq, d), F32),
        compiler_params=_cparams(1),
        name="moe_combine",
    )(dest_flat, h2, gate, g.reshape(1, d).astype(F32), ys)


def _pick(pref, n):
    t = pref
    while n % t:
        t //= 2
    return t


def _row_tile(n, pref):
    t = pref // ROW_ALIGN * ROW_ALIGN
    while n % t:
        t -= ROW_ALIGN
    return t


def kernel(x, meta_tokens, norm_mix_g, w_in, b_gate, lambda_q1, lambda_k1, lambda_q2, lambda_k2, head_norm_g, w_attn_out, conv_w, conv_b, conv_ln_g, conv_ln_b, w_conv_out, b_conv_out, w_out, norm_ffn_g, w_router, b_router, w_gate_up, b_gate_up, w_down, b_down, final_norm_g):
    batch, seq, d = x.shape
    depth = w_in.shape[0]
    assert depth == 1 and seq % ROW_ALIGN == 0 and N_META <= ROW_ALIGN
    n_heads = d // 256
    hw = 2 * HEAD_DIM
    qk_w = n_heads * hw
    conv_ch = conv_w.shape[2]
    n_pad = ROW_ALIGN - N_META
    lp = n_pad + N_META + seq
    tp = batch * lp
    f = w_down.shape[2]
    layer = 0
    lam_init = 0.8 - 0.6 * math.exp(-0.3 * layer)

    h0, u = _embed_norm(x, meta_tokens, norm_mix_g[layer], n_pad=n_pad)

    proj = _matmul(u, w_in[layer], BF16, _row_tile(tp, 1536), _pick(512, w_in.shape[2]), name="in_proj")

    tq = 384 if lp % 384 == 0 else ROW_ALIGN
    vt = proj[:, 2 * qk_w:3 * qk_w].reshape(batch, lp, n_heads, hw).transpose(0, 2, 3, 1)
    o = _attention(proj, vt, lambda_q1[layer], lambda_k1[layer], lambda_q2[layer], lambda_k2[layer],
                   head_norm_g[layer], batch=batch, lp=lp, n_heads=n_heads, tq=tq, lam_init=lam_init,
                   n_pad=n_pad)
    ca_col = 3 * qk_w
    z = _conv_branch(proj, conv_w[layer], conv_b[layer], conv_ln_g[layer], conv_ln_b[layer],
                     ca_blk=ca_col // conv_ch, cg_blk=ca_col // conv_ch + 1, tm=_pick(256, tp))
    merged = _merge(o, z, proj, w_attn_out[layer], w_conv_out[layer],
                    b_conv_out[layer], b_gate[layer], gate_col=ca_col + 2 * conv_ch,
                    tm=_row_tile(tp, 768), tn=_pick(512, d))
    h2 = _matmul(merged, w_out[layer], F32, _row_tile(tp, 768), _pick(512, d), res=h0, name="out_proj")

    pos = np.arange(tp) % lp
    valid_np = pos >= n_pad
    valid = jnp.asarray(valid_np.astype(np.float32).reshape(1, tp))
    u_packed, top_e, gate_t, rank_t, cnt = _router(h2, norm_ffn_g[layer], w_router[layer], b_router[layer],
                                                   valid, tm=_pick(256, tp))
    counts = cnt[:, 0]
    padded = (counts + MOE_TILE - 1) // MOE_TILE * MOE_TILE
    e_ids = np.arange(N_EXPERTS)
    pad_end = jnp.sum(jnp.where(jnp.asarray(e_ids[None, :] <= e_ids[:, None]), padded[None, :], 0), axis=1)
    pad_start = pad_end - padded
    start_tok = jnp.sum(jnp.where(top_e[:, :, None] == jnp.asarray(e_ids, jnp.int32), pad_start, 0), axis=-1)
    n_real = int(valid_np.sum()) * TOP_K
    n_tiles = -(-(n_real + N_EXPERTS * (MOE_TILE - 1)) // MOE_TILE)
    n_slots = n_tiles * MOE_TILE
    dump = n_slots + (np.cumsum(~valid_np) - 1)[None, :] * TOP_K + np.arange(TOP_K)[:, None]
    dest_t = jnp.where(jnp.asarray(valid_np)[None, :], start_tok + rank_t, jnp.asarray(dump, jnp.int32))
    dest_flat = dest_t.T.reshape(-1).astype(jnp.int32)
    n_dump = int((~valid_np).sum()) * TOP_K
    n_used = (pad_end[-1] // MOE_TILE).astype(jnp.int32).reshape(1)
    tile_start = jnp.asarray(np.arange(n_tiles, dtype=np.int32) * MOE_TILE)
    tile_e = jnp.minimum(jnp.sum((pad_end[None, :] <= tile_start[:, None]).astype(jnp.int32), axis=1),
                         N_EXPERTS - 1).astype(jnp.int32)

    xs = _dispatch(dest_flat, tile_e, n_used, u_packed, n_slots + n_dump, tm=ROW_ALIGN, n_tiles=n_tiles)
    act = _gate_up(tile_e, n_used, xs, w_gate_up[layer],
                   b_gate_up[layer].reshape(N_EXPERTS, 1, 2 * f).astype(F32), n_tiles=n_tiles, tf=_pick(512, f))
    ys = _down(tile_e, n_used, act, w_down[layer],
               b_down[layer].reshape(N_EXPERTS, 1, d).astype(F32), n_tiles=n_tiles, tn=_pick(2048, d))
    return _combine(dest_flat, h2, gate_t.T, final_norm_g, ys, batch=batch, seq=seq, lp=lp,
                    row0=ROW_ALIGN, tm=ROW_ALIGN)
```

```python
import functools
import math

import numpy as np
import jax
import jax.numpy as jnp
from jax import lax
from jax.experimental import pallas as pl
from jax.experimental.pallas import tpu as pltpu

N_META = 16
HEAD_DIM = 64
N_EXPERTS = 32
TOP_K = 4
CONV_K = 31
EPS = 1e-5
SWIGLU_LIMIT = 7.0
SWIGLU_ALPHA = 1.702
SUBLANES = 8
LANES = 128
ROW_ALIGN = 128
CONV_HALO = 32
MOE_TILE = 512
VMEM_LIMIT = 56 * 1024 * 1024

F32 = jnp.float32
BF16 = jnp.bfloat16


def _cparams(n_axes, flags=None):
    return pltpu.CompilerParams(dimension_semantics=("arbitrary",) * n_axes,
                                vmem_limit_bytes=VMEM_LIMIT, flags=flags)


def _embed_norm_kernel(x_ref, meta_ref, g_ref, h_ref, u_ref, *, n_pad):
    i = pl.program_id(1)

    @pl.when(i == 0)
    def _():
        h_ref[0:n_pad, :] = jnp.zeros((n_pad, h_ref.shape[1]), F32)
        h_ref[n_pad:, :] = meta_ref[...]

    @pl.when(i > 0)
    def _():
        h_ref[...] = x_ref[...]

    h = h_ref[...]
    ms = jnp.mean(h * h, axis=-1, keepdims=True)
    u_ref[...] = (h * lax.rsqrt(ms + EPS) * g_ref[...]).astype(u_ref.dtype)


def _embed_norm(x, meta, g, *, n_pad):
    batch, seq, d = x.shape
    tm = ROW_ALIGN
    nb = (n_pad + N_META + seq) // tm
    out_spec = pl.BlockSpec((tm, d), lambda b, i: (b * nb + i, 0))
    return pl.pallas_call(
        functools.partial(_embed_norm_kernel, n_pad=n_pad),
        grid=(batch, nb),
        in_specs=[pl.BlockSpec((None, tm, d), lambda b, i: (b, jnp.maximum(i - 1, 0), 0)),
                  pl.BlockSpec((N_META, d), lambda b, i: (0, 0)),
                  pl.BlockSpec((1, d), lambda b, i: (0, 0))],
        out_specs=[out_spec, out_spec],
        out_shape=[jax.ShapeDtypeStruct((batch * nb * tm, d), F32),
                   jax.ShapeDtypeStruct((batch * nb * tm, d), BF16)],
        compiler_params=_cparams(2),
        name="embed_norm",
    )(x, meta.astype(F32), g.reshape(1, d).astype(F32))


def _cast_weight_once(w_ref, wb_sc):
    @pl.when(pl.program_id(1) == 0)
    def _():
        wb_sc[...] = w_ref[...].astype(BF16)


def _matmul_kernel(a_ref, w_ref, o_ref, wb_sc):
    _cast_weight_once(w_ref, wb_sc)
    o_ref[...] = jnp.dot(a_ref[...], wb_sc[...], preferred_element_type=F32).astype(o_ref.dtype)


def _matmul_res_kernel(a_ref, w_ref, r_ref, o_ref, wb_sc):
    _cast_weight_once(w_ref, wb_sc)
    acc = jnp.dot(a_ref[...], wb_sc[...], preferred_element_type=F32)
    o_ref[...] = (acc + r_ref[...]).astype(o_ref.dtype)


def _matmul(a, w, out_dtype, tm, tn, res=None, name="matmul"):
    m, k = a.shape
    n = w.shape[1]
    in_specs = [pl.BlockSpec((tm, k), lambda j, i: (i, 0)),
                pl.BlockSpec((k, tn), lambda j, i: (0, j))]
    args = [a, w]
    kern = _matmul_kernel
    if res is not None:
        in_specs.append(pl.BlockSpec((tm, tn), lambda j, i: (i, j)))
        args.append(res)
        kern = _matmul_res_kernel
    return pl.pallas_call(
        kern,
        grid=(n // tn, m // tm),
        in_specs=in_specs,
        out_specs=pl.BlockSpec((tm, tn), lambda j, i: (i, j)),
        out_shape=jax.ShapeDtypeStruct((m, n), out_dtype),
        scratch_shapes=[pltpu.VMEM((k, tn), BF16)],
        compiler_params=_cparams(2),
        name=name,
    )(*args)


def _attn_kernel(lq1_ref, lk1_ref, lq2_ref, lk2_ref, hg_ref, bias_ref, q_ref, k_ref, vt_ref, o_ref,
                 q12_sc, sa_sc, sb_sc, m_sc, l_sc, acc_sc, *, tq, lam_init):
    nq = q_ref.shape[0] // tq
    lam = (jnp.exp(jnp.sum(lq1_ref[...] * lk1_ref[...], axis=-1, keepdims=True))
           - jnp.exp(jnp.sum(lq2_ref[...] * lk2_ref[...], axis=-1, keepdims=True)) + lam_init)

    def q_tile(qi, carry):
        q0 = pl.multiple_of(qi * tq, tq)
        q = (q_ref[pl.ds(q0, tq), :].astype(F32) * (HEAD_DIM ** -0.5 * math.log2(math.e))).astype(BF16)
        lane = lax.broadcasted_iota(jnp.int32, q.shape, 1)
        zero = jnp.zeros_like(q)
        q12_sc[0:tq, :] = jnp.where(lane < HEAD_DIM, q, zero)
        q12_sc[tq:2 * tq, :] = jnp.where(lane >= HEAD_DIM, q, zero)
        m_sc[...] = jnp.full(m_sc.shape, -jnp.inf, F32)
        l_sc[...] = jnp.zeros(l_sc.shape, F32)
        acc_sc[...] = jnp.zeros(acc_sc.shape, F32)

        def scores(kj, mp, s_ref):
            k0 = pl.multiple_of(kj * tq, tq)
            s = lax.dot_general(k_ref[pl.ds(k0, tq), :], q12_sc[mp * tq:(mp + 1) * tq, :],
                                (((1,), (1,)), ((), ())), preferred_element_type=F32)
            kind = jnp.where(kj == 0, 1, 0) + jnp.where(kj == qi, 2, 0)
            s_ref[...] = s + bias_ref[kind]

        def update(kj, mp, s_ref):
            k0 = pl.multiple_of(kj * tq, tq)
            s = s_ref[...]
            m_prev = m_sc[mp]
            m_new = jnp.maximum(m_prev, jnp.max(s, axis=0, keepdims=True))
            alpha = jnp.exp2(m_prev - m_new)
            p = jnp.exp2(s - m_new)
            l_sc[mp] = alpha * l_sc[mp] + jnp.sum(p.reshape(tq // 8, 8, tq), axis=0)
            acc_sc[mp] = alpha * acc_sc[mp] + jnp.dot(vt_ref[:, pl.ds(k0, tq)], p.astype(BF16),
                                                      preferred_element_type=F32)
            m_sc[mp] = m_new

        scores(0, 0, sa_sc)

        def body(kj, c):
            scores(kj, 1, sb_sc)
            update(kj, 0, sa_sc)
            scores(jnp.minimum(kj + 1, qi), 0, sa_sc)
            update(kj, 1, sb_sc)
            return c

        lax.fori_loop(0, qi + 1, body, 0)

        o1 = acc_sc[0] / jnp.sum(l_sc[0], axis=0, keepdims=True)
        o2 = acc_sc[1] / jnp.sum(l_sc[1], axis=0, keepdims=True)
        o = o1 - lam * o2
        ms = jnp.mean(o * o, axis=0, keepdims=True)
        o = o * lax.rsqrt(ms + EPS) * hg_ref[...] * (1.0 - lam_init)
        o_ref[pl.ds(q0, tq), :] = o.T.astype(o_ref.dtype)
        return carry

    lax.fori_loop(0, nq, q_tile, 0)


def _attn_bias(tq, n_pad):
    neg = np.float32(np.finfo(np.float32).min)
    r = np.arange(tq)[:, None]
    c = np.arange(tq)[None, :]
    pad = np.broadcast_to(r < n_pad, (tq, tq))
    future = r > c
    tiles = [np.zeros((tq, tq), bool), pad, future, pad | future]
    return jnp.asarray(np.stack([np.where(t, neg, np.float32(0)) for t in tiles]).astype(np.float32))


def _attention(proj, vt, lq1, lk1, lq2, lk2, head_g, *, batch, lp, n_heads, tq, lam_init, n_pad):
    hw = 2 * HEAD_DIM
    assert n_pad <= tq
    koff = n_heads
    vec = lambda a: a.reshape(1, -1).astype(F32)
    small = lambda n: pl.BlockSpec((1, n), lambda b, h: (0, 0))
    return pl.pallas_call(
        functools.partial(_attn_kernel, tq=tq, lam_init=lam_init),
        grid=(batch, n_heads),
        in_specs=[small(HEAD_DIM), small(HEAD_DIM), small(HEAD_DIM), small(HEAD_DIM),
                  pl.BlockSpec((hw, 1), lambda b, h: (0, 0)),
                  pl.BlockSpec((4, tq, tq), lambda b, h: (0, 0, 0)),
                  pl.BlockSpec((lp, hw), lambda b, h: (b, h)),
                  pl.BlockSpec((lp, hw), lambda b, h: (b, koff + h)),
                  pl.BlockSpec((None, None, hw, lp), lambda b, h: (b, h, 0, 0))],
        out_specs=pl.BlockSpec((lp, hw), lambda b, h: (b, h)),
        out_shape=jax.ShapeDtypeStruct((batch * lp, n_heads * hw), BF16),
        scratch_shapes=[pltpu.VMEM((2 * tq, hw), BF16),
                        pltpu.VMEM((tq, tq), F32), pltpu.VMEM((tq, tq), F32),
                        pltpu.VMEM((2, 1, tq), F32), pltpu.VMEM((2, 8, tq), F32),
                        pltpu.VMEM((2, hw, tq), F32)],
        compiler_params=_cparams(2),
        name="diff_attention",
    )(vec(lq1), vec(lk1), vec(lq2), vec(lk2), head_g.reshape(hw, 1).astype(F32), _attn_bias(tq, n_pad),
      proj, proj, vt)


def _conv_kernel(ca_ref, cg_ref, ca_h_ref, cg_h_ref, w_ref, b_ref, lg_ref, lb_ref, z_ref,
                 ext_sc, sh_sc, y_sc, *, tm):
    i = pl.program_id(0)
    n_ch = w_ref.shape[1]
    n_ext = CONV_HALO + tm
    glu = lambda a, g: a.astype(F32) * jax.nn.sigmoid(g.astype(F32))
    halo = glu(ca_h_ref[...], cg_h_ref[...])
    ext_sc[0:CONV_HALO, :] = jnp.where(i > 0, halo, jnp.zeros_like(halo))
    ext_sc[CONV_HALO:n_ext, :] = glu(ca_ref[...], cg_ref[...])
    ext_sc[n_ext:n_ext + SUBLANES, :] = jnp.zeros((SUBLANES, n_ch), F32)
    base = CONV_HALO - (CONV_K - 1)

    def slab(lc, carry):
        l0 = pl.multiple_of(lc * LANES, LANES)
        for rho in range(SUBLANES):
            sh_sc[rho] = ext_sc[rho:rho + n_ext, pl.ds(l0, LANES)]
        acc = jnp.zeros((tm, LANES), F32) + b_ref[:, pl.ds(l0, LANES)]
        for j in range(CONV_K):
            rho = (base + j) % SUBLANES
            a = base + j - rho
            acc = acc + w_ref[j:j + 1, pl.ds(l0, LANES)] * sh_sc[rho, a:a + tm, :]
        y_sc[:, pl.ds(l0, LANES)] = acc
        return carry

    lax.fori_loop(0, n_ch // LANES, slab, 0)
    acc = y_sc[...]
    mu = jnp.mean(acc, axis=-1, keepdims=True)
    d = acc - mu
    var = jnp.mean(d * d, axis=-1, keepdims=True)
    y = d * lax.rsqrt(var + EPS) * lg_ref[...] + lb_ref[...]
    z_ref[...] = (y * jax.nn.sigmoid(y)).astype(z_ref.dtype)


def _conv_branch(proj, conv_w, conv_b, ln_g, ln_b, *, ca_blk, cg_blk, tm):
    m = proj.shape[0]
    c = conv_w.shape[1]
    hb = tm // CONV_HALO
    row = lambda a: a.reshape(1, c).astype(F32)
    vec = pl.BlockSpec((1, c), lambda i: (0, 0))
    return pl.pallas_call(
        functools.partial(_conv_kernel, tm=tm),
        grid=(m // tm,),
        in_specs=[pl.BlockSpec((tm, c), lambda i: (i, ca_blk)),
                  pl.BlockSpec((tm, c), lambda i: (i, cg_blk)),
                  pl.BlockSpec((CONV_HALO, c), lambda i: (jnp.maximum(i * hb - 1, 0), ca_blk)),
                  pl.BlockSpec((CONV_HALO, c), lambda i: (jnp.maximum(i * hb - 1, 0), cg_blk)),
                  pl.BlockSpec((CONV_K, c), lambda i: (0, 0)),
                  vec, vec, vec],
        out_specs=pl.BlockSpec((tm, c), lambda i: (i, 0)),
        out_shape=jax.ShapeDtypeStruct((m, c), BF16),
        scratch_shapes=[pltpu.VMEM((CONV_HALO + tm + SUBLANES, c), F32),
                        pltpu.VMEM((SUBLANES, CONV_HALO + tm, LANES), F32),
                        pltpu.VMEM((tm, c), F32)],
        compiler_params=_cparams(1),
        name="conformer_conv",
    )(proj, proj, proj, proj, conv_w.astype(F32), row(conv_b), row(ln_g), row(ln_b))


def _merge_kernel(o_ref, z_ref, wa_ref, wc_ref, bc_ref, g1_ref, g2_ref, bg1_ref, bg2_ref, out_ref,
                  wab_sc, wcb_sc):
    _cast_weight_once(wa_ref, wab_sc)
    _cast_weight_once(wc_ref, wcb_sc)
    ya = jnp.dot(o_ref[...], wab_sc[...], preferred_element_type=F32)
    yc = jnp.dot(z_ref[...], wcb_sc[...], preferred_element_type=F32) + bc_ref[...]
    g1 = jax.nn.sigmoid(g1_ref[...].astype(F32) + bg1_ref[...])
    g2 = jax.nn.sigmoid(g2_ref[...].astype(F32) + bg2_ref[...])
    out_ref[...] = (g1 * ya + g2 * yc).astype(out_ref.dtype)


def _merge(o, z, proj, wa, wc, bc, bg, *, gate_col, tm, tn):
    m, ka = o.shape
    kc = z.shape[1]
    d = wa.shape[1]
    g1_blk = gate_col // tn
    g2_blk = (gate_col + d) // tn
    nb = d // tn
    bg2 = bg.reshape(1, 2 * d).astype(F32)
    return pl.pallas_call(
        _merge_kernel,
        grid=(nb, m // tm),
        in_specs=[pl.BlockSpec((tm, ka), lambda j, i: (i, 0)),
                  pl.BlockSpec((tm, kc), lambda j, i: (i, 0)),
                  pl.BlockSpec((ka, tn), lambda j, i: (0, j)),
                  pl.BlockSpec((kc, tn), lambda j, i: (0, j)),
                  pl.BlockSpec((1, tn), lambda j, i: (0, j)),
                  pl.BlockSpec((tm, tn), lambda j, i: (i, g1_blk + j)),
                  pl.BlockSpec((tm, tn), lambda j, i: (i, g2_blk + j)),
                  pl.BlockSpec((1, tn), lambda j, i: (0, j)),
                  pl.BlockSpec((1, tn), lambda j, i: (0, nb + j))],
        out_specs=pl.BlockSpec((tm, tn), lambda j, i: (i, j)),
        out_shape=jax.ShapeDtypeStruct((m, d), BF16),
        scratch_shapes=[pltpu.VMEM((ka, tn), BF16), pltpu.VMEM((kc, tn), BF16)],
        compiler_params=_cparams(2),
        name="mixer_merge",
    )(o, z, wa, wc, bc.reshape(1, d).astype(F32), proj, proj, bg2, bg2)


def _router_kernel(h_ref, g_ref, wr_ref, br_ref, valid_ref,
                   up_ref, e_ref, gate_ref, rank_ref, cnt_ref, carry_sc, *, tm):
    i = pl.program_id(0)

    @pl.when(i == 0)
    def _():
        carry_sc[...] = jnp.zeros(carry_sc.shape, F32)

    h = h_ref[...]
    ms = jnp.mean(h * h, axis=-1, keepdims=True)
    u = h * lax.rsqrt(ms + EPS) * g_ref[...]

    half = u.shape[1] // 2
    bits = lax.bitcast_convert_type(u.astype(BF16).astype(F32), jnp.uint32)
    up_ref[...] = (bits[:, half:] & jnp.uint32(0xFFFF0000)) | (bits[:, :half] >> 16)

    logits = lax.dot_general(wr_ref[...], u, (((1,), (1,)), ((), ())),
                             precision=lax.Precision.HIGHEST,
                             preferred_element_type=F32) + br_ref[...]
    n_e = logits.shape[0]
    eiota = lax.broadcasted_iota(jnp.int32, logits.shape, 0).astype(F32)
    work = logits
    sel = jnp.zeros(logits.shape, jnp.bool_)
    top_l, top_e = [], []
    for _ in range(TOP_K):
        mx = jnp.max(work, axis=0, keepdims=True)
        idx = jnp.min(jnp.where(work == mx, eiota, float(n_e)), axis=0, keepdims=True)
        hit = eiota == idx
        top_l.append(mx)
        top_e.append(idx)
        sel = sel | hit
        work = jnp.where(hit, -jnp.inf, work)
    ex = [jnp.exp(t - top_l[0]) for t in top_l]
    den = ex[0] + ex[1] + ex[2] + ex[3]
    gate_ref[...] = jnp.concatenate([e / den for e in ex], axis=0)
    e_ref[...] = jnp.concatenate(top_e, axis=0).astype(jnp.int32)

    selv = jnp.where(sel & (valid_ref[...] > 0.0), 1.0, 0.0)
    before = (lax.broadcasted_iota(jnp.int32, (tm, tm), 0)
              < lax.broadcasted_iota(jnp.int32, (tm, tm), 1)).astype(BF16)
    rank_all = jnp.dot(selv.astype(BF16), before, preferred_element_type=F32) + carry_sc[...]
    ranks = [jnp.sum(jnp.where(eiota == idx, rank_all, 0.0), axis=0, keepdims=True) for idx in top_e]
    rank_ref[...] = jnp.concatenate(ranks, axis=0).astype(jnp.int32)
    carry = carry_sc[...] + jnp.sum(selv, axis=1, keepdims=True)
    carry_sc[...] = carry
    cnt_ref[...] = jnp.broadcast_to(carry, cnt_ref.shape).astype(jnp.int32)


def _router(h2, g, w_router, b_router, valid, *, tm):
    m, d = h2.shape
    n_e = w_router.shape[1]
    tok = lambda dt: jax.ShapeDtypeStruct((TOP_K, m), dt)
    tok_spec = pl.BlockSpec((TOP_K, tm), lambda i: (0, i))
    return pl.pallas_call(
        functools.partial(_router_kernel, tm=tm),
        grid=(m // tm,),
        in_specs=[pl.BlockSpec((tm, d), lambda i: (i, 0)),
                  pl.BlockSpec((1, d), lambda i: (0, 0)),
                  pl.BlockSpec((n_e, d), lambda i: (0, 0)),
                  pl.BlockSpec((n_e, 1), lambda i: (0, 0)),
                  pl.BlockSpec((1, tm), lambda i: (0, i))],
        out_specs=[pl.BlockSpec((tm, d // 2), lambda i: (i, 0)),
                   tok_spec, tok_spec, tok_spec,
                   pl.BlockSpec((n_e, 128), lambda i: (0, 0))],
        out_shape=[jax.ShapeDtypeStruct((m, d // 2), jnp.uint32),
                   tok(jnp.int32), tok(F32), tok(jnp.int32),
                   jax.ShapeDtypeStruct((n_e, 128), jnp.int32)],
        scratch_shapes=[pltpu.VMEM((n_e, 1), F32)],
        compiler_params=_cparams(1),
        name="router",
    )(h2, g.reshape(1, d).astype(F32), w_router.T.astype(F32), b_router.reshape(n_e, 1).astype(F32), valid)


def _dispatch_kernel(dest_ref, te_ref, nu_ref, u_ref, xs_ref, zero_sc, sem, zsem, *, tm, n_tiles):
    i = pl.program_id(0)

    @pl.when(i == 0)
    def _():
        zero_sc[...] = jnp.zeros(zero_sc.shape, zero_sc.dtype)
        nu = nu_ref[0]

        def partly_filled(t):
            nxt = te_ref[jnp.minimum(t + 1, n_tiles - 1)]
            return (t >= nu - 1) | (te_ref[t] != nxt)

        def tile_copy(t):
            return pltpu.make_async_copy(zero_sc, xs_ref.at[pl.ds(t * MOE_TILE, MOE_TILE), :], zsem)

        def start(t, c):
            @pl.when(partly_filled(t))
            def _():
                tile_copy(t).start()
            return c

        def wait(t, c):
            @pl.when(partly_filled(t))
            def _():
                tile_copy(t).wait()
            return c

        lax.fori_loop(0, n_tiles, start, 0)
        lax.fori_loop(0, n_tiles, wait, 0)

    def start_row(r, c):
        for k in range(TOP_K):
            d = dest_ref[(i * tm + r) * TOP_K + k]
            pltpu.make_async_copy(u_ref.at[pl.ds(r, 1), :], xs_ref.at[pl.ds(d, 1), :], sem).start()
        return c

    lax.fori_loop(0, tm, start_row, 0, unroll=8)
    for k in range(TOP_K):
        pltpu.make_async_copy(u_ref, xs_ref.at[pl.ds(0, tm), :], sem).wait()


def _dispatch(dest_flat, tile_e, n_used, u_packed, n_rows, *, tm, n_tiles):
    m, w = u_packed.shape
    grid_spec = pltpu.PrefetchScalarGridSpec(
        num_scalar_prefetch=3,
        grid=(m // tm,),
        in_specs=[pl.BlockSpec((tm, w), lambda i, dest, te, nu: (i, 0))],
        out_specs=pl.BlockSpec(memory_space=pl.ANY),
        scratch_shapes=[pltpu.VMEM((MOE_TILE, w), jnp.uint32),
                        pltpu.SemaphoreType.DMA(()), pltpu.SemaphoreType.DMA(())],
    )
    return pl.pallas_call(
        functools.partial(_dispatch_kernel, tm=tm, n_tiles=n_tiles),
        grid_spec=grid_spec,
        out_shape=jax.ShapeDtypeStruct((n_rows, w), jnp.uint32),
        compiler_params=_cparams(1),
        name="moe_dispatch",
    )(dest_flat, tile_e, n_used, u_packed)


def _new_expert(te_ref, nu_ref, r):
    return (r < nu_ref[0]) & ((r == 0) | (te_ref[r] != te_ref[jnp.maximum(r - 1, 0)]))


def _gate_up_kernel(te_ref, nu_ref, xs_ref, wg_ref, wu_ref, bg_ref, bu_ref, act_ref, wgb_sc, wub_sc):
    r = pl.program_id(1)

    @pl.when(_new_expert(te_ref, nu_ref, r))
    def _():
        wgb_sc[...] = wg_ref[...].astype(BF16)
        wub_sc[...] = wu_ref[...].astype(BF16)

    @pl.when(r < nu_ref[0])
    def _():
        w = xs_ref[...]
        half = w.shape[1]
        lo = lax.bitcast_convert_type(w << 16, F32).astype(BF16)
        hi = lax.bitcast_convert_type(w & jnp.uint32(0xFFFF0000), F32).astype(BF16)
        g = (jnp.dot(lo, wgb_sc[:half, :], preferred_element_type=F32)
             + jnp.dot(hi, wgb_sc[half:, :], preferred_element_type=F32) + bg_ref[...])
        u = (jnp.dot(lo, wub_sc[:half, :], preferred_element_type=F32)
             + jnp.dot(hi, wub_sc[half:, :], preferred_element_type=F32) + bu_ref[...])
        g = jnp.minimum(g, SWIGLU_LIMIT)
        u = jnp.clip(u, -SWIGLU_LIMIT, SWIGLU_LIMIT)
        act_ref[...] = ((u + 1.0) * (g * jax.nn.sigmoid(SWIGLU_ALPHA * g))).astype(act_ref.dtype)

    @pl.when(r >= nu_ref[0])
    def _():
        act_ref[...] = jnp.zeros(act_ref.shape, act_ref.dtype)


def _gate_up(tile_e, n_used, xs, w_gu, b_gu, *, n_tiles, tf):
    n_e, d, f2 = w_gu.shape
    f = f2 // 2
    nj = f // tf
    rr = lambda r, nu: jnp.maximum(jnp.minimum(r, nu[0] - 1), 0)
    grid_spec = pltpu.PrefetchScalarGridSpec(
        num_scalar_prefetch=2,
        grid=(nj, n_tiles),
        in_specs=[pl.BlockSpec((MOE_TILE, d // 2), lambda j, r, te, nu: (rr(r, nu), 0)),
                  pl.BlockSpec((None, d, tf), lambda j, r, te, nu: (te[rr(r, nu)], 0, j)),
                  pl.BlockSpec((None, d, tf), lambda j, r, te, nu: (te[rr(r, nu)], 0, nj + j)),
                  pl.BlockSpec((None, 1, tf), lambda j, r, te, nu: (te[rr(r, nu)], 0, j)),
                  pl.BlockSpec((None, 1, tf), lambda j, r, te, nu: (te[rr(r, nu)], 0, nj + j))],
        out_specs=pl.BlockSpec((MOE_TILE, tf), lambda j, r, te, nu: (r, j)),
        scratch_shapes=[pltpu.VMEM((d, tf), BF16), pltpu.VMEM((d, tf), BF16)],
    )
    return pl.pallas_call(
        _gate_up_kernel,
        grid_spec=grid_spec,
        out_shape=jax.ShapeDtypeStruct((n_tiles * MOE_TILE, f), BF16),
        compiler_params=_cparams(2),
        name="moe_gate_up",
    )(tile_e, n_used, xs, w_gu, w_gu, b_gu, b_gu)


def _down_kernel(te_ref, nu_ref, act_ref, wd_ref, bd_ref, y_ref, wdb_sc):
    r = pl.program_id(1)

    @pl.when(_new_expert(te_ref, nu_ref, r))
    def _():
        wdb_sc[...] = wd_ref[...].astype(BF16)

    @pl.when(r < nu_ref[0])
    def _():
        y_ref[...] = jnp.dot(act_ref[...], wdb_sc[...], preferred_element_type=F32) + bd_ref[...]

    @pl.when(r >= nu_ref[0])
    def _():
        y_ref[...] = jnp.zeros(y_ref.shape, y_ref.dtype)


def _down(tile_e, n_used, act, w_d, b_d, *, n_tiles, tn):
    n_e, f, d = w_d.shape
    rr = lambda r, nu: jnp.maximum(jnp.minimum(r, nu[0] - 1), 0)
    grid_spec = pltpu.PrefetchScalarGridSpec(
        num_scalar_prefetch=2,
        grid=(d // tn, n_tiles),
        in_specs=[pl.BlockSpec((MOE_TILE, f), lambda j, r, te, nu: (rr(r, nu), 0)),
                  pl.BlockSpec((None, f, tn), lambda j, r, te, nu: (te[rr(r, nu)], 0, j)),
                  pl.BlockSpec((None, 1, tn), lambda j, r, te, nu: (te[rr(r, nu)], 0, j))],
        out_specs=pl.BlockSpec((MOE_TILE, tn), lambda j, r, te, nu: (r, j)),
        scratch_shapes=[pltpu.VMEM((f, tn), BF16)],
    )
    return pl.pallas_call(
        _down_kernel,
        grid_spec=grid_spec,
        out_shape=jax.ShapeDtypeStruct((n_tiles * MOE_TILE, d), F32),
        compiler_params=_cparams(2),
        name="moe_down",
    )(tile_e, n_used, act, w_d, b_d)


def _combine_kernel(dest_ref, h_ref, gate_ref, g_ref, ys_ref, out_ref, buf, sem, *, tm, lp, row0, nst):
    s = pl.program_id(0)
    n_steps = pl.num_programs(0)

    def start_gather(step, slot):
        t0 = (step // nst) * lp + row0 + (step % nst) * tm

        def start_row(r, c):
            for k in range(TOP_K):
                d = dest_ref[(t0 + r) * TOP_K + k]
                pltpu.make_async_copy(ys_ref.at[pl.ds(d, 1), :], buf.at[slot, k, pl.ds(r, 1), :],
                                      sem.at[slot]).start()
            return c

        lax.fori_loop(0, tm, start_row, 0, unroll=8)

    @pl.when(s == 0)
    def _():
        start_gather(0, 0)

    @pl.when(s + 1 < n_steps)
    def _():
        start_gather(s + 1, (s + 1) % 2)

    slot = s % 2
    for k in range(TOP_K):
        pltpu.make_async_copy(ys_ref.at[pl.ds(0, tm), :], buf.at[slot, k], sem.at[slot]).wait()

    gate = gate_ref[...]
    acc = jnp.zeros(h_ref.shape, F32)
    for k in range(TOP_K):
        acc = acc + gate[:, k:k + 1] * buf[slot, k]
    h = h_ref[...] + acc
    ms = jnp.mean(h * h, axis=-1, keepdims=True)
    out_ref[...] = (h * lax.rsqrt(ms + EPS) * g_ref[...]).astype(out_ref.dtype)


def _combine(dest_flat, h2, gate, g, ys, *, batch, seq, lp, row0, tm):
    d = h2.shape[1]
    nb_b = lp // tm
    nb0 = row0 // tm
    nst = seq // tm
    blk = lambda s: (s // nst) * nb_b + nb0 + s % nst
    grid_spec = pltpu.PrefetchScalarGridSpec(
        num_scalar_prefetch=1,
        grid=(batch * nst,),
        in_specs=[pl.BlockSpec((tm, d), lambda s, dest: (blk(s), 0)),
                  pl.BlockSpec((tm, TOP_K), lambda s, dest: (blk(s), 0)),
                  pl.BlockSpec((1, d), lambda s, dest: (0, 0)),
                  pl.BlockSpec(memory_space=pl.ANY)],
        out_specs=pl.BlockSpec((None, tm, d), lambda s, dest: (s // nst, s % nst, 0)),
        scratch_shapes=[pltpu.VMEM((2, TOP_K, tm, d), F32), pltpu.SemaphoreType.DMA((2,))],
    )
    return pl.pallas_call(
        functools.partial(_combine_kernel, tm=tm, lp=lp, row0=row0, nst=nst),
        grid_spec=grid_spec,
        out_shape=jax.ShapeDtypeStruct((batch, seq, d), F32),
        compiler_params=_cparams(1),
        name="moe_combine",
    )(dest_flat, h2, gate, g.reshape(1, d).astype(F32), ys)


def _pick(pref, n):
    t = pref
    while n % t:
        t //= 2
    return t


def _row_tile(n, pref):
    t = pref // ROW_ALIGN * ROW_ALIGN
    while n % t:
        t -= ROW_ALIGN
    return t


def kernel(x, meta_tokens, norm_mix_g, w_in, b_gate, lambda_q1, lambda_k1, lambda_q2, lambda_k2, head_norm_g, w_attn_out, conv_w, conv_b, conv_ln_g, conv_ln_b, w_conv_out, b_conv_out, w_out, norm_ffn_g, w_router, b_router, w_gate_up, b_gate_up, w_down, b_down, final_norm_g):
    batch, seq, d = x.shape
    depth = w_in.shape[0]
    assert depth == 1 and seq % ROW_ALIGN == 0 and N_META <= ROW_ALIGN
    n_heads = d // 256
    hw = 2 * HEAD_DIM
    qk_w = n_heads * hw
    conv_ch = conv_w.shape[2]
    n_pad = ROW_ALIGN - N_META
    lp = n_pad + N_META + seq
    tp = batch * lp
    f = w_down.shape[2]
    layer = 0
    lam_init = 0.8 - 0.6 * math.exp(-0.3 * layer)

    h0, u = _embed_norm(x, meta_tokens, norm_mix_g[layer], n_pad=n_pad)

    proj = _matmul(u, w_in[layer], BF16, _row_tile(tp, 1536), _pick(512, w_in.shape[2]), name="in_proj")

    tq = 384 if lp % 384 == 0 else ROW_ALIGN
    vt = proj[:, 2 * qk_w:3 * qk_w].reshape(batch, lp, n_heads, hw).transpose(0, 2, 3, 1)
    o = _attention(proj, vt, lambda_q1[layer], lambda_k1[layer], lambda_q2[layer], lambda_k2[layer],
                   head_norm_g[layer], batch=batch, lp=lp, n_heads=n_heads, tq=tq, lam_init=lam_init,
                   n_pad=n_pad)
    ca_col = 3 * qk_w
    z = _conv_branch(proj, conv_w[layer], conv_b[layer], conv_ln_g[layer], conv_ln_b[layer],
                     ca_blk=ca_col // conv_ch, cg_blk=ca_col // conv_ch + 1, tm=_pick(256, tp))
    merged = _merge(o, z, proj, w_attn_out[layer], w_conv_out[layer],
                    b_conv_out[layer], b_gate[layer], gate_col=ca_col + 2 * conv_ch,
                    tm=_row_tile(tp, 768), tn=_pick(512, d))
    h2 = _matmul(merged, w_out[layer], F32, _row_tile(tp, 768), _pick(512, d), res=h0, name="out_proj")

    pos = np.arange(tp) % lp
    valid_np = pos >= n_pad
    valid = jnp.asarray(valid_np.astype(np.float32).reshape(1, tp))
    u_packed, top_e, gate_t, rank_t, cnt = _router(h2, norm_ffn_g[layer], w_router[layer], b_router[layer],
                                                   valid, tm=_pick(256, tp))
    counts = cnt[:, 0]
    padded = (counts + MOE_TILE - 1) // MOE_TILE * MOE_TILE
    e_ids = np.arange(N_EXPERTS)
    pad_end = jnp.sum(jnp.where(jnp.asarray(e_ids[None, :] <= e_ids[:, None]), padded[None, :], 0), axis=1)
    pad_start = pad_end - padded
    start_tok = jnp.sum(jnp.where(top_e[:, :, None] == jnp.asarray(e_ids, jnp.int32), pad_start, 0), axis=-1)
    n_real = int(valid_np.sum()) * TOP_K
    n_tiles = -(-(n_real + N_EXPERTS * (MOE_TILE - 1)) // MOE_TILE)
    n_slots = n_tiles * MOE_TILE
    dump = n_slots + (np.cumsum(~valid_np) - 1)[None, :] * TOP_K + np.arange(TOP_K)[:, None]
    dest_t = jnp.where(jnp.asarray(valid_np)[None, :], start_tok + rank_t, jnp.asarray(dump, jnp.int32))
    dest_flat = dest_t.T.reshape(-1).astype(jnp.int32)
    n_dump = int((~valid_np).sum()) * TOP_K
    n_used = (pad_end[-1] // MOE_TILE).astype(jnp.int32).reshape(1)
    tile_start = jnp.asarray(np.arange(n_tiles, dtype=np.int32) * MOE_TILE)
    tile_e = jnp.minimum(jnp.sum((pad_end[None, :] <= tile_start[:, None]).astype(jnp.int32), axis=1),
                         N_EXPERTS - 1).astype(jnp.int32)

    xs = _dispatch(dest_flat, tile_e, n_used, u_packed, n_slots + n_dump, tm=ROW_ALIGN, n_tiles=n_tiles)
    act = _gate_up(tile_e, n_used, xs, w_gate_up[layer],
                   b_gate_up[layer].reshape(N_EXPERTS, 1, 2 * f).astype(F32), n_tiles=n_tiles, tf=_row_tile(f, 384))
    ys = _down(tile_e, n_used, act, w_down[layer],
               b_down[layer].reshape(N_EXPERTS, 1, d).astype(F32), n_tiles=n_tiles, tn=_pick(2048, d))
    return _combine(dest_flat, h2, gate_t.T, final_norm_g, ys, batch=batch, seq=seq, lp=lp,
                    row0=ROW_ALIGN, tm=ROW_ALIGN)
```

```python
import functools
import math

import numpy as np
import jax
import jax.numpy as jnp
from jax import lax
from jax.experimental import pallas as pl
from jax.experimental.pallas import tpu as pltpu

N_META = 16
HEAD_DIM = 64
N_EXPERTS = 32
TOP_K = 4
CONV_K = 31
EPS = 1e-5
SWIGLU_LIMIT = 7.0
SWIGLU_ALPHA = 1.702
SUBLANES = 8
LANES = 128
ROW_ALIGN = 128
CONV_HALO = 32
MOE_TILE = 256
ATTN_HEADS_PER_STEP = 4
VMEM_LIMIT = 56 * 1024 * 1024

F32 = jnp.float32
BF16 = jnp.bfloat16


def _cparams(n_axes, flags=None):
    return pltpu.CompilerParams(dimension_semantics=("arbitrary",) * n_axes,
                                vmem_limit_bytes=VMEM_LIMIT, flags=flags)


def _embed_norm_kernel(x_ref, meta_ref, g_ref, h_ref, u_ref, *, n_pad):
    i = pl.program_id(1)

    @pl.when(i == 0)
    def _():
        h_ref[0:n_pad, :] = jnp.zeros((n_pad, h_ref.shape[1]), F32)
        h_ref[n_pad:, :] = meta_ref[...]

    @pl.when(i > 0)
    def _():
        h_ref[...] = x_ref[...]

    h = h_ref[...]
    ms = jnp.mean(h * h, axis=-1, keepdims=True)
    u_ref[...] = (h * lax.rsqrt(ms + EPS) * g_ref[...]).astype(u_ref.dtype)


def _embed_norm(x, meta, g, *, n_pad):
    batch, seq, d = x.shape
    tm = ROW_ALIGN
    nb = (n_pad + N_META + seq) // tm
    out_spec = pl.BlockSpec((tm, d), lambda b, i: (b * nb + i, 0))
    return pl.pallas_call(
        functools.partial(_embed_norm_kernel, n_pad=n_pad),
        grid=(batch, nb),
        in_specs=[pl.BlockSpec((None, tm, d), lambda b, i: (b, jnp.maximum(i - 1, 0), 0)),
                  pl.BlockSpec((N_META, d), lambda b, i: (0, 0)),
                  pl.BlockSpec((1, d), lambda b, i: (0, 0))],
        out_specs=[out_spec, out_spec],
        out_shape=[jax.ShapeDtypeStruct((batch * nb * tm, d), F32),
                   jax.ShapeDtypeStruct((batch * nb * tm, d), BF16)],
        compiler_params=_cparams(2),
        name="embed_norm",
    )(x, meta.astype(F32), g.reshape(1, d).astype(F32))


def _cast_weight_once(w_ref, wb_sc):
    @pl.when(pl.program_id(1) == 0)
    def _():
        wb_sc[...] = w_ref[...].astype(BF16)


def _matmul_kernel(a_ref, w_ref, o_ref, wb_sc):
    _cast_weight_once(w_ref, wb_sc)
    o_ref[...] = jnp.dot(a_ref[...], wb_sc[...], preferred_element_type=F32).astype(o_ref.dtype)


def _matmul_res_kernel(a_ref, w_ref, r_ref, o_ref, wb_sc):
    _cast_weight_once(w_ref, wb_sc)
    acc = jnp.dot(a_ref[...], wb_sc[...], preferred_element_type=F32)
    o_ref[...] = (acc + r_ref[...]).astype(o_ref.dtype)


def _matmul(a, w, out_dtype, tm, tn, res=None, name="matmul"):
    m, k = a.shape
    n = w.shape[1]
    in_specs = [pl.BlockSpec((tm, k), lambda j, i: (i, 0)),
                pl.BlockSpec((k, tn), lambda j, i: (0, j))]
    args = [a, w]
    kern = _matmul_kernel
    if res is not None:
        in_specs.append(pl.BlockSpec((tm, tn), lambda j, i: (i, j)))
        args.append(res)
        kern = _matmul_res_kernel
    return pl.pallas_call(
        kern,
        grid=(n // tn, m // tm),
        in_specs=in_specs,
        out_specs=pl.BlockSpec((tm, tn), lambda j, i: (i, j)),
        out_shape=jax.ShapeDtypeStruct((m, n), out_dtype),
        scratch_shapes=[pltpu.VMEM((k, tn), BF16)],
        compiler_params=_cparams(2),
        name=name,
    )(*args)


def _attn_kernel(lq1_ref, lk1_ref, lq2_ref, lk2_ref, hg_ref, bias_ref, q_ref, k_ref, vt_ref, o_ref,
                 q12_sc, s_sc, m_sc, l_sc, acc_sc, *, tq, lam_init, n_hd):
    hw = 2 * HEAD_DIM
    nq = q_ref.shape[0] // tq
    lam = (jnp.exp(jnp.sum(lq1_ref[...] * lk1_ref[...], axis=-1, keepdims=True))
           - jnp.exp(jnp.sum(lq2_ref[...] * lk2_ref[...], axis=-1, keepdims=True)) + lam_init)

    def q_tile(qi, carry):
        q0 = pl.multiple_of(qi * tq, tq)
        for hd in range(n_hd):
            q = (q_ref[pl.ds(q0, tq), hd * hw:(hd + 1) * hw].astype(F32)
                 * (HEAD_DIM ** -0.5 * math.log2(math.e))).astype(BF16)
            lane = lax.broadcasted_iota(jnp.int32, q.shape, 1)
            zero = jnp.zeros_like(q)
            q12_sc[2 * hd] = jnp.where(lane < HEAD_DIM, q, zero)
            q12_sc[2 * hd + 1] = jnp.where(lane >= HEAD_DIM, q, zero)
        m_sc[...] = jnp.full(m_sc.shape, -jnp.inf, F32)
        l_sc[...] = jnp.zeros(l_sc.shape, F32)
        acc_sc[...] = jnp.zeros(acc_sc.shape, F32)

        def scores(kj, c):
            hd = c // 2
            k0 = pl.multiple_of(kj * tq, tq)
            s = lax.dot_general(k_ref[pl.ds(k0, tq), hd * hw:(hd + 1) * hw], q12_sc[c],
                                (((1,), (1,)), ((), ())), preferred_element_type=F32)
            kind = jnp.where(kj == 0, 1, 0) + jnp.where(kj == qi, 2, 0)
            s_sc[c] = s + bias_ref[kind]

        def update(kj, c):
            hd = c // 2
            k0 = pl.multiple_of(kj * tq, tq)
            s = s_sc[c]
            m_prev = m_sc[c]
            m_new = jnp.maximum(m_prev, jnp.max(s, axis=0, keepdims=True))
            alpha = jnp.exp2(m_prev - m_new)
            p = jnp.exp2(s - m_new)
            l_sc[c] = alpha * l_sc[c] + jnp.sum(p.reshape(tq // 8, 8, tq), axis=0)
            acc_sc[c] = alpha * acc_sc[c] + jnp.dot(vt_ref[hd, :, pl.ds(k0, tq)], p.astype(BF16),
                                                    preferred_element_type=F32)
            m_sc[c] = m_new

        for hd in range(n_hd):
            scores(0, 2 * hd)

        def body(kj, c):
            nxt = jnp.minimum(kj + 1, qi)
            for hd in range(n_hd):
                scores(kj, 2 * hd + 1)
                update(kj, 2 * hd)
                scores(nxt, 2 * hd)
            for hd in range(n_hd):
                update(kj, 2 * hd + 1)
            return c

        lax.fori_loop(0, qi + 1, body, 0)

        for hd in range(n_hd):
            o1 = acc_sc[2 * hd] / jnp.sum(l_sc[2 * hd], axis=0, keepdims=True)
            o2 = acc_sc[2 * hd + 1] / jnp.sum(l_sc[2 * hd + 1], axis=0, keepdims=True)
            o = o1 - lam * o2
            ms = jnp.mean(o * o, axis=0, keepdims=True)
            o = o * lax.rsqrt(ms + EPS) * hg_ref[...] * (1.0 - lam_init)
            o_ref[pl.ds(q0, tq), hd * hw:(hd + 1) * hw] = o.T.astype(o_ref.dtype)
        return carry

    lax.fori_loop(0, nq, q_tile, 0)


def _attn_bias(tq, n_pad):
    neg = np.float32(np.finfo(np.float32).min)
    r = np.arange(tq)[:, None]
    c = np.arange(tq)[None, :]
    pad = np.broadcast_to(r < n_pad, (tq, tq))
    future = r > c
    tiles = [np.zeros((tq, tq), bool), pad, future, pad | future]
    return jnp.asarray(np.stack([np.where(t, neg, np.float32(0)) for t in tiles]).astype(np.float32))


def _attention(proj, vt, lq1, lk1, lq2, lk2, head_g, *, batch, lp, n_heads, tq, lam_init, n_pad):
    hw = 2 * HEAD_DIM
    n_hd = ATTN_HEADS_PER_STEP if n_heads % ATTN_HEADS_PER_STEP == 0 else 1
    assert n_pad <= tq
    koff = n_heads // n_hd
    vec = lambda a: a.reshape(1, -1).astype(F32)
    small = lambda n: pl.BlockSpec((1, n), lambda b, h: (0, 0))
    return pl.pallas_call(
        functools.partial(_attn_kernel, tq=tq, lam_init=lam_init, n_hd=n_hd),
        grid=(batch, n_heads // n_hd),
        in_specs=[small(HEAD_DIM), small(HEAD_DIM), small(HEAD_DIM), small(HEAD_DIM),
                  pl.BlockSpec((hw, 1), lambda b, h: (0, 0)),
                  pl.BlockSpec((4, tq, tq), lambda b, h: (0, 0, 0)),
                  pl.BlockSpec((lp, n_hd * hw), lambda b, h: (b, h)),
                  pl.BlockSpec((lp, n_hd * hw), lambda b, h: (b, koff + h)),
                  pl.BlockSpec((None, n_hd, hw, lp), lambda b, h: (b, h, 0, 0))],
        out_specs=pl.BlockSpec((lp, n_hd * hw), lambda b, h: (b, h)),
        out_shape=jax.ShapeDtypeStruct((batch * lp, n_heads * hw), BF16),
        scratch_shapes=[pltpu.VMEM((2 * n_hd, tq, hw), BF16),
                        pltpu.VMEM((2 * n_hd, tq, tq), F32),
                        pltpu.VMEM((2 * n_hd, 1, tq), F32), pltpu.VMEM((2 * n_hd, 8, tq), F32),
                        pltpu.VMEM((2 * n_hd, hw, tq), F32)],
        compiler_params=_cparams(2),
        name="diff_attention",
    )(vec(lq1), vec(lk1), vec(lq2), vec(lk2), head_g.reshape(hw, 1).astype(F32), _attn_bias(tq, n_pad),
      proj, proj, vt)


def _conv_kernel(ca_ref, cg_ref, ca_h_ref, cg_h_ref, w_ref, b_ref, lg_ref, lb_ref, z_ref,
                 ext_sc, sh_sc, y_sc, *, tm):
    i = pl.program_id(0)
    n_ch = w_ref.shape[1]
    n_ext = CONV_HALO + tm
    glu = lambda a, g: a.astype(F32) * jax.nn.sigmoid(g.astype(F32))
    halo = glu(ca_h_ref[...], cg_h_ref[...])
    ext_sc[0:CONV_HALO, :] = jnp.where(i > 0, halo, jnp.zeros_like(halo))
    ext_sc[CONV_HALO:n_ext, :] = glu(ca_ref[...], cg_ref[...])
    ext_sc[n_ext:n_ext + SUBLANES, :] = jnp.zeros((SUBLANES, n_ch), F32)
    base = CONV_HALO - (CONV_K - 1)

    def slab(lc, carry):
        l0 = pl.multiple_of(lc * LANES, LANES)
        for rho in range(SUBLANES):
            sh_sc[rho] = ext_sc[rho:rho + n_ext, pl.ds(l0, LANES)]
        acc = jnp.zeros((tm, LANES), F32) + b_ref[:, pl.ds(l0, LANES)]
        for j in range(CONV_K):
            rho = (base + j) % SUBLANES
            a = base + j - rho
            acc = acc + w_ref[j:j + 1, pl.ds(l0, LANES)] * sh_sc[rho, a:a + tm, :]
        y_sc[:, pl.ds(l0, LANES)] = acc
        return carry

    lax.fori_loop(0, n_ch // LANES, slab, 0)
    acc = y_sc[...]
    mu = jnp.mean(acc, axis=-1, keepdims=True)
    d = acc - mu
    var = jnp.mean(d * d, axis=-1, keepdims=True)
    y = d * lax.rsqrt(var + EPS) * lg_ref[...] + lb_ref[...]
    z_ref[...] = (y * jax.nn.sigmoid(y)).astype(z_ref.dtype)


def _conv_branch(proj, conv_w, conv_b, ln_g, ln_b, *, ca_blk, cg_blk, tm):
    m = proj.shape[0]
    c = conv_w.shape[1]
    hb = tm // CONV_HALO
    row = lambda a: a.reshape(1, c).astype(F32)
    vec = pl.BlockSpec((1, c), lambda i: (0, 0))
    return pl.pallas_call(
        functools.partial(_conv_kernel, tm=tm),
        grid=(m // tm,),
        in_specs=[pl.BlockSpec((tm, c), lambda i: (i, ca_blk)),
                  pl.BlockSpec((tm, c), lambda i: (i, cg_blk)),
                  pl.BlockSpec((CONV_HALO, c), lambda i: (jnp.maximum(i * hb - 1, 0), ca_blk)),
                  pl.BlockSpec((CONV_HALO, c), lambda i: (jnp.maximum(i * hb - 1, 0), cg_blk)),
                  pl.BlockSpec((CONV_K, c), lambda i: (0, 0)),
                  vec, vec, vec],
        out_specs=pl.BlockSpec((tm, c), lambda i: (i, 0)),
        out_shape=jax.ShapeDtypeStruct((m, c), BF16),
        scratch_shapes=[pltpu.VMEM((CONV_HALO + tm + SUBLANES, c), F32),
                        pltpu.VMEM((SUBLANES, CONV_HALO + tm, LANES), F32),
                        pltpu.VMEM((tm, c), F32)],
        compiler_params=_cparams(1),
        name="conformer_conv",
    )(proj, proj, proj, proj, conv_w.astype(F32), row(conv_b), row(ln_g), row(ln_b))


def _merge_kernel(o_ref, z_ref, wa_ref, wc_ref, bc_ref, g1_ref, g2_ref, bg1_ref, bg2_ref, out_ref,
                  wab_sc, wcb_sc):
    _cast_weight_once(wa_ref, wab_sc)
    _cast_weight_once(wc_ref, wcb_sc)
    ya = jnp.dot(o_ref[...], wab_sc[...], preferred_element_type=F32)
    yc = jnp.dot(z_ref[...], wcb_sc[...], preferred_element_type=F32) + bc_ref[...]
    g1 = jax.nn.sigmoid(g1_ref[...].astype(F32) + bg1_ref[...])
    g2 = jax.nn.sigmoid(g2_ref[...].astype(F32) + bg2_ref[...])
    out_ref[...] = (g1 * ya + g2 * yc).astype(out_ref.dtype)


def _merge(o, z, proj, wa, wc, bc, bg, *, gate_col, tm, tn):
    m, ka = o.shape
    kc = z.shape[1]
    d = wa.shape[1]
    g1_blk = gate_col // tn
    g2_blk = (gate_col + d) // tn
    nb = d // tn
    bg2 = bg.reshape(1, 2 * d).astype(F32)
    return pl.pallas_call(
        _merge_kernel,
        grid=(nb, m // tm),
        in_specs=[pl.BlockSpec((tm, ka), lambda j, i: (i, 0)),
                  pl.BlockSpec((tm, kc), lambda j, i: (i, 0)),
                  pl.BlockSpec((ka, tn), lambda j, i: (0, j)),
                  pl.BlockSpec((kc, tn), lambda j, i: (0, j)),
                  pl.BlockSpec((1, tn), lambda j, i: (0, j)),
                  pl.BlockSpec((tm, tn), lambda j, i: (i, g1_blk + j)),
                  pl.BlockSpec((tm, tn), lambda j, i: (i, g2_blk + j)),
                  pl.BlockSpec((1, tn), lambda j, i: (0, j)),
                  pl.BlockSpec((1, tn), lambda j, i: (0, nb + j))],
        out_specs=pl.BlockSpec((tm, tn), lambda j, i: (i, j)),
        out_shape=jax.ShapeDtypeStruct((m, d), BF16),
        scratch_shapes=[pltpu.VMEM((ka, tn), BF16), pltpu.VMEM((kc, tn), BF16)],
        compiler_params=_cparams(2),
        name="mixer_merge",
    )(o, z, wa, wc, bc.reshape(1, d).astype(F32), proj, proj, bg2, bg2)


def _router_kernel(h_ref, g_ref, wr_ref, br_ref, valid_ref,
                   up_ref, e_ref, gate_ref, rank_ref, cnt_ref, carry_sc, *, tm):
    i = pl.program_id(0)

    @pl.when(i == 0)
    def _():
        carry_sc[...] = jnp.zeros(carry_sc.shape, F32)

    h = h_ref[...]
    ms = jnp.mean(h * h, axis=-1, keepdims=True)
    u = h * lax.rsqrt(ms + EPS) * g_ref[...]

    half = u.shape[1] // 2
    bits = lax.bitcast_convert_type(u.astype(BF16).astype(F32), jnp.uint32)
    up_ref[...] = (bits[:, half:] & jnp.uint32(0xFFFF0000)) | (bits[:, :half] >> 16)

    logits = lax.dot_general(wr_ref[...], u, (((1,), (1,)), ((), ())),
                             precision=lax.Precision.HIGHEST,
                             preferred_element_type=F32) + br_ref[...]
    n_e = logits.shape[0]
    eiota = lax.broadcasted_iota(jnp.int32, logits.shape, 0).astype(F32)
    work = logits
    sel = jnp.zeros(logits.shape, jnp.bool_)
    top_l, top_e = [], []
    for _ in range(TOP_K):
        mx = jnp.max(work, axis=0, keepdims=True)
        idx = jnp.min(jnp.where(work == mx, eiota, float(n_e)), axis=0, keepdims=True)
        hit = eiota == idx
        top_l.append(mx)
        top_e.append(idx)
        sel = sel | hit
        work = jnp.where(hit, -jnp.inf, work)
    ex = [jnp.exp(t - top_l[0]) for t in top_l]
    den = ex[0] + ex[1] + ex[2] + ex[3]
    gate_ref[...] = jnp.concatenate([e / den for e in ex], axis=0)
    e_ref[...] = jnp.concatenate(top_e, axis=0).astype(jnp.int32)

    selv = jnp.where(sel & (valid_ref[...] > 0.0), 1.0, 0.0)
    before = (lax.broadcasted_iota(jnp.int32, (tm, tm), 0)
              < lax.broadcasted_iota(jnp.int32, (tm, tm), 1)).astype(BF16)
    rank_all = jnp.dot(selv.astype(BF16), before, preferred_element_type=F32) + carry_sc[...]
    ranks = [jnp.sum(jnp.where(eiota == idx, rank_all, 0.0), axis=0, keepdims=True) for idx in top_e]
    rank_ref[...] = jnp.concatenate(ranks, axis=0).astype(jnp.int32)
    carry = carry_sc[...] + jnp.sum(selv, axis=1, keepdims=True)
    carry_sc[...] = carry
    cnt_ref[...] = jnp.broadcast_to(carry, cnt_ref.shape).astype(jnp.int32)


def _router(h2, g, w_router, b_router, valid, *, tm):
    m, d = h2.shape
    n_e = w_router.shape[1]
    tok = lambda dt: jax.ShapeDtypeStruct((TOP_K, m), dt)
    tok_spec = pl.BlockSpec((TOP_K, tm), lambda i: (0, i))
    return pl.pallas_call(
        functools.partial(_router_kernel, tm=tm),
        grid=(m // tm,),
        in_specs=[pl.BlockSpec((tm, d), lambda i: (i, 0)),
                  pl.BlockSpec((1, d), lambda i: (0, 0)),
                  pl.BlockSpec((n_e, d), lambda i: (0, 0)),
                  pl.BlockSpec((n_e, 1), lambda i: (0, 0)),
                  pl.BlockSpec((1, tm), lambda i: (0, i))],
        out_specs=[pl.BlockSpec((tm, d // 2), lambda i: (i, 0)),
                   tok_spec, tok_spec, tok_spec,
                   pl.BlockSpec((n_e, 128), lambda i: (0, 0))],
        out_shape=[jax.ShapeDtypeStruct((m, d // 2), jnp.uint32),
                   tok(jnp.int32), tok(F32), tok(jnp.int32),
                   jax.ShapeDtypeStruct((n_e, 128), jnp.int32)],
        scratch_shapes=[pltpu.VMEM((n_e, 1), F32)],
        compiler_params=_cparams(1),
        name="router",
    )(h2, g.reshape(1, d).astype(F32), w_router.T.astype(F32), b_router.reshape(n_e, 1).astype(F32), valid)


def _dispatch_kernel(dest_ref, te_ref, nu_ref, u_ref, xs_ref, zero_sc, sem, zsem, *, tm, n_tiles):
    i = pl.program_id(0)

    @pl.when(i == 0)
    def _():
        zero_sc[...] = jnp.zeros(zero_sc.shape, zero_sc.dtype)
        nu = nu_ref[0]

        def partly_filled(t):
            nxt = te_ref[jnp.minimum(t + 1, n_tiles - 1)]
            return (t >= nu - 1) | (te_ref[t] != nxt)

        def tile_copy(t):
            return pltpu.make_async_copy(zero_sc, xs_ref.at[pl.ds(t * MOE_TILE, MOE_TILE), :], zsem)

        def start(t, c):
            @pl.when(partly_filled(t))
            def _():
                tile_copy(t).start()
            return c

        def wait(t, c):
            @pl.when(partly_filled(t))
            def _():
                tile_copy(t).wait()
            return c

        lax.fori_loop(0, n_tiles, start, 0)
        lax.fori_loop(0, n_tiles, wait, 0)

    def start_row(r, c):
        for k in range(TOP_K):
            d = dest_ref[(i * tm + r) * TOP_K + k]
            pltpu.make_async_copy(u_ref.at[pl.ds(r, 1), :], xs_ref.at[pl.ds(d, 1), :], sem).start()
        return c

    lax.fori_loop(0, tm, start_row, 0, unroll=8)
    for k in range(TOP_K):
        pltpu.make_async_copy(u_ref, xs_ref.at[pl.ds(0, tm), :], sem).wait()


def _dispatch(dest_flat, tile_e, n_used, u_packed, n_rows, *, tm, n_tiles):
    m, w = u_packed.shape
    grid_spec = pltpu.PrefetchScalarGridSpec(
        num_scalar_prefetch=3,
        grid=(m // tm,),
        in_specs=[pl.BlockSpec((tm, w), lambda i, dest, te, nu: (i, 0))],
        out_specs=pl.BlockSpec(memory_space=pl.ANY),
        scratch_shapes=[pltpu.VMEM((MOE_TILE, w), jnp.uint32),
                        pltpu.SemaphoreType.DMA(()), pltpu.SemaphoreType.DMA(())],
    )
    return pl.pallas_call(
        functools.partial(_dispatch_kernel, tm=tm, n_tiles=n_tiles),
        grid_spec=grid_spec,
        out_shape=jax.ShapeDtypeStruct((n_rows, w), jnp.uint32),
        compiler_params=_cparams(1),
        name="moe_dispatch",
    )(dest_flat, tile_e, n_used, u_packed)


def _new_expert(te_ref, nu_ref, r):
    return (r < nu_ref[0]) & ((r == 0) | (te_ref[r] != te_ref[jnp.maximum(r - 1, 0)]))


def _gate_up_kernel(te_ref, nu_ref, xs_ref, wg_ref, wu_ref, bg_ref, bu_ref, act_ref, wgb_sc, wub_sc):
    r = pl.program_id(1)

    @pl.when(_new_expert(te_ref, nu_ref, r))
    def _():
        wgb_sc[...] = wg_ref[...].astype(BF16)
        wub_sc[...] = wu_ref[...].astype(BF16)

    @pl.when(r < nu_ref[0])
    def _():
        w = xs_ref[...]
        half = w.shape[1]
        lo = lax.bitcast_convert_type(w << 16, F32).astype(BF16)
        hi = lax.bitcast_convert_type(w & jnp.uint32(0xFFFF0000), F32).astype(BF16)
        g = (jnp.dot(lo, wgb_sc[:half, :], preferred_element_type=F32)
             + jnp.dot(hi, wgb_sc[half:, :], preferred_element_type=F32) + bg_ref[...])
        u = (jnp.dot(lo, wub_sc[:half, :], preferred_element_type=F32)
             + jnp.dot(hi, wub_sc[half:, :], preferred_element_type=F32) + bu_ref[...])
        g = jnp.minimum(g, SWIGLU_LIMIT)
        u = jnp.clip(u, -SWIGLU_LIMIT, SWIGLU_LIMIT)
        act_ref[...] = ((u + 1.0) * (g * jax.nn.sigmoid(SWIGLU_ALPHA * g))).astype(act_ref.dtype)

    @pl.when(r >= nu_ref[0])
    def _():
        act_ref[...] = jnp.zeros(act_ref.shape, act_ref.dtype)


def _gate_up(tile_e, n_used, xs, w_gu, b_gu, *, n_tiles, tf):
    n_e, d, f2 = w_gu.shape
    f = f2 // 2
    nj = f // tf
    rr = lambda r, nu: jnp.maximum(jnp.minimum(r, nu[0] - 1), 0)
    grid_spec = pltpu.PrefetchScalarGridSpec(
        num_scalar_prefetch=2,
        grid=(nj, n_tiles),
        in_specs=[pl.BlockSpec((MOE_TILE, d // 2), lambda j, r, te, nu: (rr(r, nu), 0)),
                  pl.BlockSpec((None, d, tf), lambda j, r, te, nu: (te[rr(r, nu)], 0, j)),
                  pl.BlockSpec((None, d, tf), lambda j, r, te, nu: (te[rr(r, nu)], 0, nj + j)),
                  pl.BlockSpec((None, 1, tf), lambda j, r, te, nu: (te[rr(r, nu)], 0, j)),
                  pl.BlockSpec((None, 1, tf), lambda j, r, te, nu: (te[rr(r, nu)], 0, nj + j))],
        out_specs=pl.BlockSpec((MOE_TILE, tf), lambda j, r, te, nu: (r, j)),
        scratch_shapes=[pltpu.VMEM((d, tf), BF16), pltpu.VMEM((d, tf), BF16)],
    )
    return pl.pallas_call(
        _gate_up_kernel,
        grid_spec=grid_spec,
        out_shape=jax.ShapeDtypeStruct((n_tiles * MOE_TILE, f), BF16),
        compiler_params=_cparams(2),
        name="moe_gate_up",
    )(tile_e, n_used, xs, w_gu, w_gu, b_gu, b_gu)


def _down_kernel(te_ref, nu_ref, act_ref, wd_ref, bd_ref, y_ref, wdb_sc):
    r = pl.program_id(1)

    @pl.when(_new_expert(te_ref, nu_ref, r))
    def _():
        wdb_sc[...] = wd_ref[...].astype(BF16)

    @pl.when(r < nu_ref[0])
    def _():
        y_ref[...] = jnp.dot(act_ref[...], wdb_sc[...], preferred_element_type=F32) + bd_ref[...]

    @pl.when(r >= nu_ref[0])
    def _():
        y_ref[...] = jnp.zeros(y_ref.shape, y_ref.dtype)


def _down(tile_e, n_used, act, w_d, b_d, *, n_tiles, tn):
    n_e, f, d = w_d.shape
    rr = lambda r, nu: jnp.maximum(jnp.minimum(r, nu[0] - 1), 0)
    grid_spec = pltpu.PrefetchScalarGridSpec(
        num_scalar_prefetch=2,
        grid=(d // tn, n_tiles),
        in_specs=[pl.BlockSpec((MOE_TILE, f), lambda j, r, te, nu: (rr(r, nu), 0)),
                  pl.BlockSpec((None, f, tn), lambda j, r, te, nu: (te[rr(r, nu)], 0, j)),
                  pl.BlockSpec((None, 1, tn), lambda j, r, te, nu: (te[rr(r, nu)], 0, j))],
        out_specs=pl.BlockSpec((MOE_TILE, tn), lambda j, r, te, nu: (r, j)),
        scratch_shapes=[pltpu.VMEM((f, tn), BF16)],
    )
    return pl.pallas_call(
        _down_kernel,
        grid_spec=grid_spec,
        out_shape=jax.ShapeDtypeStruct((n_tiles * MOE_TILE, d), F32),
        compiler_params=_cparams(2),
        name="moe_down",
    )(tile_e, n_used, act, w_d, b_d)


def _combine_kernel(dest_ref, h_ref, gate_ref, g_ref, ys_ref, out_ref, buf, sem, *, tm, lp, row0, nst):
    s = pl.program_id(0)
    n_steps = pl.num_programs(0)

    def start_gather(step, slot):
        t0 = (step // nst) * lp + row0 + (step % nst) * tm

        def start_row(r, c):
            for k in range(TOP_K):
                d = dest_ref[(t0 + r) * TOP_K + k]
                pltpu.make_async_copy(ys_ref.at[pl.ds(d, 1), :], buf.at[slot, k, pl.ds(r, 1), :],
                                      sem.at[slot]).start()
            return c

        lax.fori_loop(0, tm, start_row, 0, unroll=8)

    @pl.when(s == 0)
    def _():
        start_gather(0, 0)

    @pl.when(s + 1 < n_steps)
    def _():
        start_gather(s + 1, (s + 1) % 2)

    slot = s % 2
    for k in range(TOP_K):
        pltpu.make_async_copy(ys_ref.at[pl.ds(0, tm), :], buf.at[slot, k], sem.at[slot]).wait()

    gate = gate_ref[...]
    acc = jnp.zeros(h_ref.shape, F32)
    for k in range(TOP_K):
        acc = acc + gate[:, k:k + 1] * buf[slot, k]
    h = h_ref[...] + acc
    ms = jnp.mean(h * h, axis=-1, keepdims=True)
    out_ref[...] = (h * lax.rsqrt(ms + EPS) * g_ref[...]).astype(out_ref.dtype)


def _combine(dest_flat, h2, gate, g, ys, *, batch, seq, lp, row0, tm):
    d = h2.shape[1]
    nb_b = lp // tm
    nb0 = row0 // tm
    nst = seq // tm
    blk = lambda s: (s // nst) * nb_b + nb0 + s % nst
    grid_spec = pltpu.PrefetchScalarGridSpec(
        num_scalar_prefetch=1,
        grid=(batch * nst,),
        in_specs=[pl.BlockSpec((tm, d), lambda s, dest: (blk(s), 0)),
                  pl.BlockSpec((tm, TOP_K), lambda s, dest: (blk(s), 0)),
                  pl.BlockSpec((1, d), lambda s, dest: (0, 0)),
                  pl.BlockSpec(memory_space=pl.ANY)],
        out_specs=pl.BlockSpec((None, tm, d), lambda s, dest: (s // nst, s % nst, 0)),
        scratch_shapes=[pltpu.VMEM((2, TOP_K, tm, d), F32), pltpu.SemaphoreType.DMA((2,))],
    )
    return pl.pallas_call(
        functools.partial(_combine_kernel, tm=tm, lp=lp, row0=row0, nst=nst),
        grid_spec=grid_spec,
        out_shape=jax.ShapeDtypeStruct((batch, seq, d), F32),
        compiler_params=_cparams(1),
        name="moe_combine",
    )(dest_flat, h2, gate, g.reshape(1, d).astype(F32), ys)


def _pick(pref, n):
    t = pref
    while n % t:
        t //= 2
    return t


def _row_tile(n, pref):
    t = pref // ROW_ALIGN * ROW_ALIGN
    while n % t:
        t -= ROW_ALIGN
    return t


def kernel(x, meta_tokens, norm_mix_g, w_in, b_gate, lambda_q1, lambda_k1, lambda_q2, lambda_k2, head_norm_g, w_attn_out, conv_w, conv_b, conv_ln_g, conv_ln_b, w_conv_out, b_conv_out, w_out, norm_ffn_g, w_router, b_router, w_gate_up, b_gate_up, w_down, b_down, final_norm_g):
    batch, seq, d = x.shape
    depth = w_in.shape[0]
    assert depth == 1 and seq % ROW_ALIGN == 0 and N_META <= ROW_ALIGN
    n_heads = d // 256
    hw = 2 * HEAD_DIM
    qk_w = n_heads * hw
    conv_ch = conv_w.shape[2]
    n_pad = ROW_ALIGN - N_META
    lp = n_pad + N_META + seq
    tp = batch * lp
    f = w_down.shape[2]
    layer = 0
    lam_init = 0.8 - 0.6 * math.exp(-0.3 * layer)

    h0, u = _embed_norm(x, meta_tokens, norm_mix_g[layer], n_pad=n_pad)

    proj = _matmul(u, w_in[layer], BF16, _row_tile(tp, 1536), _pick(512, w_in.shape[2]), name="in_proj")

    tq = 384 if lp % 384 == 0 else ROW_ALIGN
    vt = proj[:, 2 * qk_w:3 * qk_w].reshape(batch, lp, n_heads, hw).transpose(0, 2, 3, 1)
    o = _attention(proj, vt, lambda_q1[layer], lambda_k1[layer], lambda_q2[layer], lambda_k2[layer],
                   head_norm_g[layer], batch=batch, lp=lp, n_heads=n_heads, tq=tq, lam_init=lam_init,
                   n_pad=n_pad)
    ca_col = 3 * qk_w
    z = _conv_branch(proj, conv_w[layer], conv_b[layer], conv_ln_g[layer], conv_ln_b[layer],
                     ca_blk=ca_col // conv_ch, cg_blk=ca_col // conv_ch + 1, tm=_pick(256, tp))
    merged = _merge(o, z, proj, w_attn_out[layer], w_conv_out[layer],
                    b_conv_out[layer], b_gate[layer], gate_col=ca_col + 2 * conv_ch,
                    tm=_row_tile(tp, 768), tn=_pick(512, d))
    h2 = _matmul(merged, w_out[layer], F32, _row_tile(tp, 768), _pick(512, d), res=h0, name="out_proj")

    pos = np.arange(tp) % lp
    valid_np = pos >= n_pad
    valid = jnp.asarray(valid_np.astype(np.float32).reshape(1, tp))
    u_packed, top_e, gate_t, rank_t, cnt = _router(h2, norm_ffn_g[layer], w_router[layer], b_router[layer],
                                                   valid, tm=_pick(256, tp))
    counts = cnt[:, 0]
    padded = (counts + MOE_TILE - 1) // MOE_TILE * MOE_TILE
    e_ids = np.arange(N_EXPERTS)
    pad_end = jnp.sum(jnp.where(jnp.asarray(e_ids[None, :] <= e_ids[:, None]), padded[None, :], 0), axis=1)
    pad_start = pad_end - padded
    start_tok = jnp.sum(jnp.where(top_e[:, :, None] == jnp.asarray(e_ids, jnp.int32), pad_start, 0), axis=-1)
    n_real = int(valid_np.sum()) * TOP_K
    n_tiles = -(-(n_real + N_EXPERTS * (MOE_TILE - 1)) // MOE_TILE)
    n_slots = n_tiles * MOE_TILE
    dump = n_slots + (np.cumsum(~valid_np) - 1)[None, :] * TOP_K + np.arange(TOP_K)[:, None]
    dest_t = jnp.where(jnp.asarray(valid_np)[None, :], start_tok + rank_t, jnp.asarray(dump, jnp.int32))
    dest_flat = dest_t.T.reshape(-1).astype(jnp.int32)
    n_dump = int((~valid_np).sum()) * TOP_K
    n_used = (pad_end[-1] // MOE_TILE).astype(jnp.int32).reshape(1)
    tile_start = jnp.asarray(np.arange(n_tiles, dtype=np.int32) * MOE_TILE)
    tile_e = jnp.minimum(jnp.sum((pad_end[None, :] <= tile_start[:, None]).astype(jnp.int32), axis=1),
                         N_EXPERTS - 1).astype(jnp.int32)

    xs = _dispatch(dest_flat, tile_e, n_used, u_packed, n_slots + n_dump, tm=ROW_ALIGN, n_tiles=n_tiles)
    act = _gate_up(tile_e, n_used, xs, w_gate_up[layer],
                   b_gate_up[layer].reshape(N_EXPERTS, 1, 2 * f).astype(F32), n_tiles=n_tiles, tf=_pick(512, f))
    ys = _down(tile_e, n_used, act, w_down[layer],
               b_down[layer].reshape(N_EXPERTS, 1, d).astype(F32), n_tiles=n_tiles, tn=_pick(2048, d))
    return _combine(dest_flat, h2, gate_t.T, final_norm_g, ys, batch=batch, seq=seq, lp=lp,
                    row0=ROW_ALIGN, tm=ROW_ALIGN)
```

```python
import functools
import math

import numpy as np
import jax
import jax.numpy as jnp
from jax import lax
from jax.experimental import pallas as pl
from jax.experimental.pallas import tpu as pltpu

N_META = 16
HEAD_DIM = 64
N_EXPERTS = 32
TOP_K = 4
CONV_K = 31
EPS = 1e-5
SWIGLU_LIMIT = 7.0
SWIGLU_ALPHA = 1.702
SUBLANES = 8
LANES = 128
ROW_ALIGN = 128
CONV_HALO = 32
MOE_TILE = 256
ATTN_HEADS_PER_STEP = 4
VMEM_LIMIT = 56 * 1024 * 1024

F32 = jnp.float32
BF16 = jnp.bfloat16


def _cparams(n_axes, flags=None):
    return pltpu.CompilerParams(dimension_semantics=("arbitrary",) * n_axes,
                                vmem_limit_bytes=VMEM_LIMIT, flags=flags)


def _embed_norm_kernel(x_ref, meta_ref, g_ref, h_ref, u_ref, *, n_pad):
    i = pl.program_id(1)

    @pl.when(i == 0)
    def _():
        h_ref[0:n_pad, :] = jnp.zeros((n_pad, h_ref.shape[1]), F32)
        h_ref[n_pad:, :] = meta_ref[...]

    @pl.when(i > 0)
    def _():
        h_ref[...] = x_ref[...]

    h = h_ref[...]
    ms = jnp.mean(h * h, axis=-1, keepdims=True)
    u_ref[...] = (h * lax.rsqrt(ms + EPS) * g_ref[...]).astype(u_ref.dtype)


def _embed_norm(x, meta, g, *, n_pad):
    batch, seq, d = x.shape
    tm = ROW_ALIGN
    nb = (n_pad + N_META + seq) // tm
    out_spec = pl.BlockSpec((tm, d), lambda b, i: (b * nb + i, 0))
    return pl.pallas_call(
        functools.partial(_embed_norm_kernel, n_pad=n_pad),
        grid=(batch, nb),
        in_specs=[pl.BlockSpec((None, tm, d), lambda b, i: (b, jnp.maximum(i - 1, 0), 0)),
                  pl.BlockSpec((N_META, d), lambda b, i: (0, 0)),
                  pl.BlockSpec((1, d), lambda b, i: (0, 0))],
        out_specs=[out_spec, out_spec],
        out_shape=[jax.ShapeDtypeStruct((batch * nb * tm, d), F32),
                   jax.ShapeDtypeStruct((batch * nb * tm, d), BF16)],
        compiler_params=_cparams(2),
        name="embed_norm",
    )(x, meta.astype(F32), g.reshape(1, d).astype(F32))


def _cast_weight_once(w_ref, wb_sc):
    @pl.when(pl.program_id(1) == 0)
    def _():
        wb_sc[...] = w_ref[...].astype(BF16)


def _matmul_kernel(a_ref, w_ref, o_ref, wb_sc):
    _cast_weight_once(w_ref, wb_sc)
    o_ref[...] = jnp.dot(a_ref[...], wb_sc[...], preferred_element_type=F32).astype(o_ref.dtype)


def _matmul_res_kernel(a_ref, w_ref, r_ref, o_ref, wb_sc):
    _cast_weight_once(w_ref, wb_sc)
    acc = jnp.dot(a_ref[...], wb_sc[...], preferred_element_type=F32)
    o_ref[...] = (acc + r_ref[...]).astype(o_ref.dtype)


def _matmul(a, w, out_dtype, tm, tn, res=None, name="matmul"):
    m, k = a.shape
    n = w.shape[1]
    in_specs = [pl.BlockSpec((tm, k), lambda j, i: (i, 0)),
                pl.BlockSpec((k, tn), lambda j, i: (0, j))]
    args = [a, w]
    kern = _matmul_kernel
    if res is not None:
        in_specs.append(pl.BlockSpec((tm, tn), lambda j, i: (i, j)))
        args.append(res)
        kern = _matmul_res_kernel
    return pl.pallas_call(
        kern,
        grid=(n // tn, m // tm),
        in_specs=in_specs,
        out_specs=pl.BlockSpec((tm, tn), lambda j, i: (i, j)),
        out_shape=jax.ShapeDtypeStruct((m, n), out_dtype),
        scratch_shapes=[pltpu.VMEM((k, tn), BF16)],
        compiler_params=_cparams(2),
        name=name,
    )(*args)


def _attn_kernel(lq1_ref, lk1_ref, lq2_ref, lk2_ref, hg_ref, bias_ref, q_ref, k_ref, vt_ref, o_ref,
                 q12_sc, s_sc, m_sc, l_sc, acc_sc, *, tq, lam_init, n_hd):
    hw = 2 * HEAD_DIM
    nq = q_ref.shape[0] // tq
    lam = (jnp.exp(jnp.sum(lq1_ref[...] * lk1_ref[...], axis=-1, keepdims=True))
           - jnp.exp(jnp.sum(lq2_ref[...] * lk2_ref[...], axis=-1, keepdims=True)) + lam_init)

    def q_tile(qi, carry):
        q0 = pl.multiple_of(qi * tq, tq)
        for hd in range(n_hd):
            q = (q_ref[pl.ds(q0, tq), hd * hw:(hd + 1) * hw].astype(F32)
                 * (HEAD_DIM ** -0.5 * math.log2(math.e))).astype(BF16)
            lane = lax.broadcasted_iota(jnp.int32, q.shape, 1)
            zero = jnp.zeros_like(q)
            q12_sc[2 * hd] = jnp.where(lane < HEAD_DIM, q, zero)
            q12_sc[2 * hd + 1] = jnp.where(lane >= HEAD_DIM, q, zero)
        m_sc[...] = jnp.full(m_sc.shape, -jnp.inf, F32)
        l_sc[...] = jnp.zeros(l_sc.shape, F32)
        acc_sc[...] = jnp.zeros(acc_sc.shape, F32)

        def scores(kj, c):
            hd = c // 2
            k0 = pl.multiple_of(kj * tq, tq)
            s = lax.dot_general(k_ref[pl.ds(k0, tq), hd * hw:(hd + 1) * hw], q12_sc[c],
                                (((1,), (1,)), ((), ())), preferred_element_type=F32)
            kind = jnp.where(kj == 0, 1, 0) + jnp.where(kj == qi, 2, 0)
            s_sc[c] = s + bias_ref[kind]

        def update(kj, c):
            hd = c // 2
            k0 = pl.multiple_of(kj * tq, tq)
            s = s_sc[c]
            m_prev = m_sc[c]
            m_new = jnp.maximum(m_prev, jnp.max(s, axis=0, keepdims=True))
            alpha = jnp.exp2(m_prev - m_new)
            p = jnp.exp2(s - m_new)
            l_sc[c] = alpha * l_sc[c] + jnp.sum(p.reshape(tq // 8, 8, tq), axis=0)
            acc_sc[c] = alpha * acc_sc[c] + jnp.dot(vt_ref[hd, :, pl.ds(k0, tq)], p.astype(BF16),
                                                    preferred_element_type=F32)
            m_sc[c] = m_new

        for hd in range(n_hd):
            scores(0, 2 * hd)

        def body(kj, c):
            nxt = jnp.minimum(kj + 1, qi)
            for hd in range(n_hd):
                scores(kj, 2 * hd + 1)
                update(kj, 2 * hd)
                scores(nxt, 2 * hd)
            for hd in range(n_hd):
                update(kj, 2 * hd + 1)
            return c

        lax.fori_loop(0, qi + 1, body, 0)

        for hd in range(n_hd):
            o1 = acc_sc[2 * hd] / jnp.sum(l_sc[2 * hd], axis=0, keepdims=True)
            o2 = acc_sc[2 * hd + 1] / jnp.sum(l_sc[2 * hd + 1], axis=0, keepdims=True)
            o = o1 - lam * o2
            ms = jnp.mean(o * o, axis=0, keepdims=True)
            o = o * lax.rsqrt(ms + EPS) * hg_ref[...] * (1.0 - lam_init)
            o_ref[pl.ds(q0, tq), hd * hw:(hd + 1) * hw] = o.T.astype(o_ref.dtype)
        return carry

    lax.fori_loop(0, nq, q_tile, 0)


def _attn_bias(tq, n_pad):
    neg = np.float32(np.finfo(np.float32).min)
    r = np.arange(tq)[:, None]
    c = np.arange(tq)[None, :]
    pad = np.broadcast_to(r < n_pad, (tq, tq))
    future = r > c
    tiles = [np.zeros((tq, tq), bool), pad, future, pad | future]
    return jnp.asarray(np.stack([np.where(t, neg, np.float32(0)) for t in tiles]).astype(np.float32))


def _attention(proj, vt, lq1, lk1, lq2, lk2, head_g, *, batch, lp, n_heads, tq, lam_init, n_pad):
    hw = 2 * HEAD_DIM
    n_hd = ATTN_HEADS_PER_STEP if n_heads % ATTN_HEADS_PER_STEP == 0 else 1
    assert n_pad <= tq
    koff = n_heads // n_hd
    vec = lambda a: a.reshape(1, -1).astype(F32)
    small = lambda n: pl.BlockSpec((1, n), lambda b, h: (0, 0))
    return pl.pallas_call(
        functools.partial(_attn_kernel, tq=tq, lam_init=lam_init, n_hd=n_hd),
        grid=(batch, n_heads // n_hd),
        in_specs=[small(HEAD_DIM), small(HEAD_DIM), small(HEAD_DIM), small(HEAD_DIM),
                  pl.BlockSpec((hw, 1), lambda b, h: (0, 0)),
                  pl.BlockSpec((4, tq, tq), lambda b, h: (0, 0, 0)),
                  pl.BlockSpec((lp, n_hd * hw), lambda b, h: (b, h)),
                  pl.BlockSpec((lp, n_hd * hw), lambda b, h: (b, koff + h)),
                  pl.BlockSpec((None, n_hd, hw, lp), lambda b, h: (b, h, 0, 0))],
        out_specs=pl.BlockSpec((lp, n_hd * hw), lambda b, h: (b, h)),
        out_shape=jax.ShapeDtypeStruct((batch * lp, n_heads * hw), BF16),
        scratch_shapes=[pltpu.VMEM((2 * n_hd, tq, hw), BF16),
                        pltpu.VMEM((2 * n_hd, tq, tq), F32),
                        pltpu.VMEM((2 * n_hd, 1, tq), F32), pltpu.VMEM((2 * n_hd, 8, tq), F32),
                        pltpu.VMEM((2 * n_hd, hw, tq), F32)],
        compiler_params=_cparams(2),
        name="diff_attention",
    )(vec(lq1), vec(lk1), vec(lq2), vec(lk2), head_g.reshape(hw, 1).astype(F32), _attn_bias(tq, n_pad),
      proj, proj, vt)


def _conv_kernel(ca_ref, cg_ref, ca_h_ref, cg_h_ref, w_ref, b_ref, lg_ref, lb_ref, z_ref,
                 ext_sc, sh_sc, y_sc, *, tm):
    i = pl.program_id(0)
    n_ch = w_ref.shape[1]
    n_ext = CONV_HALO + tm
    glu = lambda a, g: a.astype(F32) * jax.nn.sigmoid(g.astype(F32))
    halo = glu(ca_h_ref[...], cg_h_ref[...])
    ext_sc[0:CONV_HALO, :] = jnp.where(i > 0, halo, jnp.zeros_like(halo))
    ext_sc[CONV_HALO:n_ext, :] = glu(ca_ref[...], cg_ref[...])
    ext_sc[n_ext:n_ext + SUBLANES, :] = jnp.zeros((SUBLANES, n_ch), F32)
    base = CONV_HALO - (CONV_K - 1)

    def slab(lc, carry):
        l0 = pl.multiple_of(lc * LANES, LANES)
        for rho in range(SUBLANES):
            sh_sc[rho] = ext_sc[rho:rho + n_ext, pl.ds(l0, LANES)]
        acc = jnp.zeros((tm, LANES), F32) + b_ref[:, pl.ds(l0, LANES)]
        for j in range(CONV_K):
            rho = (base + j) % SUBLANES
            a = base + j - rho
            acc = acc + w_ref[j:j + 1, pl.ds(l0, LANES)] * sh_sc[rho, a:a + tm, :]
        y_sc[:, pl.ds(l0, LANES)] = acc
        return carry

    lax.fori_loop(0, n_ch // LANES, slab, 0)
    acc = y_sc[...]
    mu = jnp.mean(acc, axis=-1, keepdims=True)
    d = acc - mu
    var = jnp.mean(d * d, axis=-1, keepdims=True)
    y = d * lax.rsqrt(var + EPS) * lg_ref[...] + lb_ref[...]
    z_ref[...] = (y * jax.nn.sigmoid(y)).astype(z_ref.dtype)


def _conv_branch(proj, conv_w, conv_b, ln_g, ln_b, *, ca_blk, cg_blk, tm):
    m = proj.shape[0]
    c = conv_w.shape[1]
    hb = tm // CONV_HALO
    row = lambda a: a.reshape(1, c).astype(F32)
    vec = pl.BlockSpec((1, c), lambda i: (0, 0))
    return pl.pallas_call(
        functools.partial(_conv_kernel, tm=tm),
        grid=(m // tm,),
        in_specs=[pl.BlockSpec((tm, c), lambda i: (i, ca_blk)),
                  pl.BlockSpec((tm, c), lambda i: (i, cg_blk)),
                  pl.BlockSpec((CONV_HALO, c), lambda i: (jnp.maximum(i * hb - 1, 0), ca_blk)),
                  pl.BlockSpec((CONV_HALO, c), lambda i: (jnp.maximum(i * hb - 1, 0), cg_blk)),
                  pl.BlockSpec((CONV_K, c), lambda i: (0, 0)),
                  vec, vec, vec],
        out_specs=pl.BlockSpec((tm, c), lambda i: (i, 0)),
        out_shape=jax.ShapeDtypeStruct((m, c), BF16),
        scratch_shapes=[pltpu.VMEM((CONV_HALO + tm + SUBLANES, c), F32),
                        pltpu.VMEM((SUBLANES, CONV_HALO + tm, LANES), F32),
                        pltpu.VMEM((tm, c), F32)],
        compiler_params=_cparams(1),
        name="conformer_conv",
    )(proj, proj, proj, proj, conv_w.astype(F32), row(conv_b), row(ln_g), row(ln_b))


def _merge_kernel(o_ref, z_ref, wa_ref, wc_ref, bc_ref, g1_ref, g2_ref, bg1_ref, bg2_ref, out_ref,
                  wab_sc, wcb_sc):
    _cast_weight_once(wa_ref, wab_sc)
    _cast_weight_once(wc_ref, wcb_sc)
    ya = jnp.dot(o_ref[...], wab_sc[...], preferred_element_type=F32)
    yc = jnp.dot(z_ref[...], wcb_sc[...], preferred_element_type=F32) + bc_ref[...]
    g1 = jax.nn.sigmoid(g1_ref[...].astype(F32) + bg1_ref[...])
    g2 = jax.nn.sigmoid(g2_ref[...].astype(F32) + bg2_ref[...])
    out_ref[...] = (g1 * ya + g2 * yc).astype(out_ref.dtype)


def _merge(o, z, proj, wa, wc, bc, bg, *, gate_col, tm, tn):
    m, ka = o.shape
    kc = z.shape[1]
    d = wa.shape[1]
    g1_blk = gate_col // tn
    g2_blk = (gate_col + d) // tn
    nb = d // tn
    bg2 = bg.reshape(1, 2 * d).astype(F32)
    return pl.pallas_call(
        _merge_kernel,
        grid=(nb, m // tm),
        in_specs=[pl.BlockSpec((tm, ka), lambda j, i: (i, 0)),
                  pl.BlockSpec((tm, kc), lambda j, i: (i, 0)),
                  pl.BlockSpec((ka, tn), lambda j, i: (0, j)),
                  pl.BlockSpec((kc, tn), lambda j, i: (0, j)),
                  pl.BlockSpec((1, tn), lambda j, i: (0, j)),
                  pl.BlockSpec((tm, tn), lambda j, i: (i, g1_blk + j)),
                  pl.BlockSpec((tm, tn), lambda j, i: (i, g2_blk + j)),
                  pl.BlockSpec((1, tn), lambda j, i: (0, j)),
                  pl.BlockSpec((1, tn), lambda j, i: (0, nb + j))],
        out_specs=pl.BlockSpec((tm, tn), lambda j, i: (i, j)),
        out_shape=jax.ShapeDtypeStruct((m, d), BF16),
        scratch_shapes=[pltpu.VMEM((ka, tn), BF16), pltpu.VMEM((kc, tn), BF16)],
        compiler_params=_cparams(2),
        name="mixer_merge",
    )(o, z, wa, wc, bc.reshape(1, d).astype(F32), proj, proj, bg2, bg2)


def _router_kernel(h_ref, g_ref, wr_ref, br_ref, valid_ref,
                   up_ref, e_ref, gate_ref, rank_ref, cnt_ref, carry_sc, *, tm):
    i = pl.program_id(0)

    @pl.when(i == 0)
    def _():
        carry_sc[...] = jnp.zeros(carry_sc.shape, F32)

    h = h_ref[...]
    ms = jnp.mean(h * h, axis=-1, keepdims=True)
    u = h * lax.rsqrt(ms + EPS) * g_ref[...]

    half = u.shape[1] // 2
    bits = lax.bitcast_convert_type(u.astype(BF16).astype(F32), jnp.uint32)
    up_ref[...] = (bits[:, half:] & jnp.uint32(0xFFFF0000)) | (bits[:, :half] >> 16)

    logits = lax.dot_general(wr_ref[...], u, (((1,), (1,)), ((), ())),
                             precision=lax.Precision.HIGHEST,
                             preferred_element_type=F32) + br_ref[...]
    n_e = logits.shape[0]
    eiota = lax.broadcasted_iota(jnp.int32, logits.shape, 0).astype(F32)
    work = logits
    sel = jnp.zeros(logits.shape, jnp.bool_)
    top_l, top_e = [], []
    for _ in range(TOP_K):
        mx = jnp.max(work, axis=0, keepdims=True)
        idx = jnp.min(jnp.where(work == mx, eiota, float(n_e)), axis=0, keepdims=True)
        hit = eiota == idx
        top_l.append(mx)
        top_e.append(idx)
        sel = sel | hit
        work = jnp.where(hit, -jnp.inf, work)
    ex = [jnp.exp(t - top_l[0]) for t in top_l]
    den = ex[0] + ex[1] + ex[2] + ex[3]
    gate_ref[...] = jnp.concatenate([e / den for e in ex], axis=0)
    e_ref[...] = jnp.concatenate(top_e, axis=0).astype(jnp.int32)

    selv = jnp.where(sel & (valid_ref[...] > 0.0), 1.0, 0.0)
    before = (lax.broadcasted_iota(jnp.int32, (tm, tm), 0)
              < lax.broadcasted_iota(jnp.int32, (tm, tm), 1)).astype(BF16)
    rank_all = jnp.dot(selv.astype(BF16), before, preferred_element_type=F32) + carry_sc[...]
    ranks = [jnp.sum(jnp.where(eiota == idx, rank_all, 0.0), axis=0, keepdims=True) for idx in top_e]
    rank_ref[...] = jnp.concatenate(ranks, axis=0).astype(jnp.int32)
    carry = carry_sc[...] + jnp.sum(selv, axis=1, keepdims=True)
    carry_sc[...] = carry
    cnt_ref[...] = jnp.broadcast_to(carry, cnt_ref.shape).astype(jnp.int32)


def _router(h2, g, w_router, b_router, valid, *, tm):
    m, d = h2.shape
    n_e = w_router.shape[1]
    tok = lambda dt: jax.ShapeDtypeStruct((TOP_K, m), dt)
    tok_spec = pl.BlockSpec((TOP_K, tm), lambda i: (0, i))
    return pl.pallas_call(
        functools.partial(_router_kernel, tm=tm),
        grid=(m // tm,),
        in_specs=[pl.BlockSpec((tm, d), lambda i: (i, 0)),
                  pl.BlockSpec((1, d), lambda i: (0, 0)),
                  pl.BlockSpec((n_e, d), lambda i: (0, 0)),
                  pl.BlockSpec((n_e, 1), lambda i: (0, 0)),
                  pl.BlockSpec((1, tm), lambda i: (0, i))],
        out_specs=[pl.BlockSpec((tm, d // 2), lambda i: (i, 0)),
                   tok_spec, tok_spec, tok_spec,
                   pl.BlockSpec((n_e, 128), lambda i: (0, 0))],
        out_shape=[jax.ShapeDtypeStruct((m, d // 2), jnp.uint32),
                   tok(jnp.int32), tok(F32), tok(jnp.int32),
                   jax.ShapeDtypeStruct((n_e, 128), jnp.int32)],
        scratch_shapes=[pltpu.VMEM((n_e, 1), F32)],
        compiler_params=_cparams(1),
        name="router",
    )(h2, g.reshape(1, d).astype(F32), w_router.T.astype(F32), b_router.reshape(n_e, 1).astype(F32), valid)


def _dispatch_kernel(dest_ref, te_ref, nu_ref, u_ref, xs_ref, zero_sc, sem, zsem, *, tm, n_tiles):
    i = pl.program_id(0)

    @pl.when(i == 0)
    def _():
        zero_sc[...] = jnp.zeros(zero_sc.shape, zero_sc.dtype)
        nu = nu_ref[0]

        def partly_filled(t):
            nxt = te_ref[jnp.minimum(t + 1, n_tiles - 1)]
            return (t >= nu - 1) | (te_ref[t] != nxt)

        def tile_copy(t):
            return pltpu.make_async_copy(zero_sc, xs_ref.at[pl.ds(t * MOE_TILE, MOE_TILE), :], zsem)

        def start(t, c):
            @pl.when(partly_filled(t))
            def _():
                tile_copy(t).start()
            return c

        def wait(t, c):
            @pl.when(partly_filled(t))
            def _():
                tile_copy(t).wait()
            return c

        lax.fori_loop(0, n_tiles, start, 0)
        lax.fori_loop(0, n_tiles, wait, 0)

    def start_row(r, c):
        for k in range(TOP_K):
            d = dest_ref[(i * tm + r) * TOP_K + k]
            pltpu.make_async_copy(u_ref.at[pl.ds(r, 1), :], xs_ref.at[pl.ds(d, 1), :], sem).start()
        return c

    lax.fori_loop(0, tm, start_row, 0, unroll=8)
    for k in range(TOP_K):
        pltpu.make_async_copy(u_ref, xs_ref.at[pl.ds(0, tm), :], sem).wait()


def _dispatch(dest_flat, tile_e, n_used, u_packed, n_rows, *, tm, n_tiles):
    m, w = u_packed.shape
    grid_spec = pltpu.PrefetchScalarGridSpec(
        num_scalar_prefetch=3,
        grid=(m // tm,),
        in_specs=[pl.BlockSpec((tm, w), lambda i, dest, te, nu: (i, 0))],
        out_specs=pl.BlockSpec(memory_space=pl.ANY),
        scratch_shapes=[pltpu.VMEM((MOE_TILE, w), jnp.uint32),
                        pltpu.SemaphoreType.DMA(()), pltpu.SemaphoreType.DMA(())],
    )
    return pl.pallas_call(
        functools.partial(_dispatch_kernel, tm=tm, n_tiles=n_tiles),
        grid_spec=grid_spec,
        out_shape=jax.ShapeDtypeStruct((n_rows, w), jnp.uint32),
        compiler_params=_cparams(1),
        name="moe_dispatch",
    )(dest_flat, tile_e, n_used, u_packed)


def _expert_rows_loop(t0, n, in_copy, out_copy, compute):
    @pl.when(n > 0)
    def _():
        in_copy(t0, 0).start()

    def body(i, c):
        slot = i % 2

        @pl.when(i + 1 < n)
        def _():
            in_copy(t0 + i + 1, 1 - slot).start()

        in_copy(t0 + i, slot).wait()

        @pl.when(i >= 2)
        def _():
            out_copy(t0 + i - 2, slot).wait()

        compute(slot)
        out_copy(t0 + i, slot).start()
        return c

    lax.fori_loop(0, n, body, 0)

    @pl.when(n >= 2)
    def _():
        out_copy(t0 + n - 2, n % 2).wait()

    @pl.when(n >= 1)
    def _():
        out_copy(t0 + n - 1, (n - 1) % 2).wait()


def _zero_unused_tiles(ts_ref, tc_ref, obuf, out_copy, n_tiles):
    last = pl.num_programs(1) - 1

    @pl.when(pl.program_id(1) == last)
    def _():
        obuf[0] = jnp.zeros(obuf.shape[1:], obuf.dtype)

        def body(t, c):
            cp = out_copy(t, 0)
            cp.start()
            cp.wait()
            return c

        lax.fori_loop(ts_ref[last] + tc_ref[last], n_tiles, body, 0)


def _gate_up_kernel(ts_ref, tc_ref, xs_ref, wg_ref, wu_ref, bg_ref, bu_ref, act_ref,
                    wgb_sc, wub_sc, xbuf, obuf, xsem, osem, *, tf, n_tiles):
    j = pl.program_id(0)
    e = pl.program_id(1)
    n = tc_ref[e]

    def in_copy(t, slot):
        return pltpu.make_async_copy(xs_ref.at[pl.ds(t * MOE_TILE, MOE_TILE), :], xbuf.at[slot], xsem.at[slot])

    def out_copy(t, slot):
        return pltpu.make_async_copy(obuf.at[slot],
                                     act_ref.at[pl.ds(t * MOE_TILE, MOE_TILE), pl.ds(j * tf, tf)], osem.at[slot])

    @pl.when(n > 0)
    def _():
        wgb_sc[...] = wg_ref[...].astype(BF16)
        wub_sc[...] = wu_ref[...].astype(BF16)

    def compute(slot):
        w = xbuf[slot]
        half = w.shape[1]
        lo = lax.bitcast_convert_type(w << 16, F32).astype(BF16)
        hi = lax.bitcast_convert_type(w & jnp.uint32(0xFFFF0000), F32).astype(BF16)
        g = (jnp.dot(lo, wgb_sc[:half, :], preferred_element_type=F32)
             + jnp.dot(hi, wgb_sc[half:, :], preferred_element_type=F32) + bg_ref[...])
        u = (jnp.dot(lo, wub_sc[:half, :], preferred_element_type=F32)
             + jnp.dot(hi, wub_sc[half:, :], preferred_element_type=F32) + bu_ref[...])
        g = jnp.minimum(g, SWIGLU_LIMIT)
        u = jnp.clip(u, -SWIGLU_LIMIT, SWIGLU_LIMIT)
        obuf[slot] = ((u + 1.0) * (g * jax.nn.sigmoid(SWIGLU_ALPHA * g))).astype(obuf.dtype)

    _expert_rows_loop(ts_ref[e], n, in_copy, out_copy, compute)
    _zero_unused_tiles(ts_ref, tc_ref, obuf, out_copy, n_tiles)


def _gate_up(tile_start, tile_count, xs, w_gu, b_gu, *, n_tiles, tf):
    n_e, d, f2 = w_gu.shape
    f = f2 // 2
    nj = f // tf
    grid_spec = pltpu.PrefetchScalarGridSpec(
        num_scalar_prefetch=2,
        grid=(nj, n_e),
        in_specs=[pl.BlockSpec(memory_space=pl.ANY),
                  pl.BlockSpec((None, d, tf), lambda j, e, ts, tc: (e, 0, j)),
                  pl.BlockSpec((None, d, tf), lambda j, e, ts, tc: (e, 0, nj + j)),
                  pl.BlockSpec((None, 1, tf), lambda j, e, ts, tc: (e, 0, j)),
                  pl.BlockSpec((None, 1, tf), lambda j, e, ts, tc: (e, 0, nj + j))],
        out_specs=pl.BlockSpec(memory_space=pl.ANY),
        scratch_shapes=[pltpu.VMEM((d, tf), BF16), pltpu.VMEM((d, tf), BF16),
                        pltpu.VMEM((2, MOE_TILE, d // 2), jnp.uint32), pltpu.VMEM((2, MOE_TILE, tf), BF16),
                        pltpu.SemaphoreType.DMA((2,)), pltpu.SemaphoreType.DMA((2,))],
    )
    return pl.pallas_call(
        functools.partial(_gate_up_kernel, tf=tf, n_tiles=n_tiles),
        grid_spec=grid_spec,
        out_shape=jax.ShapeDtypeStruct((n_tiles * MOE_TILE, f), BF16),
        compiler_params=_cparams(2),
        name="moe_gate_up",
    )(tile_start, tile_count, xs, w_gu, w_gu, b_gu, b_gu)


def _down_kernel(ts_ref, tc_ref, act_ref, wd_ref, bd_ref, y_ref, wdb_sc, abuf, obuf, asem, osem, *, tn, n_tiles):
    j = pl.program_id(0)
    e = pl.program_id(1)
    n = tc_ref[e]

    def in_copy(t, slot):
        return pltpu.make_async_copy(act_ref.at[pl.ds(t * MOE_TILE, MOE_TILE), :], abuf.at[slot], asem.at[slot])

    def out_copy(t, slot):
        return pltpu.make_async_copy(obuf.at[slot],
                                     y_ref.at[pl.ds(t * MOE_TILE, MOE_TILE), pl.ds(j * tn, tn)], osem.at[slot])

    @pl.when(n > 0)
    def _():
        wdb_sc[...] = wd_ref[...].astype(BF16)

    def compute(slot):
        obuf[slot] = jnp.dot(abuf[slot], wdb_sc[...], preferred_element_type=F32) + bd_ref[...]

    _expert_rows_loop(ts_ref[e], n, in_copy, out_copy, compute)
    _zero_unused_tiles(ts_ref, tc_ref, obuf, out_copy, n_tiles)


def _down(tile_start, tile_count, act, w_d, b_d, *, n_tiles, tn):
    n_e, f, d = w_d.shape
    grid_spec = pltpu.PrefetchScalarGridSpec(
        num_scalar_prefetch=2,
        grid=(d // tn, n_e),
        in_specs=[pl.BlockSpec(memory_space=pl.ANY),
                  pl.BlockSpec((None, f, tn), lambda j, e, ts, tc: (e, 0, j)),
                  pl.BlockSpec((None, 1, tn), lambda j, e, ts, tc: (e, 0, j))],
        out_specs=pl.BlockSpec(memory_space=pl.ANY),
        scratch_shapes=[pltpu.VMEM((f, tn), BF16),
                        pltpu.VMEM((2, MOE_TILE, f), BF16), pltpu.VMEM((2, MOE_TILE, tn), F32),
                        pltpu.SemaphoreType.DMA((2,)), pltpu.SemaphoreType.DMA((2,))],
    )
    return pl.pallas_call(
        functools.partial(_down_kernel, tn=tn, n_tiles=n_tiles),
        grid_spec=grid_spec,
        out_shape=jax.ShapeDtypeStruct((n_tiles * MOE_TILE, d), F32),
        compiler_params=_cparams(2),
        name="moe_down",
    )(tile_start, tile_count, act, w_d, b_d)


def _combine_kernel(dest_ref, h_ref, gate_ref, g_ref, ys_ref, out_ref, buf, sem, *, tm, lp, row0, nst):
    s = pl.program_id(0)
    n_steps = pl.num_programs(0)

    def start_gather(step, slot):
        t0 = (step // nst) * lp + row0 + (step % nst) * tm

        def start_row(r, c):
            for k in range(TOP_K):
                d = dest_ref[(t0 + r) * TOP_K + k]
                pltpu.make_async_copy(ys_ref.at[pl.ds(d, 1), :], buf.at[slot, k, pl.ds(r, 1), :],
                                      sem.at[slot]).start()
            return c

        lax.fori_loop(0, tm, start_row, 0, unroll=8)

    @pl.when(s == 0)
    def _():
        start_gather(0, 0)

    @pl.when(s + 1 < n_steps)
    def _():
        start_gather(s + 1, (s + 1) % 2)

    slot = s % 2
    for k in range(TOP_K):
        pltpu.make_async_copy(ys_ref.at[pl.ds(0, tm), :], buf.at[slot, k], sem.at[slot]).wait()

    gate = gate_ref[...]
    acc = jnp.zeros(h_ref.shape, F32)
    for k in range(TOP_K):
        acc = acc + gate[:, k:k + 1] * buf[slot, k]
    h = h_ref[...] + acc
    ms = jnp.mean(h * h, axis=-1, keepdims=True)
    out_ref[...] = (h * lax.rsqrt(ms + EPS) * g_ref[...]).astype(out_ref.dtype)


def _combine(dest_flat, h2, gate, g, ys, *, batch, seq, lp, row0, tm):
    d = h2.shape[1]
    nb_b = lp // tm
    nb0 = row0 // tm
    nst = seq // tm
    blk = lambda s: (s // nst) * nb_b + nb0 + s % nst
    grid_spec = pltpu.PrefetchScalarGridSpec(
        num_scalar_prefetch=1,
        grid=(batch * nst,),
        in_specs=[pl.BlockSpec((tm, d), lambda s, dest: (blk(s), 0)),
                  pl.BlockSpec((tm, TOP_K), lambda s, dest: (blk(s), 0)),
                  pl.BlockSpec((1, d), lambda s, dest: (0, 0)),
                  pl.BlockSpec(memory_space=pl.ANY)],
        out_specs=pl.BlockSpec((None, tm, d), lambda s, dest: (s // nst, s % nst, 0)),
        scratch_shapes=[pltpu.VMEM((2, TOP_K, tm, d), F32), pltpu.SemaphoreType.DMA((2,))],
    )
    return pl.pallas_call(
        functools.partial(_combine_kernel, tm=tm, lp=lp, row0=row0, nst=nst),
        grid_spec=grid_spec,
        out_shape=jax.ShapeDtypeStruct((batch, seq, d), F32),
        compiler_params=_cparams(1),
        name="moe_combine",
    )(dest_flat, h2, gate, g.reshape(1, d).astype(F32), ys)


def _pick(pref, n):
    t = pref
    while n % t:
        t //= 2
    return t


def _row_tile(n, pref):
    t = pref // ROW_ALIGN * ROW_ALIGN
    while n % t:
        t -= ROW_ALIGN
    return t


def kernel(x, meta_tokens, norm_mix_g, w_in, b_gate, lambda_q1, lambda_k1, lambda_q2, lambda_k2, head_norm_g, w_attn_out, conv_w, conv_b, conv_ln_g, conv_ln_b, w_conv_out, b_conv_out, w_out, norm_ffn_g, w_router, b_router, w_gate_up, b_gate_up, w_down, b_down, final_norm_g):
    batch, seq, d = x.shape
    depth = w_in.shape[0]
    assert depth == 1 and seq % ROW_ALIGN == 0 and N_META <= ROW_ALIGN
    n_heads = d // 256
    hw = 2 * HEAD_DIM
    qk_w = n_heads * hw
    conv_ch = conv_w.shape[2]
    n_pad = ROW_ALIGN - N_META
    lp = n_pad + N_META + seq
    tp = batch * lp
    f = w_down.shape[2]
    layer = 0
    lam_init = 0.8 - 0.6 * math.exp(-0.3 * layer)

    h0, u = _embed_norm(x, meta_tokens, norm_mix_g[layer], n_pad=n_pad)

    proj = _matmul(u, w_in[layer], BF16, _row_tile(tp, 1536), _pick(512, w_in.shape[2]), name="in_proj")

    tq = 384 if lp % 384 == 0 else ROW_ALIGN
    vt = proj[:, 2 * qk_w:3 * qk_w].reshape(batch, lp, n_heads, hw).transpose(0, 2, 3, 1)
    o = _attention(proj, vt, lambda_q1[layer], lambda_k1[layer], lambda_q2[layer], lambda_k2[layer],
                   head_norm_g[layer], batch=batch, lp=lp, n_heads=n_heads, tq=tq, lam_init=lam_init,
                   n_pad=n_pad)
    ca_col = 3 * qk_w
    z = _conv_branch(proj, conv_w[layer], conv_b[layer], conv_ln_g[layer], conv_ln_b[layer],
                     ca_blk=ca_col // conv_ch, cg_blk=ca_col // conv_ch + 1, tm=_pick(256, tp))
    merged = _merge(o, z, proj, w_attn_out[layer], w_conv_out[layer],
                    b_conv_out[layer], b_gate[layer], gate_col=ca_col + 2 * conv_ch,
                    tm=_row_tile(tp, 768), tn=_pick(512, d))
    h2 = _matmul(merged, w_out[layer], F32, _row_tile(tp, 768), _pick(512, d), res=h0, name="out_proj")

    pos = np.arange(tp) % lp
    valid_np = pos >= n_pad
    valid = jnp.asarray(valid_np.astype(np.float32).reshape(1, tp))
    u_packed, top_e, gate_t, rank_t, cnt = _router(h2, norm_ffn_g[layer], w_router[layer], b_router[layer],
                                                   valid, tm=_pick(256, tp))
    counts = cnt[:, 0]
    padded = (counts + MOE_TILE - 1) // MOE_TILE * MOE_TILE
    e_ids = np.arange(N_EXPERTS)
    pad_end = jnp.sum(jnp.where(jnp.asarray(e_ids[None, :] <= e_ids[:, None]), padded[None, :], 0), axis=1)
    pad_start = pad_end - padded
    start_tok = jnp.sum(jnp.where(top_e[:, :, None] == jnp.asarray(e_ids, jnp.int32), pad_start, 0), axis=-1)
    n_real = int(valid_np.sum()) * TOP_K
    n_tiles = -(-(n_real + N_EXPERTS * (MOE_TILE - 1)) // MOE_TILE)
    n_slots = n_tiles * MOE_TILE
    dump = n_slots + (np.cumsum(~valid_np) - 1)[None, :] * TOP_K + np.arange(TOP_K)[:, None]
    dest_t = jnp.where(jnp.asarray(valid_np)[None, :], start_tok + rank_t, jnp.asarray(dump, jnp.int32))
    dest_flat = dest_t.T.reshape(-1).astype(jnp.int32)
    n_dump = int((~valid_np).sum()) * TOP_K
    n_used = (pad_end[-1] // MOE_TILE).astype(jnp.int32).reshape(1)
    tile_start = jnp.asarray(np.arange(n_tiles, dtype=np.int32) * MOE_TILE)
    tile_e = jnp.minimum(jnp.sum((pad_end[None, :] <= tile_start[:, None]).astype(jnp.int32), axis=1),
                         N_EXPERTS - 1).astype(jnp.int32)

    xs = _dispatch(dest_flat, tile_e, n_used, u_packed, n_slots + n_dump, tm=ROW_ALIGN, n_tiles=n_tiles)
    tile_first = (pad_start // MOE_TILE).astype(jnp.int32)
    tile_count = (padded // MOE_TILE).astype(jnp.int32)
    act = _gate_up(tile_first, tile_count, xs, w_gate_up[layer],
                   b_gate_up[layer].reshape(N_EXPERTS, 1, 2 * f).astype(F32), n_tiles=n_tiles, tf=_pick(512, f))
    ys = _down(tile_first, tile_count, act, w_down[layer],
               b_down[layer].reshape(N_EXPERTS, 1, d).astype(F32), n_tiles=n_tiles, tn=_pick(2048, d))
    return _combine(dest_flat, h2, gate_t.T, final_norm_g, ys, batch=batch, seq=seq, lp=lp,
                    row0=ROW_ALIGN, tm=ROW_ALIGN)
```

```python
import functools
import math

import numpy as np
import jax
import jax.numpy as jnp
from jax import lax
from jax.experimental import pallas as pl
from jax.experimental.pallas import tpu as pltpu

N_META = 16
HEAD_DIM = 64
N_EXPERTS = 32
TOP_K = 4
CONV_K = 31
EPS = 1e-5
SWIGLU_LIMIT = 7.0
SWIGLU_ALPHA = 1.702
SUBLANES = 8
LANES = 128
ROW_ALIGN = 128
CONV_HALO = 32
MOE_TILE = 256
ATTN_HEADS_PER_STEP = 4
VMEM_LIMIT = 56 * 1024 * 1024

F32 = jnp.float32
BF16 = jnp.bfloat16


def _cparams(n_axes, flags=None):
    return pltpu.CompilerParams(dimension_semantics=("arbitrary",) * n_axes,
                                vmem_limit_bytes=VMEM_LIMIT, flags=flags)


def _embed_norm_kernel(x_ref, meta_ref, g_ref, h_ref, u_ref, *, n_pad):
    i = pl.program_id(1)

    @pl.when(i == 0)
    def _():
        h_ref[0:n_pad, :] = jnp.zeros((n_pad, h_ref.shape[1]), F32)
        h_ref[n_pad:, :] = meta_ref[...]

    @pl.when(i > 0)
    def _():
        h_ref[...] = x_ref[...]

    h = h_ref[...]
    ms = jnp.mean(h * h, axis=-1, keepdims=True)
    u_ref[...] = (h * lax.rsqrt(ms + EPS) * g_ref[...]).astype(u_ref.dtype)


def _embed_norm(x, meta, g, *, n_pad):
    batch, seq, d = x.shape
    tm = ROW_ALIGN
    nb = (n_pad + N_META + seq) // tm
    out_spec = pl.BlockSpec((tm, d), lambda b, i: (b * nb + i, 0))
    return pl.pallas_call(
        functools.partial(_embed_norm_kernel, n_pad=n_pad),
        grid=(batch, nb),
        in_specs=[pl.BlockSpec((None, tm, d), lambda b, i: (b, jnp.maximum(i - 1, 0), 0)),
                  pl.BlockSpec((N_META, d), lambda b, i: (0, 0)),
                  pl.BlockSpec((1, d), lambda b, i: (0, 0))],
        out_specs=[out_spec, out_spec],
        out_shape=[jax.ShapeDtypeStruct((batch * nb * tm, d), F32),
                   jax.ShapeDtypeStruct((batch * nb * tm, d), BF16)],
        compiler_params=_cparams(2),
        name="embed_norm",
    )(x, meta.astype(F32), g.reshape(1, d).astype(F32))


def _cast_weight_once(w_ref, wb_sc):
    @pl.when(pl.program_id(1) == 0)
    def _():
        wb_sc[...] = w_ref[...].astype(BF16)


def _matmul_kernel(a_ref, w_ref, o_ref, wb_sc):
    _cast_weight_once(w_ref, wb_sc)
    o_ref[...] = jnp.dot(a_ref[...], wb_sc[...], preferred_element_type=F32).astype(o_ref.dtype)


def _matmul_res_kernel(a_ref, w_ref, r_ref, o_ref, wb_sc):
    _cast_weight_once(w_ref, wb_sc)
    acc = jnp.dot(a_ref[...], wb_sc[...], preferred_element_type=F32)
    o_ref[...] = (acc + r_ref[...]).astype(o_ref.dtype)


def _matmul(a, w, out_dtype, tm, tn, res=None, name="matmul"):
    m, k = a.shape
    n = w.shape[1]
    in_specs = [pl.BlockSpec((tm, k), lambda j, i: (i, 0)),
                pl.BlockSpec((k, tn), lambda j, i: (0, j))]
    args = [a, w]
    kern = _matmul_kernel
    if res is not None:
        in_specs.append(pl.BlockSpec((tm, tn), lambda j, i: (i, j)))
        args.append(res)
        kern = _matmul_res_kernel
    return pl.pallas_call(
        kern,
        grid=(n // tn, m // tm),
        in_specs=in_specs,
        out_specs=pl.BlockSpec((tm, tn), lambda j, i: (i, j)),
        out_shape=jax.ShapeDtypeStruct((m, n), out_dtype),
        scratch_shapes=[pltpu.VMEM((k, tn), BF16)],
        compiler_params=_cparams(2),
        name=name,
    )(*args)


def _attn_kernel(lq1_ref, lk1_ref, lq2_ref, lk2_ref, hg_ref, bias_ref, q_ref, k_ref, vt_ref, o_ref,
                 q12_sc, s_sc, m_sc, l_sc, acc_sc, *, tq, lam_init, n_hd):
    hw = 2 * HEAD_DIM
    nq = q_ref.shape[0] // tq
    lam = (jnp.exp(jnp.sum(lq1_ref[...] * lk1_ref[...], axis=-1, keepdims=True))
           - jnp.exp(jnp.sum(lq2_ref[...] * lk2_ref[...], axis=-1, keepdims=True)) + lam_init)

    def q_tile(qi, carry):
        q0 = pl.multiple_of(qi * tq, tq)
        for hd in range(n_hd):
            q = (q_ref[pl.ds(q0, tq), hd * hw:(hd + 1) * hw].astype(F32)
                 * (HEAD_DIM ** -0.5 * math.log2(math.e))).astype(BF16)
            lane = lax.broadcasted_iota(jnp.int32, q.shape, 1)
            zero = jnp.zeros_like(q)
            q12_sc[2 * hd] = jnp.where(lane < HEAD_DIM, q, zero)
            q12_sc[2 * hd + 1] = jnp.where(lane >= HEAD_DIM, q, zero)
        m_sc[...] = jnp.full(m_sc.shape, -jnp.inf, F32)
        l_sc[...] = jnp.zeros(l_sc.shape, F32)
        acc_sc[...] = jnp.zeros(acc_sc.shape, F32)

        def scores(kj, c):
            hd = c // 2
            k0 = pl.multiple_of(kj * tq, tq)
            s = lax.dot_general(k_ref[pl.ds(k0, tq), hd * hw:(hd + 1) * hw], q12_sc[c],
                                (((1,), (1,)), ((), ())), preferred_element_type=F32)
            kind = jnp.where(kj == 0, 1, 0) + jnp.where(kj == qi, 2, 0)
            s_sc[c] = s + bias_ref[kind]

        def update(kj, c):
            hd = c // 2
            k0 = pl.multiple_of(kj * tq, tq)
            s = s_sc[c]
            m_prev = m_sc[c]
            m_new = jnp.maximum(m_prev, jnp.max(s, axis=0, keepdims=True))
            alpha = jnp.exp2(m_prev - m_new)
            p = jnp.exp2(s - m_new)
            l_sc[c] = alpha * l_sc[c] + jnp.sum(p.reshape(tq // 8, 8, tq), axis=0)
            acc_sc[c] = alpha * acc_sc[c] + jnp.dot(vt_ref[hd, :, pl.ds(k0, tq)], p.astype(BF16),
                                                    preferred_element_type=F32)
            m_sc[c] = m_new

        for hd in range(n_hd):
            scores(0, 2 * hd)

        def body(kj, c):
            nxt = jnp.minimum(kj + 1, qi)
            for hd in range(n_hd):
                scores(kj, 2 * hd + 1)
                update(kj, 2 * hd)
                scores(nxt, 2 * hd)
            for hd in range(n_hd):
                update(kj, 2 * hd + 1)
            return c

        lax.fori_loop(0, qi + 1, body, 0)

        for hd in range(n_hd):
            o1 = acc_sc[2 * hd] / jnp.sum(l_sc[2 * hd], axis=0, keepdims=True)
            o2 = acc_sc[2 * hd + 1] / jnp.sum(l_sc[2 * hd + 1], axis=0, keepdims=True)
            o = o1 - lam * o2
            ms = jnp.mean(o * o, axis=0, keepdims=True)
            o = o * lax.rsqrt(ms + EPS) * hg_ref[...] * (1.0 - lam_init)
            o_ref[pl.ds(q0, tq), hd * hw:(hd + 1) * hw] = o.T.astype(o_ref.dtype)
        return carry

    lax.fori_loop(0, nq, q_tile, 0)


def _attn_bias(tq, n_pad):
    neg = np.float32(np.finfo(np.float32).min)
    r = np.arange(tq)[:, None]
    c = np.arange(tq)[None, :]
    pad = np.broadcast_to(r < n_pad, (tq, tq))
    future = r > c
    tiles = [np.zeros((tq, tq), bool), pad, future, pad | future]
    return jnp.asarray(np.stack([np.where(t, neg, np.float32(0)) for t in tiles]).astype(np.float32))


def _attention(proj, vt, lq1, lk1, lq2, lk2, head_g, *, batch, lp, n_heads, tq, lam_init, n_pad):
    hw = 2 * HEAD_DIM
    n_hd = ATTN_HEADS_PER_STEP if n_heads % ATTN_HEADS_PER_STEP == 0 else 1
    assert n_pad <= tq
    koff = n_heads // n_hd
    vec = lambda a: a.reshape(1, -1).astype(F32)
    small = lambda n: pl.BlockSpec((1, n), lambda b, h: (0, 0))
    return pl.pallas_call(
        functools.partial(_attn_kernel, tq=tq, lam_init=lam_init, n_hd=n_hd),
        grid=(batch, n_heads // n_hd),
        in_specs=[small(HEAD_DIM), small(HEAD_DIM), small(HEAD_DIM), small(HEAD_DIM),
                  pl.BlockSpec((hw, 1), lambda b, h: (0, 0)),
                  pl.BlockSpec((4, tq, tq), lambda b, h: (0, 0, 0)),
                  pl.BlockSpec((lp, n_hd * hw), lambda b, h: (b, h)),
                  pl.BlockSpec((lp, n_hd * hw), lambda b, h: (b, koff + h)),
                  pl.BlockSpec((None, n_hd, hw, lp), lambda b, h: (b, h, 0, 0))],
        out_specs=pl.BlockSpec((lp, n_hd * hw), lambda b, h: (b, h)),
        out_shape=jax.ShapeDtypeStruct((batch * lp, n_heads * hw), BF16),
        scratch_shapes=[pltpu.VMEM((2 * n_hd, tq, hw), BF16),
                        pltpu.VMEM((2 * n_hd, tq, tq), F32),
                        pltpu.VMEM((2 * n_hd, 1, tq), F32), pltpu.VMEM((2 * n_hd, 8, tq), F32),
                        pltpu.VMEM((2 * n_hd, hw, tq), F32)],
        compiler_params=_cparams(2),
        name="diff_attention",
    )(vec(lq1), vec(lk1), vec(lq2), vec(lk2), head_g.reshape(hw, 1).astype(F32), _attn_bias(tq, n_pad),
      proj, proj, vt)


def _conv_kernel(ca_ref, cg_ref, ca_h_ref, cg_h_ref, w_ref, b_ref, lg_ref, lb_ref, z_ref,
                 ext_sc, sh_sc, y_sc, *, tm):
    i = pl.program_id(0)
    n_ch = w_ref.shape[1]
    n_ext = CONV_HALO + tm
    glu = lambda a, g: a.astype(F32) * jax.nn.sigmoid(g.astype(F32))
    halo = glu(ca_h_ref[...], cg_h_ref[...])
    ext_sc[0:CONV_HALO, :] = jnp.where(i > 0, halo, jnp.zeros_like(halo))
    ext_sc[CONV_HALO:n_ext, :] = glu(ca_ref[...], cg_ref[...])
    ext_sc[n_ext:n_ext + SUBLANES, :] = jnp.zeros((SUBLANES, n_ch), F32)
    base = CONV_HALO - (CONV_K - 1)

    def slab(lc, carry):
        l0 = pl.multiple_of(lc * LANES, LANES)
        for rho in range(SUBLANES):
            sh_sc[rho] = ext_sc[rho:rho + n_ext, pl.ds(l0, LANES)]
        acc = jnp.zeros((tm, LANES), F32) + b_ref[:, pl.ds(l0, LANES)]
        for j in range(CONV_K):
            rho = (base + j) % SUBLANES
            a = base + j - rho
            acc = acc + w_ref[j:j + 1, pl.ds(l0, LANES)] * sh_sc[rho, a:a + tm, :]
        y_sc[:, pl.ds(l0, LANES)] = acc
        return carry

    lax.fori_loop(0, n_ch // LANES, slab, 0)
    acc = y_sc[...]
    mu = jnp.mean(acc, axis=-1, keepdims=True)
    d = acc - mu
    var = jnp.mean(d * d, axis=-1, keepdims=True)
    y = d * lax.rsqrt(var + EPS) * lg_ref[...] + lb_ref[...]
    z_ref[...] = (y * jax.nn.sigmoid(y)).astype(z_ref.dtype)


def _conv_branch(proj, conv_w, conv_b, ln_g, ln_b, *, ca_blk, cg_blk, tm):
    m = proj.shape[0]
    c = conv_w.shape[1]
    hb = tm // CONV_HALO
    row = lambda a: a.reshape(1, c).astype(F32)
    vec = pl.BlockSpec((1, c), lambda i: (0, 0))
    return pl.pallas_call(
        functools.partial(_conv_kernel, tm=tm),
        grid=(m // tm,),
        in_specs=[pl.BlockSpec((tm, c), lambda i: (i, ca_blk)),
                  pl.BlockSpec((tm, c), lambda i: (i, cg_blk)),
                  pl.BlockSpec((CONV_HALO, c), lambda i: (jnp.maximum(i * hb - 1, 0), ca_blk)),
                  pl.BlockSpec((CONV_HALO, c), lambda i: (jnp.maximum(i * hb - 1, 0), cg_blk)),
                  pl.BlockSpec((CONV_K, c), lambda i: (0, 0)),
                  vec, vec, vec],
        out_specs=pl.BlockSpec((tm, c), lambda i: (i, 0)),
        out_shape=jax.ShapeDtypeStruct((m, c), BF16),
        scratch_shapes=[pltpu.VMEM((CONV_HALO + tm + SUBLANES, c), F32),
                        pltpu.VMEM((SUBLANES, CONV_HALO + tm, LANES), F32),
                        pltpu.VMEM((tm, c), F32)],
        compiler_params=_cparams(1),
        name="conformer_conv",
    )(proj, proj, proj, proj, conv_w.astype(F32), row(conv_b), row(ln_g), row(ln_b))


def _merge_kernel(o_ref, z_ref, wa_ref, wc_ref, bc_ref, g1_ref, g2_ref, bg1_ref, bg2_ref, out_ref,
                  wab_sc, wcb_sc):
    _cast_weight_once(wa_ref, wab_sc)
    _cast_weight_once(wc_ref, wcb_sc)
    ya = jnp.dot(o_ref[...], wab_sc[...], preferred_element_type=F32)
    yc = jnp.dot(z_ref[...], wcb_sc[...], preferred_element_type=F32) + bc_ref[...]
    g1 = jax.nn.sigmoid(g1_ref[...].astype(F32) + bg1_ref[...])
    g2 = jax.nn.sigmoid(g2_ref[...].astype(F32) + bg2_ref[...])
    out_ref[...] = (g1 * ya + g2 * yc).astype(out_ref.dtype)


def _merge(o, z, proj, wa, wc, bc, bg, *, gate_col, tm, tn):
    m, ka = o.shape
    kc = z.shape[1]
    d = wa.shape[1]
    g1_blk = gate_col // tn
    g2_blk = (gate_col + d) // tn
    nb = d // tn
    bg2 = bg.reshape(1, 2 * d).astype(F32)
    return pl.pallas_call(
        _merge_kernel,
        grid=(nb, m // tm),
        in_specs=[pl.BlockSpec((tm, ka), lambda j, i: (i, 0)),
                  pl.BlockSpec((tm, kc), lambda j, i: (i, 0)),
                  pl.BlockSpec((ka, tn), lambda j, i: (0, j)),
                  pl.BlockSpec((kc, tn), lambda j, i: (0, j)),
                  pl.BlockSpec((1, tn), lambda j, i: (0, j)),
                  pl.BlockSpec((tm, tn), lambda j, i: (i, g1_blk + j)),
                  pl.BlockSpec((tm, tn), lambda j, i: (i, g2_blk + j)),
                  pl.BlockSpec((1, tn), lambda j, i: (0, j)),
                  pl.BlockSpec((1, tn), lambda j, i: (0, nb + j))],
        out_specs=pl.BlockSpec((tm, tn), lambda j, i: (i, j)),
        out_shape=jax.ShapeDtypeStruct((m, d), BF16),
        scratch_shapes=[pltpu.VMEM((ka, tn), BF16), pltpu.VMEM((kc, tn), BF16)],
        compiler_params=_cparams(2),
        name="mixer_merge",
    )(o, z, wa, wc, bc.reshape(1, d).astype(F32), proj, proj, bg2, bg2)


def _router_kernel(h_ref, g_ref, wr_ref, br_ref, valid_ref,
                   up_ref, e_ref, gate_ref, rank_ref, cnt_ref, carry_sc, *, tm):
    i = pl.program_id(0)

    @pl.when(i == 0)
    def _():
        carry_sc[...] = jnp.zeros(carry_sc.shape, F32)

    h = h_ref[...]
    ms = jnp.mean(h * h, axis=-1, keepdims=True)
    u = h * lax.rsqrt(ms + EPS) * g_ref[...]

    half = u.shape[1] // 2
    bits = lax.bitcast_convert_type(u.astype(BF16).astype(F32), jnp.uint32)
    up_ref[...] = (bits[:, half:] & jnp.uint32(0xFFFF0000)) | (bits[:, :half] >> 16)

    logits = lax.dot_general(wr_ref[...], u, (((1,), (1,)), ((), ())),
                             precision=lax.Precision.HIGHEST,
                             preferred_element_type=F32) + br_ref[...]
    n_e = logits.shape[0]
    eiota = lax.broadcasted_iota(jnp.int32, logits.shape, 0).astype(F32)
    work = logits
    sel = jnp.zeros(logits.shape, jnp.bool_)
    top_l, top_e = [], []
    for _ in range(TOP_K):
        mx = jnp.max(work, axis=0, keepdims=True)
        idx = jnp.min(jnp.where(work == mx, eiota, float(n_e)), axis=0, keepdims=True)
        hit = eiota == idx
        top_l.append(mx)
        top_e.append(idx)
        sel = sel | hit
        work = jnp.where(hit, -jnp.inf, work)
    ex = [jnp.exp(t - top_l[0]) for t in top_l]
    den = ex[0] + ex[1] + ex[2] + ex[3]
    gate_ref[...] = jnp.concatenate([e / den for e in ex], axis=0)
    e_ref[...] = jnp.concatenate(top_e, axis=0).astype(jnp.int32)

    selv = jnp.where(sel & (valid_ref[...] > 0.0), 1.0, 0.0)
    before = (lax.broadcasted_iota(jnp.int32, (tm, tm), 0)
              < lax.broadcasted_iota(jnp.int32, (tm, tm), 1)).astype(BF16)
    rank_all = jnp.dot(selv.astype(BF16), before, preferred_element_type=F32) + carry_sc[...]
    ranks = [jnp.sum(jnp.where(eiota == idx, rank_all, 0.0), axis=0, keepdims=True) for idx in top_e]
    rank_ref[...] = jnp.concatenate(ranks, axis=0).astype(jnp.int32)
    carry = carry_sc[...] + jnp.sum(selv, axis=1, keepdims=True)
    carry_sc[...] = carry
    cnt_ref[...] = jnp.broadcast_to(carry, cnt_ref.shape).astype(jnp.int32)


def _router(h2, g, w_router, b_router, valid, *, tm):
    m, d = h2.shape
    n_e = w_router.shape[1]
    tok = lambda dt: jax.ShapeDtypeStruct((TOP_K, m), dt)
    tok_spec = pl.BlockSpec((TOP_K, tm), lambda i: (0, i))
    return pl.pallas_call(
        functools.partial(_router_kernel, tm=tm),
        grid=(m // tm,),
        in_specs=[pl.BlockSpec((tm, d), lambda i: (i, 0)),
                  pl.BlockSpec((1, d), lambda i: (0, 0)),
                  pl.BlockSpec((n_e, d), lambda i: (0, 0)),
                  pl.BlockSpec((n_e, 1), lambda i: (0, 0)),
                  pl.BlockSpec((1, tm), lambda i: (0, i))],
        out_specs=[pl.BlockSpec((tm, d // 2), lambda i: (i, 0)),
                   tok_spec, tok_spec, tok_spec,
                   pl.BlockSpec((n_e, 128), lambda i: (0, 0))],
        out_shape=[jax.ShapeDtypeStruct((m, d // 2), jnp.uint32),
                   tok(jnp.int32), tok(F32), tok(jnp.int32),
                   jax.ShapeDtypeStruct((n_e, 128), jnp.int32)],
        scratch_shapes=[pltpu.VMEM((n_e, 1), F32)],
        compiler_params=_cparams(1),
        name="router",
    )(h2, g.reshape(1, d).astype(F32), w_router.T.astype(F32), b_router.reshape(n_e, 1).astype(F32), valid)


def _dispatch_kernel(dest_ref, te_ref, nu_ref, u_ref, xs_ref, zero_sc, sem, zsem, *, tm, n_tiles):
    i = pl.program_id(0)

    @pl.when(i == 0)
    def _():
        zero_sc[...] = jnp.zeros(zero_sc.shape, zero_sc.dtype)
        nu = nu_ref[0]

        def partly_filled(t):
            nxt = te_ref[jnp.minimum(t + 1, n_tiles - 1)]
            return (t >= nu - 1) | (te_ref[t] != nxt)

        def tile_copy(t):
            return pltpu.make_async_copy(zero_sc, xs_ref.at[pl.ds(t * MOE_TILE, MOE_TILE), :], zsem)

        def start(t, c):
            @pl.when(partly_filled(t))
            def _():
                tile_copy(t).start()
            return c

        def wait(t, c):
            @pl.when(partly_filled(t))
            def _():
                tile_copy(t).wait()
            return c

        lax.fori_loop(0, n_tiles, start, 0)
        lax.fori_loop(0, n_tiles, wait, 0)

    def start_row(r, c):
        for k in range(TOP_K):
            d = dest_ref[(i * tm + r) * TOP_K + k]
            pltpu.make_async_copy(u_ref.at[pl.ds(r, 1), :], xs_ref.at[pl.ds(d, 1), :], sem).start(priority=k % 2)
        return c

    lax.fori_loop(0, tm, start_row, 0, unroll=8)
    for k in range(TOP_K):
        pltpu.make_async_copy(u_ref, xs_ref.at[pl.ds(0, tm), :], sem).wait()


def _dispatch(dest_flat, tile_e, n_used, u_packed, n_rows, *, tm, n_tiles):
    m, w = u_packed.shape
    grid_spec = pltpu.PrefetchScalarGridSpec(
        num_scalar_prefetch=3,
        grid=(m // tm,),
        in_specs=[pl.BlockSpec((tm, w), lambda i, dest, te, nu: (i, 0))],
        out_specs=pl.BlockSpec(memory_space=pl.ANY),
        scratch_shapes=[pltpu.VMEM((MOE_TILE, w), jnp.uint32),
                        pltpu.SemaphoreType.DMA(()), pltpu.SemaphoreType.DMA(())],
    )
    return pl.pallas_call(
        functools.partial(_dispatch_kernel, tm=tm, n_tiles=n_tiles),
        grid_spec=grid_spec,
        out_shape=jax.ShapeDtypeStruct((n_rows, w), jnp.uint32),
        compiler_params=_cparams(1),
        name="moe_dispatch",
    )(dest_flat, tile_e, n_used, u_packed)


def _expert_rows_loop(t0, n, in_copy, out_copy, compute):
    @pl.when(n > 0)
    def _():
        in_copy(t0, 0).start(priority=1)

    def body(i, c):
        slot = i % 2

        @pl.when(i + 1 < n)
        def _():
            in_copy(t0 + i + 1, 1 - slot).start(priority=1)

        in_copy(t0 + i, slot).wait()

        @pl.when(i >= 2)
        def _():
            out_copy(t0 + i - 2, slot).wait()

        compute(slot)
        out_copy(t0 + i, slot).start(priority=1)
        return c

    lax.fori_loop(0, n, body, 0)

    @pl.when(n >= 2)
    def _():
        out_copy(t0 + n - 2, n % 2).wait()

    @pl.when(n >= 1)
    def _():
        out_copy(t0 + n - 1, (n - 1) % 2).wait()


def _zero_unused_tiles(ts_ref, tc_ref, obuf, out_copy, n_tiles):
    last = pl.num_programs(1) - 1

    @pl.when(pl.program_id(1) == last)
    def _():
        obuf[0] = jnp.zeros(obuf.shape[1:], obuf.dtype)

        def body(t, c):
            cp = out_copy(t, 0)
            cp.start()
            cp.wait()
            return c

        lax.fori_loop(ts_ref[last] + tc_ref[last], n_tiles, body, 0)


def _gate_up_kernel(ts_ref, tc_ref, xs_ref, wg_ref, wu_ref, bg_ref, bu_ref, act_ref,
                    wgb_sc, wub_sc, xbuf, obuf, xsem, osem, *, tf, n_tiles):
    j = pl.program_id(0)
    e = pl.program_id(1)
    n = tc_ref[e]

    def in_copy(t, slot):
        return pltpu.make_async_copy(xs_ref.at[pl.ds(t * MOE_TILE, MOE_TILE), :], xbuf.at[slot], xsem.at[slot])

    def out_copy(t, slot):
        return pltpu.make_async_copy(obuf.at[slot],
                                     act_ref.at[pl.ds(t * MOE_TILE, MOE_TILE), pl.ds(j * tf, tf)], osem.at[slot])

    @pl.when(n > 0)
    def _():
        wgb_sc[...] = wg_ref[...].astype(BF16)
        wub_sc[...] = wu_ref[...].astype(BF16)

    def compute(slot):
        w = xbuf[slot]
        half = w.shape[1]
        lo = lax.bitcast_convert_type(w << 16, F32).astype(BF16)
        hi = lax.bitcast_convert_type(w & jnp.uint32(0xFFFF0000), F32).astype(BF16)
        g = (jnp.dot(lo, wgb_sc[:half, :], preferred_element_type=F32)
             + jnp.dot(hi, wgb_sc[half:, :], preferred_element_type=F32) + bg_ref[...])
        u = (jnp.dot(lo, wub_sc[:half, :], preferred_element_type=F32)
             + jnp.dot(hi, wub_sc[half:, :], preferred_element_type=F32) + bu_ref[...])
        g = jnp.minimum(g, SWIGLU_LIMIT)
        u = jnp.clip(u, -SWIGLU_LIMIT, SWIGLU_LIMIT)
        obuf[slot] = ((u + 1.0) * (g * jax.nn.sigmoid(SWIGLU_ALPHA * g))).astype(obuf.dtype)

    _expert_rows_loop(ts_ref[e], n, in_copy, out_copy, compute)
    _zero_unused_tiles(ts_ref, tc_ref, obuf, out_copy, n_tiles)


def _gate_up(tile_start, tile_count, xs, w_gu, b_gu, *, n_tiles, tf):
    n_e, d, f2 = w_gu.shape
    f = f2 // 2
    nj = f // tf
    grid_spec = pltpu.PrefetchScalarGridSpec(
        num_scalar_prefetch=2,
        grid=(nj, n_e),
        in_specs=[pl.BlockSpec(memory_space=pl.ANY),
                  pl.BlockSpec((None, d, tf), lambda j, e, ts, tc: (e, 0, j)),
                  pl.BlockSpec((None, d, tf), lambda j, e, ts, tc: (e, 0, nj + j)),
                  pl.BlockSpec((None, 1, tf), lambda j, e, ts, tc: (e, 0, j)),
                  pl.BlockSpec((None, 1, tf), lambda j, e, ts, tc: (e, 0, nj + j))],
        out_specs=pl.BlockSpec(memory_space=pl.ANY),
        scratch_shapes=[pltpu.VMEM((d, tf), BF16), pltpu.VMEM((d, tf), BF16),
                        pltpu.VMEM((2, MOE_TILE, d // 2), jnp.uint32), pltpu.VMEM((2, MOE_TILE, tf), BF16),
                        pltpu.SemaphoreType.DMA((2,)), pltpu.SemaphoreType.DMA((2,))],
    )
    return pl.pallas_call(
        functools.partial(_gate_up_kernel, tf=tf, n_tiles=n_tiles),
        grid_spec=grid_spec,
        out_shape=jax.ShapeDtypeStruct((n_tiles * MOE_TILE, f), BF16),
        compiler_params=_cparams(2),
        name="moe_gate_up",
    )(tile_start, tile_count, xs, w_gu, w_gu, b_gu, b_gu)


def _down_kernel(ts_ref, tc_ref, act_ref, wd_ref, bd_ref, y_ref, wdb_sc, abuf, obuf, asem, osem, *, tn, n_tiles):
    j = pl.program_id(0)
    e = pl.program_id(1)
    n = tc_ref[e]

    def in_copy(t, slot):
        return pltpu.make_async_copy(act_ref.at[pl.ds(t * MOE_TILE, MOE_TILE), :], abuf.at[slot], asem.at[slot])

    def out_copy(t, slot):
        return pltpu.make_async_copy(obuf.at[slot],
                                     y_ref.at[pl.ds(t * MOE_TILE, MOE_TILE), pl.ds(j * tn, tn)], osem.at[slot])

    @pl.when(n > 0)
    def _():
        wdb_sc[...] = wd_ref[...].astype(BF16)

    def compute(slot):
        obuf[slot] = jnp.dot(abuf[slot], wdb_sc[...], preferred_element_type=F32) + bd_ref[...]

    _expert_rows_loop(ts_ref[e], n, in_copy, out_copy, compute)
    _zero_unused_tiles(ts_ref, tc_ref, obuf, out_copy, n_tiles)


def _down(tile_start, tile_count, act, w_d, b_d, *, n_tiles, tn):
    n_e, f, d = w_d.shape
    grid_spec = pltpu.PrefetchScalarGridSpec(
        num_scalar_prefetch=2,
        grid=(d // tn, n_e),
        in_specs=[pl.BlockSpec(memory_space=pl.ANY),
                  pl.BlockSpec((None, f, tn), lambda j, e, ts, tc: (e, 0, j)),
                  pl.BlockSpec((None, 1, tn), lambda j, e, ts, tc: (e, 0, j))],
        out_specs=pl.BlockSpec(memory_space=pl.ANY),
        scratch_shapes=[pltpu.VMEM((f, tn), BF16),
                        pltpu.VMEM((2, MOE_TILE, f), BF16), pltpu.VMEM((2, MOE_TILE, tn), F32),
                        pltpu.SemaphoreType.DMA((2,)), pltpu.SemaphoreType.DMA((2,))],
    )
    return pl.pallas_call(
        functools.partial(_down_kernel, tn=tn, n_tiles=n_tiles),
        grid_spec=grid_spec,
        out_shape=jax.ShapeDtypeStruct((n_tiles * MOE_TILE, d), F32),
        compiler_params=_cparams(2),
        name="moe_down",
    )(tile_start, tile_count, act, w_d, b_d)


def _combine_kernel(dest_ref, h_ref, gate_ref, g_ref, ys_ref, out_ref, buf, sem, *, tm, lp, row0, nst):
    s = pl.program_id(0)
    n_steps = pl.num_programs(0)

    def start_gather(step, slot):
        t0 = (step // nst) * lp + row0 + (step % nst) * tm

        def start_row(r, c):
            for k in range(TOP_K):
                d = dest_ref[(t0 + r) * TOP_K + k]
                pltpu.make_async_copy(ys_ref.at[pl.ds(d, 1), :], buf.at[slot, k, pl.ds(r, 1), :],
                                      sem.at[slot]).start(priority=k % 2)
            return c

        lax.fori_loop(0, tm, start_row, 0, unroll=8)

    @pl.when(s == 0)
    def _():
        start_gather(0, 0)

    @pl.when(s + 1 < n_steps)
    def _():
        start_gather(s + 1, (s + 1) % 2)

    slot = s % 2
    for k in range(TOP_K):
        pltpu.make_async_copy(ys_ref.at[pl.ds(0, tm), :], buf.at[slot, k], sem.at[slot]).wait()

    gate = gate_ref[...]
    acc = jnp.zeros(h_ref.shape, F32)
    for k in range(TOP_K):
        acc = acc + gate[:, k:k + 1] * buf[slot, k]
    h = h_ref[...] + acc
    ms = jnp.mean(h * h, axis=-1, keepdims=True)
    out_ref[...] = (h * lax.rsqrt(ms + EPS) * g_ref[...]).astype(out_ref.dtype)


def _combine(dest_flat, h2, gate, g, ys, *, batch, seq, lp, row0, tm):
    d = h2.shape[1]
    nb_b = lp // tm
    nb0 = row0 // tm
    nst = seq // tm
    blk = lambda s: (s // nst) * nb_b + nb0 + s % nst
    grid_spec = pltpu.PrefetchScalarGridSpec(
        num_scalar_prefetch=1,
        grid=(batch * nst,),
        in_specs=[pl.BlockSpec((tm, d), lambda s, dest: (blk(s), 0)),
                  pl.BlockSpec((tm, TOP_K), lambda s, dest: (blk(s), 0)),
                  pl.BlockSpec((1, d), lambda s, dest: (0, 0)),
                  pl.BlockSpec(memory_space=pl.ANY)],
        out_specs=pl.BlockSpec((None, tm, d), lambda s, dest: (s // nst, s % nst, 0)),
        scratch_shapes=[pltpu.VMEM((2, TOP_K, tm, d), F32), pltpu.SemaphoreType.DMA((2,))],
    )
    return pl.pallas_call(
        functools.partial(_combine_kernel, tm=tm, lp=lp, row0=row0, nst=nst),
        grid_spec=grid_spec,
        out_shape=jax.ShapeDtypeStruct((batch, seq, d), F32),
        compiler_params=_cparams(1),
        name="moe_combine",
    )(dest_flat, h2, gate, g.reshape(1, d).astype(F32), ys)


def _pick(pref, n):
    t = pref
    while n % t:
        t //= 2
    return t


def _row_tile(n, pref):
    t = pref // ROW_ALIGN * ROW_ALIGN
    while n % t:
        t -= ROW_ALIGN
    return t


def kernel(x, meta_tokens, norm_mix_g, w_in, b_gate, lambda_q1, lambda_k1, lambda_q2, lambda_k2, head_norm_g, w_attn_out, conv_w, conv_b, conv_ln_g, conv_ln_b, w_conv_out, b_conv_out, w_out, norm_ffn_g, w_router, b_router, w_gate_up, b_gate_up, w_down, b_down, final_norm_g):
    batch, seq, d = x.shape
    depth = w_in.shape[0]
    assert depth == 1 and seq % ROW_ALIGN == 0 and N_META <= ROW_ALIGN
    n_heads = d // 256
    hw = 2 * HEAD_DIM
    qk_w = n_heads * hw
    conv_ch = conv_w.shape[2]
    n_pad = ROW_ALIGN - N_META
    lp = n_pad + N_META + seq
    tp = batch * lp
    f = w_down.shape[2]
    layer = 0
    lam_init = 0.8 - 0.6 * math.exp(-0.3 * layer)

    h0, u = _embed_norm(x, meta_tokens, norm_mix_g[layer], n_pad=n_pad)

    proj = _matmul(u, w_in[layer], BF16, _row_tile(tp, 1536), _pick(512, w_in.shape[2]), name="in_proj")

    tq = 384 if lp % 384 == 0 else ROW_ALIGN
    vt = proj[:, 2 * qk_w:3 * qk_w].reshape(batch, lp, n_heads, hw).transpose(0, 2, 3, 1)
    o = _attention(proj, vt, lambda_q1[layer], lambda_k1[layer], lambda_q2[layer], lambda_k2[layer],
                   head_norm_g[layer], batch=batch, lp=lp, n_heads=n_heads, tq=tq, lam_init=lam_init,
                   n_pad=n_pad)
    ca_col = 3 * qk_w
    z = _conv_branch(proj, conv_w[layer], conv_b[layer], conv_ln_g[layer], conv_ln_b[layer],
                     ca_blk=ca_col // conv_ch, cg_blk=ca_col // conv_ch + 1, tm=_pick(256, tp))
    merged = _merge(o, z, proj, w_attn_out[layer], w_conv_out[layer],
                    b_conv_out[layer], b_gate[layer], gate_col=ca_col + 2 * conv_ch,
                    tm=_row_tile(tp, 768), tn=_pick(512, d))
    h2 = _matmul(merged, w_out[layer], F32, _row_tile(tp, 768), _pick(512, d), res=h0, name="out_proj")

    pos = np.arange(tp) % lp
    valid_np = pos >= n_pad
    valid = jnp.asarray(valid_np.astype(np.float32).reshape(1, tp))
    u_packed, top_e, gate_t, rank_t, cnt = _router(h2, norm_ffn_g[layer], w_router[layer], b_router[layer],
                                                   valid, tm=_pick(256, tp))
    counts = cnt[:, 0]
    padded = (counts + MOE_TILE - 1) // MOE_TILE * MOE_TILE
    e_ids = np.arange(N_EXPERTS)
    pad_end = jnp.sum(jnp.where(jnp.asarray(e_ids[None, :] <= e_ids[:, None]), padded[None, :], 0), axis=1)
    pad_start = pad_end - padded
    start_tok = jnp.sum(jnp.where(top_e[:, :, None] == jnp.asarray(e_ids, jnp.int32), pad_start, 0), axis=-1)
    n_real = int(valid_np.sum()) * TOP_K
    n_tiles = -(-(n_real + N_EXPERTS * (MOE_TILE - 1)) // MOE_TILE)
    n_slots = n_tiles * MOE_TILE
    dump = n_slots + (np.cumsum(~valid_np) - 1)[None, :] * TOP_K + np.arange(TOP_K)[:, None]
    dest_t = jnp.where(jnp.asarray(valid_np)[None, :], start_tok + rank_t, jnp.asarray(dump, jnp.int32))
    dest_flat = dest_t.T.reshape(-1).astype(jnp.int32)
    n_dump = int((~valid_np).sum()) * TOP_K
    n_used = (pad_end[-1] // MOE_TILE).astype(jnp.int32).reshape(1)
    tile_start = jnp.asarray(np.arange(n_tiles, dtype=np.int32) * MOE_TILE)
    tile_e = jnp.minimum(jnp.sum((pad_end[None, :] <= tile_start[:, None]).astype(jnp.int32), axis=1),
                         N_EXPERTS - 1).astype(jnp.int32)

    xs = _dispatch(dest_flat, tile_e, n_used, u_packed, n_slots + n_dump, tm=ROW_ALIGN, n_tiles=n_tiles)
    tile_first = (pad_start // MOE_TILE).astype(jnp.int32)
    tile_count = (padded // MOE_TILE).astype(jnp.int32)
    act = _gate_up(tile_first, tile_count, xs, w_gate_up[layer],
                   b_gate_up[layer].reshape(N_EXPERTS, 1, 2 * f).astype(F32), n_tiles=n_tiles, tf=_pick(512, f))
    ys = _down(tile_first, tile_count, act, w_down[layer],
               b_down[layer].reshape(N_EXPERTS, 1, d).astype(F32), n_tiles=n_tiles, tn=_pick(2048, d))
    return _combine(dest_flat, h2, gate_t.T, final_norm_g, ys, batch=batch, seq=seq, lp=lp,
                    row0=ROW_ALIGN, tm=ROW_ALIGN)
```

```python
import functools
import math

import numpy as np
import jax
import jax.numpy as jnp
from jax import lax
from jax.experimental import pallas as pl
from jax.experimental.pallas import tpu as pltpu

N_META = 16
HEAD_DIM = 64
N_EXPERTS = 32
TOP_K = 4
CONV_K = 31
EPS = 1e-5
SWIGLU_LIMIT = 7.0
SWIGLU_ALPHA = 1.702
SUBLANES = 8
LANES = 128
ROW_ALIGN = 128
CONV_HALO = 32
MOE_TILE = 256
ATTN_HEADS_PER_STEP = 4
VMEM_LIMIT = 56 * 1024 * 1024

F32 = jnp.float32
BF16 = jnp.bfloat16


def _cparams(n_axes, flags=None):
    return pltpu.CompilerParams(dimension_semantics=("arbitrary",) * n_axes,
                                vmem_limit_bytes=VMEM_LIMIT, flags=flags)


def _embed_norm_kernel(x_ref, meta_ref, g_ref, h_ref, u_ref, *, n_pad):
    i = pl.program_id(1)

    @pl.when(i == 0)
    def _():
        h_ref[0:n_pad, :] = jnp.zeros((n_pad, h_ref.shape[1]), F32)
        h_ref[n_pad:, :] = meta_ref[...]

    @pl.when(i > 0)
    def _():
        h_ref[...] = x_ref[...]

    h = h_ref[...]
    ms = jnp.mean(h * h, axis=-1, keepdims=True)
    u_ref[...] = (h * lax.rsqrt(ms + EPS) * g_ref[...]).astype(u_ref.dtype)


def _embed_norm(x, meta, g, *, n_pad):
    batch, seq, d = x.shape
    tm = ROW_ALIGN
    nb = (n_pad + N_META + seq) // tm
    out_spec = pl.BlockSpec((tm, d), lambda b, i: (b * nb + i, 0))
    return pl.pallas_call(
        functools.partial(_embed_norm_kernel, n_pad=n_pad),
        grid=(batch, nb),
        in_specs=[pl.BlockSpec((None, tm, d), lambda b, i: (b, jnp.maximum(i - 1, 0), 0)),
                  pl.BlockSpec((N_META, d), lambda b, i: (0, 0)),
                  pl.BlockSpec((1, d), lambda b, i: (0, 0))],
        out_specs=[out_spec, out_spec],
        out_shape=[jax.ShapeDtypeStruct((batch * nb * tm, d), F32),
                   jax.ShapeDtypeStruct((batch * nb * tm, d), BF16)],
        compiler_params=_cparams(2),
        name="embed_norm",
    )(x, meta.astype(F32), g.reshape(1, d).astype(F32))


def _cast_weight_once(w_ref, wb_sc):
    @pl.when(pl.program_id(1) == 0)
    def _():
        wb_sc[...] = w_ref[...].astype(BF16)


def _matmul_kernel(a_ref, w_ref, o_ref, wb_sc):
    _cast_weight_once(w_ref, wb_sc)
    o_ref[...] = jnp.dot(a_ref[...], wb_sc[...], preferred_element_type=F32).astype(o_ref.dtype)


def _matmul_res_kernel(a_ref, w_ref, r_ref, o_ref, wb_sc):
    _cast_weight_once(w_ref, wb_sc)
    acc = jnp.dot(a_ref[...], wb_sc[...], preferred_element_type=F32)
    o_ref[...] = (acc + r_ref[...]).astype(o_ref.dtype)


def _matmul(a, w, out_dtype, tm, tn, res=None, name="matmul"):
    m, k = a.shape
    n = w.shape[1]
    in_specs = [pl.BlockSpec((tm, k), lambda j, i: (i, 0)),
                pl.BlockSpec((k, tn), lambda j, i: (0, j))]
    args = [a, w]
    kern = _matmul_kernel
    if res is not None:
        in_specs.append(pl.BlockSpec((tm, tn), lambda j, i: (i, j)))
        args.append(res)
        kern = _matmul_res_kernel
    return pl.pallas_call(
        kern,
        grid=(n // tn, m // tm),
        in_specs=in_specs,
        out_specs=pl.BlockSpec((tm, tn), lambda j, i: (i, j)),
        out_shape=jax.ShapeDtypeStruct((m, n), out_dtype),
        scratch_shapes=[pltpu.VMEM((k, tn), BF16)],
        compiler_params=_cparams(2),
        name=name,
    )(*args)


def _attn_kernel(lq1_ref, lk1_ref, lq2_ref, lk2_ref, hg_ref, bias_ref, q_ref, k_ref, vt_ref, o_ref,
                 q12_sc, s_sc, m_sc, l_sc, acc_sc, *, tq, lam_init, n_hd):
    hw = 2 * HEAD_DIM
    nq = q_ref.shape[0] // tq
    lam = (jnp.exp(jnp.sum(lq1_ref[...] * lk1_ref[...], axis=-1, keepdims=True))
           - jnp.exp(jnp.sum(lq2_ref[...] * lk2_ref[...], axis=-1, keepdims=True)) + lam_init)

    def q_tile(qi, carry):
        q0 = pl.multiple_of(qi * tq, tq)
        for hd in range(n_hd):
            q = (q_ref[pl.ds(q0, tq), hd * hw:(hd + 1) * hw].astype(F32)
                 * (HEAD_DIM ** -0.5 * math.log2(math.e))).astype(BF16)
            lane = lax.broadcasted_iota(jnp.int32, q.shape, 1)
            zero = jnp.zeros_like(q)
            q12_sc[2 * hd] = jnp.where(lane < HEAD_DIM, q, zero)
            q12_sc[2 * hd + 1] = jnp.where(lane >= HEAD_DIM, q, zero)
        m_sc[...] = jnp.full(m_sc.shape, -jnp.inf, F32)
        l_sc[...] = jnp.zeros(l_sc.shape, F32)
        acc_sc[...] = jnp.zeros(acc_sc.shape, F32)

        def scores(kj, c):
            hd = c // 2
            k0 = pl.multiple_of(kj * tq, tq)
            s = lax.dot_general(k_ref[pl.ds(k0, tq), hd * hw:(hd + 1) * hw], q12_sc[c],
                                (((1,), (1,)), ((), ())), preferred_element_type=F32)
            kind = jnp.where(kj == 0, 1, 0) + jnp.where(kj == qi, 2, 0)
            s_sc[c] = s + bias_ref[kind]

        def update(kj, c):
            hd = c // 2
            k0 = pl.multiple_of(kj * tq, tq)
            s = s_sc[c]
            m_prev = m_sc[c]
            m_new = jnp.maximum(m_prev, jnp.max(s, axis=0, keepdims=True))
            alpha = jnp.exp2(m_prev - m_new)
            p = jnp.exp2(s - m_new)
            l_sc[c] = alpha * l_sc[c] + jnp.sum(p.reshape(tq // 8, 8, tq), axis=0)
            acc_sc[c] = alpha * acc_sc[c] + jnp.dot(vt_ref[hd, :, pl.ds(k0, tq)], p.astype(BF16),
                                                    preferred_element_type=F32)
            m_sc[c] = m_new

        for hd in range(n_hd):
            scores(0, 2 * hd)

        def body(kj, c):
            nxt = jnp.minimum(kj + 1, qi)
            for hd in range(n_hd):
                scores(kj, 2 * hd + 1)
                update(kj, 2 * hd)
                scores(nxt, 2 * hd)
            for hd in range(n_hd):
                update(kj, 2 * hd + 1)
            return c

        lax.fori_loop(0, qi + 1, body, 0)

        for hd in range(n_hd):
            o1 = acc_sc[2 * hd] / jnp.sum(l_sc[2 * hd], axis=0, keepdims=True)
            o2 = acc_sc[2 * hd + 1] / jnp.sum(l_sc[2 * hd + 1], axis=0, keepdims=True)
            o = o1 - lam * o2
            ms = jnp.mean(o * o, axis=0, keepdims=True)
            o = o * lax.rsqrt(ms + EPS) * hg_ref[...] * (1.0 - lam_init)
            o_ref[pl.ds(q0, tq), hd * hw:(hd + 1) * hw] = o.T.astype(o_ref.dtype)
        return carry

    lax.fori_loop(0, nq, q_tile, 0)


def _attn_bias(tq, n_pad):
    neg = np.float32(np.finfo(np.float32).min)
    r = np.arange(tq)[:, None]
    c = np.arange(tq)[None, :]
    pad = np.broadcast_to(r < n_pad, (tq, tq))
    future = r > c
    tiles = [np.zeros((tq, tq), bool), pad, future, pad | future]
    return jnp.asarray(np.stack([np.where(t, neg, np.float32(0)) for t in tiles]).astype(np.float32))


def _attention(proj, vt, lq1, lk1, lq2, lk2, head_g, *, batch, lp, n_heads, tq, lam_init, n_pad):
    hw = 2 * HEAD_DIM
    n_hd = ATTN_HEADS_PER_STEP if n_heads % ATTN_HEADS_PER_STEP == 0 else 1
    assert n_pad <= tq
    koff = n_heads // n_hd
    vec = lambda a: a.reshape(1, -1).astype(F32)
    small = lambda n: pl.BlockSpec((1, n), lambda b, h: (0, 0))
    return pl.pallas_call(
        functools.partial(_attn_kernel, tq=tq, lam_init=lam_init, n_hd=n_hd),
        grid=(batch, n_heads // n_hd),
        in_specs=[small(HEAD_DIM), small(HEAD_DIM), small(HEAD_DIM), small(HEAD_DIM),
                  pl.BlockSpec((hw, 1), lambda b, h: (0, 0)),
                  pl.BlockSpec((4, tq, tq), lambda b, h: (0, 0, 0)),
                  pl.BlockSpec((lp, n_hd * hw), lambda b, h: (b, h)),
                  pl.BlockSpec((lp, n_hd * hw), lambda b, h: (b, koff + h)),
                  pl.BlockSpec((None, n_hd, hw, lp), lambda b, h: (b, h, 0, 0))],
        out_specs=pl.BlockSpec((lp, n_hd * hw), lambda b, h: (b, h)),
        out_shape=jax.ShapeDtypeStruct((batch * lp, n_heads * hw), BF16),
        scratch_shapes=[pltpu.VMEM((2 * n_hd, tq, hw), BF16),
                        pltpu.VMEM((2 * n_hd, tq, tq), F32),
                        pltpu.VMEM((2 * n_hd, 1, tq), F32), pltpu.VMEM((2 * n_hd, 8, tq), F32),
                        pltpu.VMEM((2 * n_hd, hw, tq), F32)],
        compiler_params=_cparams(2),
        name="diff_attention",
    )(vec(lq1), vec(lk1), vec(lq2), vec(lk2), head_g.reshape(hw, 1).astype(F32), _attn_bias(tq, n_pad),
      proj, proj, vt)


def _conv_kernel(ca_ref, cg_ref, ca_h_ref, cg_h_ref, w_ref, b_ref, lg_ref, lb_ref, z_ref,
                 ext_sc, sh_sc, y_sc, *, tm):
    i = pl.program_id(0)
    n_ch = w_ref.shape[1]
    n_ext = CONV_HALO + tm
    glu = lambda a, g: a.astype(F32) * jax.nn.sigmoid(g.astype(F32))
    halo = glu(ca_h_ref[...], cg_h_ref[...])
    ext_sc[0:CONV_HALO, :] = jnp.where(i > 0, halo, jnp.zeros_like(halo))
    ext_sc[CONV_HALO:n_ext, :] = glu(ca_ref[...], cg_ref[...])
    ext_sc[n_ext:n_ext + SUBLANES, :] = jnp.zeros((SUBLANES, n_ch), F32)
    base = CONV_HALO - (CONV_K - 1)

    def slab(lc, carry):
        l0 = pl.multiple_of(lc * LANES, LANES)
        for rho in range(SUBLANES):
            sh_sc[rho] = ext_sc[rho:rho + n_ext, pl.ds(l0, LANES)]
        acc = jnp.zeros((tm, LANES), F32) + b_ref[:, pl.ds(l0, LANES)]
        for j in range(CONV_K):
            rho = (base + j) % SUBLANES
            a = base + j - rho
            acc = acc + w_ref[j:j + 1, pl.ds(l0, LANES)] * sh_sc[rho, a:a + tm, :]
        y_sc[:, pl.ds(l0, LANES)] = acc
        return carry

    lax.fori_loop(0, n_ch // LANES, slab, 0)
    acc = y_sc[...]
    mu = jnp.mean(acc, axis=-1, keepdims=True)
    d = acc - mu
    var = jnp.mean(d * d, axis=-1, keepdims=True)
    y = d * lax.rsqrt(var + EPS) * lg_ref[...] + lb_ref[...]
    z_ref[...] = (y * jax.nn.sigmoid(y)).astype(z_ref.dtype)


def _conv_branch(proj, conv_w, conv_b, ln_g, ln_b, *, ca_blk, cg_blk, tm):
    m = proj.shape[0]
    c = conv_w.shape[1]
    hb = tm // CONV_HALO
    row = lambda a: a.reshape(1, c).astype(F32)
    vec = pl.BlockSpec((1, c), lambda i: (0, 0))
    return pl.pallas_call(
        functools.partial(_conv_kernel, tm=tm),
        grid=(m // tm,),
        in_specs=[pl.BlockSpec((tm, c), lambda i: (i, ca_blk)),
                  pl.BlockSpec((tm, c), lambda i: (i, cg_blk)),
                  pl.BlockSpec((CONV_HALO, c), lambda i: (jnp.maximum(i * hb - 1, 0), ca_blk)),
                  pl.BlockSpec((CONV_HALO, c), lambda i: (jnp.maximum(i * hb - 1, 0), cg_blk)),
                  pl.BlockSpec((CONV_K, c), lambda i: (0, 0)),
                  vec, vec, vec],
        out_specs=pl.BlockSpec((tm, c), lambda i: (i, 0)),
        out_shape=jax.ShapeDtypeStruct((m, c), BF16),
        scratch_shapes=[pltpu.VMEM((CONV_HALO + tm + SUBLANES, c), F32),
                        pltpu.VMEM((SUBLANES, CONV_HALO + tm, LANES), F32),
                        pltpu.VMEM((tm, c), F32)],
        compiler_params=_cparams(1),
        name="conformer_conv",
    )(proj, proj, proj, proj, conv_w.astype(F32), row(conv_b), row(ln_g), row(ln_b))


def _merge_kernel(o_ref, z_ref, wa_ref, wc_ref, bc_ref, g1_ref, g2_ref, bg1_ref, bg2_ref, out_ref,
                  wab_sc, wcb_sc):
    _cast_weight_once(wa_ref, wab_sc)
    _cast_weight_once(wc_ref, wcb_sc)
    ya = jnp.dot(o_ref[...], wab_sc[...], preferred_element_type=F32)
    yc = jnp.dot(z_ref[...], wcb_sc[...], preferred_element_type=F32) + bc_ref[...]
    g1 = jax.nn.sigmoid(g1_ref[...].astype(F32) + bg1_ref[...])
    g2 = jax.nn.sigmoid(g2_ref[...].astype(F32) + bg2_ref[...])
    out_ref[...] = (g1 * ya + g2 * yc).astype(out_ref.dtype)


def _merge(o, z, proj, wa, wc, bc, bg, *, gate_col, tm, tn):
    m, ka = o.shape
    kc = z.shape[1]
    d = wa.shape[1]
    g1_blk = gate_col // tn
    g2_blk = (gate_col + d) // tn
    nb = d // tn
    bg2 = bg.reshape(1, 2 * d).astype(F32)
    return pl.pallas_call(
        _merge_kernel,
        grid=(nb, m // tm),
        in_specs=[pl.BlockSpec((tm, ka), lambda j, i: (i, 0)),
                  pl.BlockSpec((tm, kc), lambda j, i: (i, 0)),
                  pl.BlockSpec((ka, tn), lambda j, i: (0, j)),
                  pl.BlockSpec((kc, tn), lambda j, i: (0, j)),
                  pl.BlockSpec((1, tn), lambda j, i: (0, j)),
                  pl.BlockSpec((tm, tn), lambda j, i: (i, g1_blk + j)),
                  pl.BlockSpec((tm, tn), lambda j, i: (i, g2_blk + j)),
                  pl.BlockSpec((1, tn), lambda j, i: (0, j)),
                  pl.BlockSpec((1, tn), lambda j, i: (0, nb + j))],
        out_specs=pl.BlockSpec((tm, tn), lambda j, i: (i, j)),
        out_shape=jax.ShapeDtypeStruct((m, d), BF16),
        scratch_shapes=[pltpu.VMEM((ka, tn), BF16), pltpu.VMEM((kc, tn), BF16)],
        compiler_params=_cparams(2),
        name="mixer_merge",
    )(o, z, wa, wc, bc.reshape(1, d).astype(F32), proj, proj, bg2, bg2)


def _router_kernel(h_ref, g_ref, wr_ref, br_ref, valid_ref,
                   up_ref, e_ref, gate_ref, rank_ref, cnt_ref, carry_sc, *, tm):
    i = pl.program_id(0)

    @pl.when(i == 0)
    def _():
        carry_sc[...] = jnp.zeros(carry_sc.shape, F32)

    h = h_ref[...]
    ms = jnp.mean(h * h, axis=-1, keepdims=True)
    u = h * lax.rsqrt(ms + EPS) * g_ref[...]

    half = u.shape[1] // 2
    bits = lax.bitcast_convert_type(u.astype(BF16).astype(F32), jnp.uint32)
    up_ref[...] = (bits[:, half:] & jnp.uint32(0xFFFF0000)) | (bits[:, :half] >> 16)

    logits = lax.dot_general(wr_ref[...], u, (((1,), (1,)), ((), ())),
                             precision=lax.Precision.HIGHEST,
                             preferred_element_type=F32) + br_ref[...]
    n_e = logits.shape[0]
    eiota = lax.broadcasted_iota(jnp.int32, logits.shape, 0).astype(F32)
    work = logits
    sel = jnp.zeros(logits.shape, jnp.bool_)
    top_l, top_e = [], []
    for _ in range(TOP_K):
        mx = jnp.max(work, axis=0, keepdims=True)
        idx = jnp.min(jnp.where(work == mx, eiota, float(n_e)), axis=0, keepdims=True)
        hit = eiota == idx
        top_l.append(mx)
        top_e.append(idx)
        sel = sel | hit
        work = jnp.where(hit, -jnp.inf, work)
    ex = [jnp.exp(t - top_l[0]) for t in top_l]
    den = ex[0] + ex[1] + ex[2] + ex[3]
    gate_ref[...] = jnp.concatenate([e / den for e in ex], axis=0)
    e_ref[...] = jnp.concatenate(top_e, axis=0).astype(jnp.int32)

    selv = jnp.where(sel & (valid_ref[...] > 0.0), 1.0, 0.0)
    before = (lax.broadcasted_iota(jnp.int32, (tm, tm), 0)
              < lax.broadcasted_iota(jnp.int32, (tm, tm), 1)).astype(BF16)
    rank_all = jnp.dot(selv.astype(BF16), before, preferred_element_type=F32) + carry_sc[...]
    ranks = [jnp.sum(jnp.where(eiota == idx, rank_all, 0.0), axis=0, keepdims=True) for idx in top_e]
    rank_ref[...] = jnp.concatenate(ranks, axis=0).astype(jnp.int32)
    carry = carry_sc[...] + jnp.sum(selv, axis=1, keepdims=True)
    carry_sc[...] = carry
    cnt_ref[...] = jnp.broadcast_to(carry, cnt_ref.shape).astype(jnp.int32)


def _router(h2, g, w_router, b_router, valid, *, tm):
    m, d = h2.shape
    n_e = w_router.shape[1]
    tok = lambda dt: jax.ShapeDtypeStruct((TOP_K, m), dt)
    tok_spec = pl.BlockSpec((TOP_K, tm), lambda i: (0, i))
    return pl.pallas_call(
        functools.partial(_router_kernel, tm=tm),
        grid=(m // tm,),
        in_specs=[pl.BlockSpec((tm, d), lambda i: (i, 0)),
                  pl.BlockSpec((1, d), lambda i: (0, 0)),
                  pl.BlockSpec((n_e, d), lambda i: (0, 0)),
                  pl.BlockSpec((n_e, 1), lambda i: (0, 0)),
                  pl.BlockSpec((1, tm), lambda i: (0, i))],
        out_specs=[pl.BlockSpec((tm, d // 2), lambda i: (i, 0)),
                   tok_spec, tok_spec, tok_spec,
                   pl.BlockSpec((n_e, 128), lambda i: (0, 0))],
        out_shape=[jax.ShapeDtypeStruct((m, d // 2), jnp.uint32),
                   tok(jnp.int32), tok(F32), tok(jnp.int32),
                   jax.ShapeDtypeStruct((n_e, 128), jnp.int32)],
        scratch_shapes=[pltpu.VMEM((n_e, 1), F32)],
        compiler_params=_cparams(1),
        name="router",
    )(h2, g.reshape(1, d).astype(F32), w_router.T.astype(F32), b_router.reshape(n_e, 1).astype(F32), valid)


def _dispatch_kernel(dest_ref, te_ref, nu_ref, u_ref, xs_ref, zero_sc, sem, zsem, *, tm, n_tiles):
    i = pl.program_id(0)

    @pl.when(i == 0)
    def _():
        zero_sc[...] = jnp.zeros(zero_sc.shape, zero_sc.dtype)
        nu = nu_ref[0]

        def partly_filled(t):
            nxt = te_ref[jnp.minimum(t + 1, n_tiles - 1)]
            return (t >= nu - 1) | (te_ref[t] != nxt)

        def tile_copy(t):
            return pltpu.make_async_copy(zero_sc, xs_ref.at[pl.ds(t * MOE_TILE, MOE_TILE), :], zsem)

        def start(t, c):
            @pl.when(partly_filled(t))
            def _():
                tile_copy(t).start()
            return c

        def wait(t, c):
            @pl.when(partly_filled(t))
            def _():
                tile_copy(t).wait()
            return c

        lax.fori_loop(0, n_tiles, start, 0)
        lax.fori_loop(0, n_tiles, wait, 0)

    def start_row(r, c):
        for k in range(TOP_K):
            d = dest_ref[(i * tm + r) * TOP_K + k]
            pltpu.make_async_copy(u_ref.at[pl.ds(r, 1), :], xs_ref.at[pl.ds(d, 1), :], sem).start(priority=k % 2)
        return c

    lax.fori_loop(0, tm, start_row, 0, unroll=8)
    for k in range(TOP_K):
        pltpu.make_async_copy(u_ref, xs_ref.at[pl.ds(0, tm), :], sem).wait()


def _dispatch(dest_flat, tile_e, n_used, u_packed, n_rows, *, tm, n_tiles):
    m, w = u_packed.shape
    grid_spec = pltpu.PrefetchScalarGridSpec(
        num_scalar_prefetch=3,
        grid=(m // tm,),
        in_specs=[pl.BlockSpec((tm, w), lambda i, dest, te, nu: (i, 0))],
        out_specs=pl.BlockSpec(memory_space=pl.ANY),
        scratch_shapes=[pltpu.VMEM((MOE_TILE, w), jnp.uint32),
                        pltpu.SemaphoreType.DMA(()), pltpu.SemaphoreType.DMA(())],
    )
    return pl.pallas_call(
        functools.partial(_dispatch_kernel, tm=tm, n_tiles=n_tiles),
        grid_spec=grid_spec,
        out_shape=jax.ShapeDtypeStruct((n_rows, w), jnp.uint32),
        compiler_params=_cparams(1),
        name="moe_dispatch",
    )(dest_flat, tile_e, n_used, u_packed)


def _stream_expert_tiles(ts_ref, tc_ref, g_ref, obuf, in_copy, out_copy, compute, *, nj, n_e, n_tiles):
    j = pl.program_id(0)
    e = pl.program_id(1)
    n_used = ts_ref[n_e - 1] + tc_ref[n_e - 1]
    t0 = ts_ref[e]

    @pl.when((j == 0) & (e == 0))
    def _():
        g_ref[0] = 0

        @pl.when(n_used > 0)
        def _():
            in_copy(0, 0).start()

    def body(i, c):
        g = g_ref[0]
        slot = g % 2
        t = t0 + i
        wraps = t + 1 >= n_used

        @pl.when(jnp.logical_not(wraps & (j == nj - 1)))
        def _():
            in_copy(jnp.where(wraps, 0, t + 1), 1 - slot).start()

        in_copy(t, slot).wait()

        @pl.when(g >= 2)
        def _():
            out_copy(j, t, slot).wait()

        compute(slot)
        out_copy(j, t, slot).start()
        g_ref[0] = g + 1
        return c

    lax.fori_loop(0, tc_ref[e], body, 0)

    @pl.when((j == nj - 1) & (e == n_e - 1))
    def _():
        g = g_ref[0]

        @pl.when(g >= 2)
        def _():
            out_copy(j, 0, g % 2).wait()

        @pl.when(g >= 1)
        def _():
            out_copy(j, 0, (g - 1) % 2).wait()

        obuf[0] = jnp.zeros(obuf.shape[1:], obuf.dtype)

        def zero_tile(t, c):
            for jj in range(nj):
                cp = out_copy(jj, t, 0)
                cp.start()
                cp.wait()
            return c

        lax.fori_loop(n_used, n_tiles, zero_tile, 0)


def _gate_up_kernel(ts_ref, tc_ref, xs_ref, wg_ref, wu_ref, bg_ref, bu_ref, act_ref,
                    wgb_sc, wub_sc, xbuf, obuf, g_ref, xsem, osem, *, tf, nj, n_e, n_tiles):
    def in_copy(t, slot):
        return pltpu.make_async_copy(xs_ref.at[pl.ds(t * MOE_TILE, MOE_TILE), :], xbuf.at[slot], xsem.at[slot])

    def out_copy(jj, t, slot):
        return pltpu.make_async_copy(obuf.at[slot],
                                     act_ref.at[pl.ds(t * MOE_TILE, MOE_TILE), pl.ds(jj * tf, tf)], osem.at[slot])

    @pl.when(tc_ref[pl.program_id(1)] > 0)
    def _():
        wgb_sc[...] = wg_ref[...].astype(BF16)
        wub_sc[...] = wu_ref[...].astype(BF16)

    def compute(slot):
        w = xbuf[slot]
        half = w.shape[1]
        lo = lax.bitcast_convert_type(w << 16, F32).astype(BF16)
        hi = lax.bitcast_convert_type(w & jnp.uint32(0xFFFF0000), F32).astype(BF16)
        g = (jnp.dot(lo, wgb_sc[:half, :], preferred_element_type=F32)
             + jnp.dot(hi, wgb_sc[half:, :], preferred_element_type=F32) + bg_ref[...])
        u = (jnp.dot(lo, wub_sc[:half, :], preferred_element_type=F32)
             + jnp.dot(hi, wub_sc[half:, :], preferred_element_type=F32) + bu_ref[...])
        g = jnp.minimum(g, SWIGLU_LIMIT)
        u = jnp.clip(u, -SWIGLU_LIMIT, SWIGLU_LIMIT)
        obuf[slot] = ((u + 1.0) * (g * jax.nn.sigmoid(SWIGLU_ALPHA * g))).astype(obuf.dtype)

    _stream_expert_tiles(ts_ref, tc_ref, g_ref, obuf, in_copy, out_copy, compute, nj=nj, n_e=n_e, n_tiles=n_tiles)


def _gate_up(tile_start, tile_count, xs, w_gu, b_gu, *, n_tiles, tf):
    n_e, d, f2 = w_gu.shape
    f = f2 // 2
    nj = f // tf
    grid_spec = pltpu.PrefetchScalarGridSpec(
        num_scalar_prefetch=2,
        grid=(nj, n_e),
        in_specs=[pl.BlockSpec(memory_space=pl.ANY),
                  pl.BlockSpec((None, d, tf), lambda j, e, ts, tc: (e, 0, j)),
                  pl.BlockSpec((None, d, tf), lambda j, e, ts, tc: (e, 0, nj + j)),
                  pl.BlockSpec((None, 1, tf), lambda j, e, ts, tc: (e, 0, j)),
                  pl.BlockSpec((None, 1, tf), lambda j, e, ts, tc: (e, 0, nj + j))],
        out_specs=pl.BlockSpec(memory_space=pl.ANY),
        scratch_shapes=[pltpu.VMEM((d, tf), BF16), pltpu.VMEM((d, tf), BF16),
                        pltpu.VMEM((2, MOE_TILE, d // 2), jnp.uint32), pltpu.VMEM((2, MOE_TILE, tf), BF16),
                        pltpu.SMEM((1,), jnp.int32),
                        pltpu.SemaphoreType.DMA((2,)), pltpu.SemaphoreType.DMA((2,))],
    )
    return pl.pallas_call(
        functools.partial(_gate_up_kernel, tf=tf, nj=nj, n_e=n_e, n_tiles=n_tiles),
        grid_spec=grid_spec,
        out_shape=jax.ShapeDtypeStruct((n_tiles * MOE_TILE, f), BF16),
        compiler_params=_cparams(2),
        name="moe_gate_up",
    )(tile_start, tile_count, xs, w_gu, w_gu, b_gu, b_gu)


def _down_kernel(ts_ref, tc_ref, act_ref, wd_ref, bd_ref, y_ref, wdb_sc, abuf, obuf, g_ref, asem, osem,
                 *, tn, nj, n_e, n_tiles):
    def in_copy(t, slot):
        return pltpu.make_async_copy(act_ref.at[pl.ds(t * MOE_TILE, MOE_TILE), :], abuf.at[slot], asem.at[slot])

    def out_copy(jj, t, slot):
        return pltpu.make_async_copy(obuf.at[slot],
                                     y_ref.at[pl.ds(t * MOE_TILE, MOE_TILE), pl.ds(jj * tn, tn)], osem.at[slot])

    @pl.when(tc_ref[pl.program_id(1)] > 0)
    def _():
        wdb_sc[...] = wd_ref[...].astype(BF16)

    def compute(slot):
        obuf[slot] = jnp.dot(abuf[slot], wdb_sc[...], preferred_element_type=F32) + bd_ref[...]

    _stream_expert_tiles(ts_ref, tc_ref, g_ref, obuf, in_copy, out_copy, compute, nj=nj, n_e=n_e, n_tiles=n_tiles)


def _down(tile_start, tile_count, act, w_d, b_d, *, n_tiles, tn):
    n_e, f, d = w_d.shape
    grid_spec = pltpu.PrefetchScalarGridSpec(
        num_scalar_prefetch=2,
        grid=(d // tn, n_e),
        in_specs=[pl.BlockSpec(memory_space=pl.ANY),
                  pl.BlockSpec((None, f, tn), lambda j, e, ts, tc: (e, 0, j)),
                  pl.BlockSpec((None, 1, tn), lambda j, e, ts, tc: (e, 0, j))],
        out_specs=pl.BlockSpec(memory_space=pl.ANY),
        scratch_shapes=[pltpu.VMEM((f, tn), BF16),
                        pltpu.VMEM((2, MOE_TILE, f), BF16), pltpu.VMEM((2, MOE_TILE, tn), F32),
                        pltpu.SMEM((1,), jnp.int32),
                        pltpu.SemaphoreType.DMA((2,)), pltpu.SemaphoreType.DMA((2,))],
    )
    return pl.pallas_call(
        functools.partial(_down_kernel, tn=tn, nj=d // tn, n_e=n_e, n_tiles=n_tiles),
        grid_spec=grid_spec,
        out_shape=jax.ShapeDtypeStruct((n_tiles * MOE_TILE, d), F32),
        compiler_params=_cparams(2),
        name="moe_down",
    )(tile_start, tile_count, act, w_d, b_d)


def _combine_kernel(dest_ref, h_ref, gate_ref, g_ref, ys_ref, out_ref, buf, sem, *, tm, lp, row0, nst):
    s = pl.program_id(0)
    n_steps = pl.num_programs(0)

    def start_gather(step, slot):
        t0 = (step // nst) * lp + row0 + (step % nst) * tm

        def start_row(r, c):
            for k in range(TOP_K):
                d = dest_ref[(t0 + r) * TOP_K + k]
                pltpu.make_async_copy(ys_ref.at[pl.ds(d, 1), :], buf.at[slot, k, pl.ds(r, 1), :],
                                      sem.at[slot]).start(priority=k % 2)
            return c

        lax.fori_loop(0, tm, start_row, 0, unroll=8)

    @pl.when(s == 0)
    def _():
        start_gather(0, 0)

    @pl.when(s + 1 < n_steps)
    def _():
        start_gather(s + 1, (s + 1) % 2)

    slot = s % 2
    for k in range(TOP_K):
        pltpu.make_async_copy(ys_ref.at[pl.ds(0, tm), :], buf.at[slot, k], sem.at[slot]).wait()

    gate = gate_ref[...]
    acc = jnp.zeros(h_ref.shape, F32)
    for k in range(TOP_K):
        acc = acc + gate[:, k:k + 1] * buf[slot, k]
    h = h_ref[...] + acc
    ms = jnp.mean(h * h, axis=-1, keepdims=True)
    out_ref[...] = (h * lax.rsqrt(ms + EPS) * g_ref[...]).astype(out_ref.dtype)


def _combine(dest_flat, h2, gate, g, ys, *, batch, seq, lp, row0, tm):
    d = h2.shape[1]
    nb_b = lp // tm
    nb0 = row0 // tm
    nst = seq // tm
    blk = lambda s: (s // nst) * nb_b + nb0 + s % nst
    grid_spec = pltpu.PrefetchScalarGridSpec(
        num_scalar_prefetch=1,
        grid=(batch * nst,),
        in_specs=[pl.BlockSpec((tm, d), lambda s, dest: (blk(s), 0)),
                  pl.BlockSpec((tm, TOP_K), lambda s, dest: (blk(s), 0)),
                  pl.BlockSpec((1, d), lambda s, dest: (0, 0)),
                  pl.BlockSpec(memory_space=pl.ANY)],
        out_specs=pl.BlockSpec((None, tm, d), lambda s, dest: (s // nst, s % nst, 0)),
        scratch_shapes=[pltpu.VMEM((2, TOP_K, tm, d), F32), pltpu.SemaphoreType.DMA((2,))],
    )
    return pl.pallas_call(
        functools.partial(_combine_kernel, tm=tm, lp=lp, row0=row0, nst=nst),
        grid_spec=grid_spec,
        out_shape=jax.ShapeDtypeStruct((batch, seq, d), F32),
        compiler_params=_cparams(1),
        name="moe_combine",
    )(dest_flat, h2, gate, g.reshape(1, d).astype(F32), ys)


def _pick(pref, n):
    t = pref
    while n % t:
        t //= 2
    return t


def _row_tile(n, pref):
    t = pref // ROW_ALIGN * ROW_ALIGN
    while n % t:
        t -= ROW_ALIGN
    return t


def kernel(x, meta_tokens, norm_mix_g, w_in, b_gate, lambda_q1, lambda_k1, lambda_q2, lambda_k2, head_norm_g, w_attn_out, conv_w, conv_b, conv_ln_g, conv_ln_b, w_conv_out, b_conv_out, w_out, norm_ffn_g, w_router, b_router, w_gate_up, b_gate_up, w_down, b_down, final_norm_g):
    batch, seq, d = x.shape
    depth = w_in.shape[0]
    assert depth == 1 and seq % ROW_ALIGN == 0 and N_META <= ROW_ALIGN
    n_heads = d // 256
    hw = 2 * HEAD_DIM
    qk_w = n_heads * hw
    conv_ch = conv_w.shape[2]
    n_pad = ROW_ALIGN - N_META
    lp = n_pad + N_META + seq
    tp = batch * lp
    f = w_down.shape[2]
    layer = 0
    lam_init = 0.8 - 0.6 * math.exp(-0.3 * layer)

    h0, u = _embed_norm(x, meta_tokens, norm_mix_g[layer], n_pad=n_pad)

    proj = _matmul(u, w_in[layer], BF16, _row_tile(tp, 1536), _pick(512, w_in.shape[2]), name="in_proj")

    tq = 384 if lp % 384 == 0 else ROW_ALIGN
    vt = proj[:, 2 * qk_w:3 * qk_w].reshape(batch, lp, n_heads, hw).transpose(0, 2, 3, 1)
    o = _attention(proj, vt, lambda_q1[layer], lambda_k1[layer], lambda_q2[layer], lambda_k2[layer],
                   head_norm_g[layer], batch=batch, lp=lp, n_heads=n_heads, tq=tq, lam_init=lam_init,
                   n_pad=n_pad)
    ca_col = 3 * qk_w
    z = _conv_branch(proj, conv_w[layer], conv_b[layer], conv_ln_g[layer], conv_ln_b[layer],
                     ca_blk=ca_col // conv_ch, cg_blk=ca_col // conv_ch + 1, tm=_pick(256, tp))
    merged = _merge(o, z, proj, w_attn_out[layer], w_conv_out[layer],
                    b_conv_out[layer], b_gate[layer], gate_col=ca_col + 2 * conv_ch,
                    tm=_row_tile(tp, 768), tn=_pick(512, d))
    h2 = _matmul(merged, w_out[layer], F32, _row_tile(tp, 768), _pick(512, d), res=h0, name="out_proj")

    pos = np.arange(tp) % lp
    valid_np = pos >= n_pad
    valid = jnp.asarray(valid_np.astype(np.float32).reshape(1, tp))
    u_packed, top_e, gate_t, rank_t, cnt = _router(h2, norm_ffn_g[layer], w_router[layer], b_router[layer],
                                                   valid, tm=_pick(256, tp))
    counts = cnt[:, 0]
    padded = (counts + MOE_TILE - 1) // MOE_TILE * MOE_TILE
    e_ids = np.arange(N_EXPERTS)
    pad_end = jnp.sum(jnp.where(jnp.asarray(e_ids[None, :] <= e_ids[:, None]), padded[None, :], 0), axis=1)
    pad_start = pad_end - padded
    start_tok = jnp.sum(jnp.where(top_e[:, :, None] == jnp.asarray(e_ids, jnp.int32), pad_start, 0), axis=-1)
    n_real = int(valid_np.sum()) * TOP_K
    n_tiles = -(-(n_real + N_EXPERTS * (MOE_TILE - 1)) // MOE_TILE)
    n_slots = n_tiles * MOE_TILE
    dump = n_slots + (np.cumsum(~valid_np) - 1)[None, :] * TOP_K + np.arange(TOP_K)[:, None]
    dest_t = jnp.where(jnp.asarray(valid_np)[None, :], start_tok + rank_t, jnp.asarray(dump, jnp.int32))
    dest_flat = dest_t.T.reshape(-1).astype(jnp.int32)
    n_dump = int((~valid_np).sum()) * TOP_K
    n_used = (pad_end[-1] // MOE_TILE).astype(jnp.int32).reshape(1)
    tile_start = jnp.asarray(np.arange(n_tiles, dtype=np.int32) * MOE_TILE)
    tile_e = jnp.minimum(jnp.sum((pad_end[None, :] <= tile_start[:, None]).astype(jnp.int32), axis=1),
                         N_EXPERTS - 1).astype(jnp.int32)

    xs = _dispatch(dest_flat, tile_e, n_used, u_packed, n_slots + n_dump, tm=ROW_ALIGN, n_tiles=n_tiles)
    tile_first = (pad_start // MOE_TILE).astype(jnp.int32)
    tile_count = (padded // MOE_TILE).astype(jnp.int32)
    act = _gate_up(tile_first, tile_count, xs, w_gate_up[layer],
                   b_gate_up[layer].reshape(N_EXPERTS, 1, 2 * f).astype(F32), n_tiles=n_tiles, tf=_pick(512, f))
    ys = _down(tile_first, tile_count, act, w_down[layer],
               b_down[layer].reshape(N_EXPERTS, 1, d).astype(F32), n_tiles=n_tiles, tn=_pick(2048, d))
    return _combine(dest_flat, h2, gate_t.T, final_norm_g, ys, batch=batch, seq=seq, lp=lp,
                    row0=ROW_ALIGN, tm=ROW_ALIGN)
```

```python
import functools
import math

import numpy as np
import jax
import jax.numpy as jnp
from jax import lax
from jax.experimental import pallas as pl
from jax.experimental.pallas import tpu as pltpu

N_META = 16
HEAD_DIM = 64
N_EXPERTS = 32
TOP_K = 4
CONV_K = 31
EPS = 1e-5
SWIGLU_LIMIT = 7.0
SWIGLU_ALPHA = 1.702
SUBLANES = 8
LANES = 128
ROW_ALIGN = 128
CONV_HALO = 32
MOE_TILE = 256
ATTN_HEADS_PER_STEP = 4
VMEM_LIMIT = 56 * 1024 * 1024

F32 = jnp.float32
BF16 = jnp.bfloat16


def _cparams(n_axes, flags=None):
    return pltpu.CompilerParams(dimension_semantics=("arbitrary",) * n_axes,
                                vmem_limit_bytes=VMEM_LIMIT, flags=flags)


def _embed_norm_kernel(x_ref, meta_ref, g_ref, h_ref, u_ref, *, n_pad):
    i = pl.program_id(1)

    @pl.when(i == 0)
    def _():
        h_ref[0:n_pad, :] = jnp.zeros((n_pad, h_ref.shape[1]), F32)
        h_ref[n_pad:, :] = meta_ref[...]

    @pl.when(i > 0)
    def _():
        h_ref[...] = x_ref[...]

    h = h_ref[...]
    ms = jnp.mean(h * h, axis=-1, keepdims=True)
    u_ref[...] = (h * lax.rsqrt(ms + EPS) * g_ref[...]).astype(u_ref.dtype)


def _embed_norm(x, meta, g, *, n_pad):
    batch, seq, d = x.shape
    tm = ROW_ALIGN
    nb = (n_pad + N_META + seq) // tm
    out_spec = pl.BlockSpec((tm, d), lambda b, i: (b * nb + i, 0))
    return pl.pallas_call(
        functools.partial(_embed_norm_kernel, n_pad=n_pad),
        grid=(batch, nb),
        in_specs=[pl.BlockSpec((None, tm, d), lambda b, i: (b, jnp.maximum(i - 1, 0), 0)),
                  pl.BlockSpec((N_META, d), lambda b, i: (0, 0)),
                  pl.BlockSpec((1, d), lambda b, i: (0, 0))],
        out_specs=[out_spec, out_spec],
        out_shape=[jax.ShapeDtypeStruct((batch * nb * tm, d), F32),
                   jax.ShapeDtypeStruct((batch * nb * tm, d), BF16)],
        compiler_params=_cparams(2),
        name="embed_norm",
    )(x, meta.astype(F32), g.reshape(1, d).astype(F32))


def _cast_weight_once(w_ref, wb_sc):
    @pl.when(pl.program_id(1) == 0)
    def _():
        wb_sc[...] = w_ref[...].astype(BF16)


def _matmul_kernel(a_ref, w_ref, o_ref, wb_sc):
    _cast_weight_once(w_ref, wb_sc)
    o_ref[...] = jnp.dot(a_ref[...], wb_sc[...], preferred_element_type=F32).astype(o_ref.dtype)


def _matmul_res_kernel(a_ref, w_ref, r_ref, o_ref, wb_sc):
    _cast_weight_once(w_ref, wb_sc)
    acc = jnp.dot(a_ref[...], wb_sc[...], preferred_element_type=F32)
    o_ref[...] = (acc + r_ref[...]).astype(o_ref.dtype)


def _matmul(a, w, out_dtype, tm, tn, res=None, name="matmul"):
    m, k = a.shape
    n = w.shape[1]
    in_specs = [pl.BlockSpec((tm, k), lambda j, i: (i, 0)),
                pl.BlockSpec((k, tn), lambda j, i: (0, j))]
    args = [a, w]
    kern = _matmul_kernel
    if res is not None:
        in_specs.append(pl.BlockSpec((tm, tn), lambda j, i: (i, j)))
        args.append(res)
        kern = _matmul_res_kernel
    return pl.pallas_call(
        kern,
        grid=(n // tn, m // tm),
        in_specs=in_specs,
        out_specs=pl.BlockSpec((tm, tn), lambda j, i: (i, j)),
        out_shape=jax.ShapeDtypeStruct((m, n), out_dtype),
        scratch_shapes=[pltpu.VMEM((k, tn), BF16)],
        compiler_params=_cparams(2),
        name=name,
    )(*args)


def _attn_kernel(lq1_ref, lk1_ref, lq2_ref, lk2_ref, hg_ref, bias_ref, q_ref, k_ref, vt_ref, o_ref,
                 q12_sc, s_sc, m_sc, l_sc, acc_sc, *, tq, lam_init, n_hd):
    hw = 2 * HEAD_DIM
    nq = q_ref.shape[0] // tq
    lam = (jnp.exp(jnp.sum(lq1_ref[...] * lk1_ref[...], axis=-1, keepdims=True))
           - jnp.exp(jnp.sum(lq2_ref[...] * lk2_ref[...], axis=-1, keepdims=True)) + lam_init)

    def q_tile(qi, carry):
        q0 = pl.multiple_of(qi * tq, tq)
        for hd in range(n_hd):
            q = (q_ref[pl.ds(q0, tq), hd * hw:(hd + 1) * hw].astype(F32)
                 * (HEAD_DIM ** -0.5 * math.log2(math.e))).astype(BF16)
            lane = lax.broadcasted_iota(jnp.int32, q.shape, 1)
            zero = jnp.zeros_like(q)
            q12_sc[2 * hd] = jnp.where(lane < HEAD_DIM, q, zero)
            q12_sc[2 * hd + 1] = jnp.where(lane >= HEAD_DIM, q, zero)
        m_sc[...] = jnp.full(m_sc.shape, -jnp.inf, F32)
        l_sc[...] = jnp.zeros(l_sc.shape, F32)
        acc_sc[...] = jnp.zeros(acc_sc.shape, F32)

        def scores(kj, c):
            hd = c // 2
            k0 = pl.multiple_of(kj * tq, tq)
            s = lax.dot_general(k_ref[pl.ds(k0, tq), hd * hw:(hd + 1) * hw], q12_sc[c],
                                (((1,), (1,)), ((), ())), preferred_element_type=F32)
            kind = jnp.where(kj == 0, 1, 0) + jnp.where(kj == qi, 2, 0)
            s_sc[c] = s + bias_ref[kind]

        def update(kj, c):
            hd = c // 2
            k0 = pl.multiple_of(kj * tq, tq)
            s = s_sc[c]
            m_prev = m_sc[c]
            m_new = jnp.maximum(m_prev, jnp.max(s, axis=0, keepdims=True))
            alpha = jnp.exp2(m_prev - m_new)
            p = jnp.exp2(s - m_new)
            l_sc[c] = alpha * l_sc[c] + jnp.sum(p.reshape(tq // 8, 8, tq), axis=0)
            acc_sc[c] = alpha * acc_sc[c] + jnp.dot(vt_ref[hd, :, pl.ds(k0, tq)], p.astype(BF16),
                                                    preferred_element_type=F32)
            m_sc[c] = m_new

        for hd in range(n_hd):
            scores(0, 2 * hd)

        def body(kj, c):
            for hd in range(n_hd):
                scores(kj, 2 * hd + 1)
                update(kj, 2 * hd)
                scores(kj + 1, 2 * hd)
            for hd in range(n_hd):
                update(kj, 2 * hd + 1)
            return c

        lax.fori_loop(0, qi, body, 0)
        for hd in range(n_hd):
            scores(qi, 2 * hd + 1)
            update(qi, 2 * hd)
        for hd in range(n_hd):
            update(qi, 2 * hd + 1)

        for hd in range(n_hd):
            o1 = acc_sc[2 * hd] / jnp.sum(l_sc[2 * hd], axis=0, keepdims=True)
            o2 = acc_sc[2 * hd + 1] / jnp.sum(l_sc[2 * hd + 1], axis=0, keepdims=True)
            o = o1 - lam * o2
            ms = jnp.mean(o * o, axis=0, keepdims=True)
            o = o * lax.rsqrt(ms + EPS) * hg_ref[...] * (1.0 - lam_init)
            o_ref[pl.ds(q0, tq), hd * hw:(hd + 1) * hw] = o.T.astype(o_ref.dtype)
        return carry

    lax.fori_loop(0, nq, q_tile, 0)


def _attn_bias(tq, n_pad):
    neg = np.float32(np.finfo(np.float32).min)
    r = np.arange(tq)[:, None]
    c = np.arange(tq)[None, :]
    pad = np.broadcast_to(r < n_pad, (tq, tq))
    future = r > c
    tiles = [np.zeros((tq, tq), bool), pad, future, pad | future]
    return jnp.asarray(np.stack([np.where(t, neg, np.float32(0)) for t in tiles]).astype(np.float32))


def _attention(proj, vt, lq1, lk1, lq2, lk2, head_g, *, batch, lp, n_heads, tq, lam_init, n_pad):
    hw = 2 * HEAD_DIM
    n_hd = ATTN_HEADS_PER_STEP if n_heads % ATTN_HEADS_PER_STEP == 0 else 1
    assert n_pad <= tq
    koff = n_heads // n_hd
    vec = lambda a: a.reshape(1, -1).astype(F32)
    small = lambda n: pl.BlockSpec((1, n), lambda b, h: (0, 0))
    return pl.pallas_call(
        functools.partial(_attn_kernel, tq=tq, lam_init=lam_init, n_hd=n_hd),
        grid=(batch, n_heads // n_hd),
        in_specs=[small(HEAD_DIM), small(HEAD_DIM), small(HEAD_DIM), small(HEAD_DIM),
                  pl.BlockSpec((hw, 1), lambda b, h: (0, 0)),
                  pl.BlockSpec((4, tq, tq), lambda b, h: (0, 0, 0)),
                  pl.BlockSpec((lp, n_hd * hw), lambda b, h: (b, h)),
                  pl.BlockSpec((lp, n_hd * hw), lambda b, h: (b, koff + h)),
                  pl.BlockSpec((None, n_hd, hw, lp), lambda b, h: (b, h, 0, 0))],
        out_specs=pl.BlockSpec((lp, n_hd * hw), lambda b, h: (b, h)),
        out_shape=jax.ShapeDtypeStruct((batch * lp, n_heads * hw), BF16),
        scratch_shapes=[pltpu.VMEM((2 * n_hd, tq, hw), BF16),
                        pltpu.VMEM((2 * n_hd, tq, tq), F32),
                        pltpu.VMEM((2 * n_hd, 1, tq), F32), pltpu.VMEM((2 * n_hd, 8, tq), F32),
                        pltpu.VMEM((2 * n_hd, hw, tq), F32)],
        compiler_params=_cparams(2),
        name="diff_attention",
    )(vec(lq1), vec(lk1), vec(lq2), vec(lk2), head_g.reshape(hw, 1).astype(F32), _attn_bias(tq, n_pad),
      proj, proj, vt)


def _conv_kernel(ca_ref, cg_ref, ca_h_ref, cg_h_ref, w_ref, b_ref, lg_ref, lb_ref, z_ref,
                 ext_sc, sh_sc, y_sc, *, tm):
    i = pl.program_id(0)
    n_ch = w_ref.shape[1]
    n_ext = CONV_HALO + tm
    glu = lambda a, g: a.astype(F32) * jax.nn.sigmoid(g.astype(F32))
    halo = glu(ca_h_ref[...], cg_h_ref[...])
    ext_sc[0:CONV_HALO, :] = jnp.where(i > 0, halo, jnp.zeros_like(halo))
    ext_sc[CONV_HALO:n_ext, :] = glu(ca_ref[...], cg_ref[...])
    ext_sc[n_ext:n_ext + SUBLANES, :] = jnp.zeros((SUBLANES, n_ch), F32)
    base = CONV_HALO - (CONV_K - 1)

    def slab(lc, carry):
        l0 = pl.multiple_of(lc * LANES, LANES)
        for rho in range(SUBLANES):
            sh_sc[rho] = ext_sc[rho:rho + n_ext, pl.ds(l0, LANES)]
        acc = jnp.zeros((tm, LANES), F32) + b_ref[:, pl.ds(l0, LANES)]
        for j in range(CONV_K):
            rho = (base + j) % SUBLANES
            a = base + j - rho
            acc = acc + w_ref[j:j + 1, pl.ds(l0, LANES)] * sh_sc[rho, a:a + tm, :]
        y_sc[:, pl.ds(l0, LANES)] = acc
        return carry

    lax.fori_loop(0, n_ch // LANES, slab, 0)
    acc = y_sc[...]
    mu = jnp.mean(acc, axis=-1, keepdims=True)
    d = acc - mu
    var = jnp.mean(d * d, axis=-1, keepdims=True)
    y = d * lax.rsqrt(var + EPS) * lg_ref[...] + lb_ref[...]
    z_ref[...] = (y * jax.nn.sigmoid(y)).astype(z_ref.dtype)


def _conv_branch(proj, conv_w, conv_b, ln_g, ln_b, *, ca_blk, cg_blk, tm):
    m = proj.shape[0]
    c = conv_w.shape[1]
    hb = tm // CONV_HALO
    row = lambda a: a.reshape(1, c).astype(F32)
    vec = pl.BlockSpec((1, c), lambda i: (0, 0))
    return pl.pallas_call(
        functools.partial(_conv_kernel, tm=tm),
        grid=(m // tm,),
        in_specs=[pl.BlockSpec((tm, c), lambda i: (i, ca_blk)),
                  pl.BlockSpec((tm, c), lambda i: (i, cg_blk)),
                  pl.BlockSpec((CONV_HALO, c), lambda i: (jnp.maximum(i * hb - 1, 0), ca_blk)),
                  pl.BlockSpec((CONV_HALO, c), lambda i: (jnp.maximum(i * hb - 1, 0), cg_blk)),
                  pl.BlockSpec((CONV_K, c), lambda i: (0, 0)),
                  vec, vec, vec],
        out_specs=pl.BlockSpec((tm, c), lambda i: (i, 0)),
        out_shape=jax.ShapeDtypeStruct((m, c), BF16),
        scratch_shapes=[pltpu.VMEM((CONV_HALO + tm + SUBLANES, c), F32),
                        pltpu.VMEM((SUBLANES, CONV_HALO + tm, LANES), F32),
                        pltpu.VMEM((tm, c), F32)],
        compiler_params=_cparams(1),
        name="conformer_conv",
    )(proj, proj, proj, proj, conv_w.astype(F32), row(conv_b), row(ln_g), row(ln_b))


def _merge_kernel(o_ref, z_ref, wa_ref, wc_ref, bc_ref, g1_ref, g2_ref, bg1_ref, bg2_ref, out_ref,
                  wab_sc, wcb_sc):
    _cast_weight_once(wa_ref, wab_sc)
    _cast_weight_once(wc_ref, wcb_sc)
    ya = jnp.dot(o_ref[...], wab_sc[...], preferred_element_type=F32)
    yc = jnp.dot(z_ref[...], wcb_sc[...], preferred_element_type=F32) + bc_ref[...]
    g1 = jax.nn.sigmoid(g1_ref[...].astype(F32) + bg1_ref[...])
    g2 = jax.nn.sigmoid(g2_ref[...].astype(F32) + bg2_ref[...])
    out_ref[...] = (g1 * ya + g2 * yc).astype(out_ref.dtype)


def _merge(o, z, proj, wa, wc, bc, bg, *, gate_col, tm, tn):
    m, ka = o.shape
    kc = z.shape[1]
    d = wa.shape[1]
    g1_blk = gate_col // tn
    g2_blk = (gate_col + d) // tn
    nb = d // tn
    bg2 = bg.reshape(1, 2 * d).astype(F32)
    return pl.pallas_call(
        _merge_kernel,
        grid=(nb, m // tm),
        in_specs=[pl.BlockSpec((tm, ka), lambda j, i: (i, 0)),
                  pl.BlockSpec((tm, kc), lambda j, i: (i, 0)),
                  pl.BlockSpec((ka, tn), lambda j, i: (0, j)),
                  pl.BlockSpec((kc, tn), lambda j, i: (0, j)),
                  pl.BlockSpec((1, tn), lambda j, i: (0, j)),
                  pl.BlockSpec((tm, tn), lambda j, i: (i, g1_blk + j)),
                  pl.BlockSpec((tm, tn), lambda j, i: (i, g2_blk + j)),
                  pl.BlockSpec((1, tn), lambda j, i: (0, j)),
                  pl.BlockSpec((1, tn), lambda j, i: (0, nb + j))],
        out_specs=pl.BlockSpec((tm, tn), lambda j, i: (i, j)),
        out_shape=jax.ShapeDtypeStruct((m, d), BF16),
        scratch_shapes=[pltpu.VMEM((ka, tn), BF16), pltpu.VMEM((kc, tn), BF16)],
        compiler_params=_cparams(2),
        name="mixer_merge",
    )(o, z, wa, wc, bc.reshape(1, d).astype(F32), proj, proj, bg2, bg2)


def _router_kernel(h_ref, g_ref, wr_ref, br_ref, valid_ref,
                   up_ref, e_ref, gate_ref, rank_ref, cnt_ref, carry_sc, *, tm):
    i = pl.program_id(0)

    @pl.when(i == 0)
    def _():
        carry_sc[...] = jnp.zeros(carry_sc.shape, F32)

    h = h_ref[...]
    ms = jnp.mean(h * h, axis=-1, keepdims=True)
    u = h * lax.rsqrt(ms + EPS) * g_ref[...]

    half = u.shape[1] // 2
    bits = lax.bitcast_convert_type(u.astype(BF16).astype(F32), jnp.uint32)
    up_ref[...] = (bits[:, half:] & jnp.uint32(0xFFFF0000)) | (bits[:, :half] >> 16)

    logits = lax.dot_general(wr_ref[...], u, (((1,), (1,)), ((), ())),
                             precision=lax.Precision.HIGHEST,
                             preferred_element_type=F32) + br_ref[...]
    n_e = logits.shape[0]
    eiota = lax.broadcasted_iota(jnp.int32, logits.shape, 0).astype(F32)
    work = logits
    sel = jnp.zeros(logits.shape, jnp.bool_)
    top_l, top_e = [], []
    for _ in range(TOP_K):
        mx = jnp.max(work, axis=0, keepdims=True)
        idx = jnp.min(jnp.where(work == mx, eiota, float(n_e)), axis=0, keepdims=True)
        hit = eiota == idx
        top_l.append(mx)
        top_e.append(idx)
        sel = sel | hit
        work = jnp.where(hit, -jnp.inf, work)
    ex = [jnp.exp(t - top_l[0]) for t in top_l]
    den = ex[0] + ex[1] + ex[2] + ex[3]
    gate_ref[...] = jnp.concatenate([e / den for e in ex], axis=0)
    e_ref[...] = jnp.concatenate(top_e, axis=0).astype(jnp.int32)

    selv = jnp.where(sel & (valid_ref[...] > 0.0), 1.0, 0.0)
    before = (lax.broadcasted_iota(jnp.int32, (tm, tm), 0)
              < lax.broadcasted_iota(jnp.int32, (tm, tm), 1)).astype(BF16)
    rank_all = jnp.dot(selv.astype(BF16), before, preferred_element_type=F32) + carry_sc[...]
    ranks = [jnp.sum(jnp.where(eiota == idx, rank_all, 0.0), axis=0, keepdims=True) for idx in top_e]
    rank_ref[...] = jnp.concatenate(ranks, axis=0).astype(jnp.int32)
    carry = carry_sc[...] + jnp.sum(selv, axis=1, keepdims=True)
    carry_sc[...] = carry
    cnt_ref[...] = jnp.broadcast_to(carry, cnt_ref.shape).astype(jnp.int32)


def _router(h2, g, w_router, b_router, valid, *, tm):
    m, d = h2.shape
    n_e = w_router.shape[1]
    tok = lambda dt: jax.ShapeDtypeStruct((TOP_K, m), dt)
    tok_spec = pl.BlockSpec((TOP_K, tm), lambda i: (0, i))
    return pl.pallas_call(
        functools.partial(_router_kernel, tm=tm),
        grid=(m // tm,),
        in_specs=[pl.BlockSpec((tm, d), lambda i: (i, 0)),
                  pl.BlockSpec((1, d), lambda i: (0, 0)),
                  pl.BlockSpec((n_e, d), lambda i: (0, 0)),
                  pl.BlockSpec((n_e, 1), lambda i: (0, 0)),
                  pl.BlockSpec((1, tm), lambda i: (0, i))],
        out_specs=[pl.BlockSpec((tm, d // 2), lambda i: (i, 0)),
                   tok_spec, tok_spec, tok_spec,
                   pl.BlockSpec((n_e, 128), lambda i: (0, 0))],
        out_shape=[jax.ShapeDtypeStruct((m, d // 2), jnp.uint32),
                   tok(jnp.int32), tok(F32), tok(jnp.int32),
                   jax.ShapeDtypeStruct((n_e, 128), jnp.int32)],
        scratch_shapes=[pltpu.VMEM((n_e, 1), F32)],
        compiler_params=_cparams(1),
        name="router",
    )(h2, g.reshape(1, d).astype(F32), w_router.T.astype(F32), b_router.reshape(n_e, 1).astype(F32), valid)


def _dispatch_kernel(dest_ref, te_ref, nu_ref, u_ref, xs_ref, zero_sc, sem, zsem, *, tm, n_tiles):
    i = pl.program_id(0)

    @pl.when(i == 0)
    def _():
        zero_sc[...] = jnp.zeros(zero_sc.shape, zero_sc.dtype)
        nu = nu_ref[0]

        def partly_filled(t):
            nxt = te_ref[jnp.minimum(t + 1, n_tiles - 1)]
            return (t >= nu - 1) | (te_ref[t] != nxt)

        def tile_copy(t):
            return pltpu.make_async_copy(zero_sc, xs_ref.at[pl.ds(t * MOE_TILE, MOE_TILE), :], zsem)

        def start(t, c):
            @pl.when(partly_filled(t))
            def _():
                tile_copy(t).start()
            return c

        def wait(t, c):
            @pl.when(partly_filled(t))
            def _():
                tile_copy(t).wait()
            return c

        lax.fori_loop(0, n_tiles, start, 0)
        lax.fori_loop(0, n_tiles, wait, 0)

    def start_row(r, c):
        for k in range(TOP_K):
            d = dest_ref[(i * tm + r) * TOP_K + k]
            pltpu.make_async_copy(u_ref.at[pl.ds(r, 1), :], xs_ref.at[pl.ds(d, 1), :], sem).start(priority=k % 2)
        return c

    lax.fori_loop(0, tm, start_row, 0, unroll=8)
    for k in range(TOP_K):
        pltpu.make_async_copy(u_ref, xs_ref.at[pl.ds(0, tm), :], sem).wait()


def _dispatch(dest_flat, tile_e, n_used, u_packed, n_rows, *, tm, n_tiles):
    m, w = u_packed.shape
    grid_spec = pltpu.PrefetchScalarGridSpec(
        num_scalar_prefetch=3,
        grid=(m // tm,),
        in_specs=[pl.BlockSpec((tm, w), lambda i, dest, te, nu: (i, 0))],
        out_specs=pl.BlockSpec(memory_space=pl.ANY),
        scratch_shapes=[pltpu.VMEM((MOE_TILE, w), jnp.uint32),
                        pltpu.SemaphoreType.DMA(()), pltpu.SemaphoreType.DMA(())],
    )
    return pl.pallas_call(
        functools.partial(_dispatch_kernel, tm=tm, n_tiles=n_tiles),
        grid_spec=grid_spec,
        out_shape=jax.ShapeDtypeStruct((n_rows, w), jnp.uint32),
        compiler_params=_cparams(1),
        name="moe_dispatch",
    )(dest_flat, tile_e, n_used, u_packed)


def _stream_expert_tiles(ts_ref, tc_ref, g_ref, obuf, in_copy, out_copy, compute, *, nj, n_e, n_tiles):
    j = pl.program_id(0)
    e = pl.program_id(1)
    n_used = ts_ref[n_e - 1] + tc_ref[n_e - 1]
    t0 = ts_ref[e]

    @pl.when((j == 0) & (e == 0))
    def _():
        g_ref[0] = 0

        @pl.when(n_used > 0)
        def _():
            in_copy(0, 0).start()

    def body(i, c):
        g = g_ref[0]
        slot = g % 2
        t = t0 + i
        wraps = t + 1 >= n_used

        @pl.when(jnp.logical_not(wraps & (j == nj - 1)))
        def _():
            in_copy(jnp.where(wraps, 0, t + 1), 1 - slot).start()

        in_copy(t, slot).wait()

        @pl.when(g >= 2)
        def _():
            out_copy(j, t, slot).wait()

        compute(slot)
        out_copy(j, t, slot).start()
        g_ref[0] = g + 1
        return c

    lax.fori_loop(0, tc_ref[e], body, 0)

    @pl.when((j == nj - 1) & (e == n_e - 1))
    def _():
        g = g_ref[0]

        @pl.when(g >= 2)
        def _():
            out_copy(j, 0, g % 2).wait()

        @pl.when(g >= 1)
        def _():
            out_copy(j, 0, (g - 1) % 2).wait()

        obuf[0] = jnp.zeros(obuf.shape[1:], obuf.dtype)

        def zero_tile(t, c):
            for jj in range(nj):
                cp = out_copy(jj, t, 0)
                cp.start()
                cp.wait()
            return c

        lax.fori_loop(n_used, n_tiles, zero_tile, 0)


def _gate_up_kernel(ts_ref, tc_ref, xs_ref, wg_ref, wu_ref, bg_ref, bu_ref, act_ref,
                    wgb_sc, wub_sc, xbuf, obuf, g_ref, xsem, osem, *, tf, nj, n_e, n_tiles):
    def in_copy(t, slot):
        return pltpu.make_async_copy(xs_ref.at[pl.ds(t * MOE_TILE, MOE_TILE), :], xbuf.at[slot], xsem.at[slot])

    def out_copy(jj, t, slot):
        return pltpu.make_async_copy(obuf.at[slot],
                                     act_ref.at[pl.ds(t * MOE_TILE, MOE_TILE), pl.ds(jj * tf, tf)], osem.at[slot])

    @pl.when(tc_ref[pl.program_id(1)] > 0)
    def _():
        wgb_sc[...] = wg_ref[...].astype(BF16)
        wub_sc[...] = wu_ref[...].astype(BF16)

    def compute(slot):
        w = xbuf[slot]
        half = w.shape[1]
        lo = lax.bitcast_convert_type(w << 16, F32).astype(BF16)
        hi = lax.bitcast_convert_type(w & jnp.uint32(0xFFFF0000), F32).astype(BF16)
        g = (jnp.dot(lo, wgb_sc[:half, :], preferred_element_type=F32)
             + jnp.dot(hi, wgb_sc[half:, :], preferred_element_type=F32) + bg_ref[...])
        u = (jnp.dot(lo, wub_sc[:half, :], preferred_element_type=F32)
             + jnp.dot(hi, wub_sc[half:, :], preferred_element_type=F32) + bu_ref[...])
        g = jnp.minimum(g, SWIGLU_LIMIT)
        u = jnp.clip(u, -SWIGLU_LIMIT, SWIGLU_LIMIT)
        obuf[slot] = ((u + 1.0) * (g * jax.nn.sigmoid(SWIGLU_ALPHA * g))).astype(obuf.dtype)

    _stream_expert_tiles(ts_ref, tc_ref, g_ref, obuf, in_copy, out_copy, compute, nj=nj, n_e=n_e, n_tiles=n_tiles)


def _gate_up(tile_start, tile_count, xs, w_gu, b_gu, *, n_tiles, tf):
    n_e, d, f2 = w_gu.shape
    f = f2 // 2
    nj = f // tf
    grid_spec = pltpu.PrefetchScalarGridSpec(
        num_scalar_prefetch=2,
        grid=(nj, n_e),
        in_specs=[pl.BlockSpec(memory_space=pl.ANY),
                  pl.BlockSpec((None, d, tf), lambda j, e, ts, tc: (e, 0, j)),
                  pl.BlockSpec((None, d, tf), lambda j, e, ts, tc: (e, 0, nj + j)),
                  pl.BlockSpec((None, 1, tf), lambda j, e, ts, tc: (e, 0, j)),
                  pl.BlockSpec((None, 1, tf), lambda j, e, ts, tc: (e, 0, nj + j))],
        out_specs=pl.BlockSpec(memory_space=pl.ANY),
        scratch_shapes=[pltpu.VMEM((d, tf), BF16), pltpu.VMEM((d, tf), BF16),
                        pltpu.VMEM((2, MOE_TILE, d // 2), jnp.uint32), pltpu.VMEM((2, MOE_TILE, tf), BF16),
                        pltpu.SMEM((1,), jnp.int32),
                        pltpu.SemaphoreType.DMA((2,)), pltpu.SemaphoreType.DMA((2,))],
    )
    return pl.pallas_call(
        functools.partial(_gate_up_kernel, tf=tf, nj=nj, n_e=n_e, n_tiles=n_tiles),
        grid_spec=grid_spec,
        out_shape=jax.ShapeDtypeStruct((n_tiles * MOE_TILE, f), BF16),
        compiler_params=_cparams(2),
        name="moe_gate_up",
    )(tile_start, tile_count, xs, w_gu, w_gu, b_gu, b_gu)


def _down_kernel(ts_ref, tc_ref, act_ref, wd_ref, bd_ref, y_ref, wdb_sc, abuf, obuf, g_ref, asem, osem,
                 *, tn, nj, n_e, n_tiles):
    def in_copy(t, slot):
        return pltpu.make_async_copy(act_ref.at[pl.ds(t * MOE_TILE, MOE_TILE), :], abuf.at[slot], asem.at[slot])

    def out_copy(jj, t, slot):
        return pltpu.make_async_copy(obuf.at[slot],
                                     y_ref.at[pl.ds(t * MOE_TILE, MOE_TILE), pl.ds(jj * tn, tn)], osem.at[slot])

    @pl.when(tc_ref[pl.program_id(1)] > 0)
    def _():
        wdb_sc[...] = wd_ref[...].astype(BF16)

    def compute(slot):
        obuf[slot] = jnp.dot(abuf[slot], wdb_sc[...], preferred_element_type=F32) + bd_ref[...]

    _stream_expert_tiles(ts_ref, tc_ref, g_ref, obuf, in_copy, out_copy, compute, nj=nj, n_e=n_e, n_tiles=n_tiles)


def _down(tile_start, tile_count, act, w_d, b_d, *, n_tiles, tn):
    n_e, f, d = w_d.shape
    grid_spec = pltpu.PrefetchScalarGridSpec(
        num_scalar_prefetch=2,
        grid=(d // tn, n_e),
        in_specs=[pl.BlockSpec(memory_space=pl.ANY),
                  pl.BlockSpec((None, f, tn), lambda j, e, ts, tc: (e, 0, j)),
                  pl.BlockSpec((None, 1, tn), lambda j, e, ts, tc: (e, 0, j))],
        out_specs=pl.BlockSpec(memory_space=pl.ANY),
        scratch_shapes=[pltpu.VMEM((f, tn), BF16),
                        pltpu.VMEM((2, MOE_TILE, f), BF16), pltpu.VMEM((2, MOE_TILE, tn), F32),
                        pltpu.SMEM((1,), jnp.int32),
                        pltpu.SemaphoreType.DMA((2,)), pltpu.SemaphoreType.DMA((2,))],
    )
    return pl.pallas_call(
        functools.partial(_down_kernel, tn=tn, nj=d // tn, n_e=n_e, n_tiles=n_tiles),
        grid_spec=grid_spec,
        out_shape=jax.ShapeDtypeStruct((n_tiles * MOE_TILE, d), F32),
        compiler_params=_cparams(2),
        name="moe_down",
    )(tile_start, tile_count, act, w_d, b_d)


def _combine_kernel(dest_ref, h_ref, gate_ref, g_ref, ys_ref, out_ref, buf, sem, *, tm, lp, row0, nst):
    s = pl.program_id(0)
    n_steps = pl.num_programs(0)

    def start_gather(step, slot):
        t0 = (step // nst) * lp + row0 + (step % nst) * tm

        def start_row(r, c):
            for k in range(TOP_K):
                d = dest_ref[(t0 + r) * TOP_K + k]
                pltpu.make_async_copy(ys_ref.at[pl.ds(d, 1), :], buf.at[slot, k, pl.ds(r, 1), :],
                                      sem.at[slot]).start(priority=k % 2)
            return c

        lax.fori_loop(0, tm, start_row, 0, unroll=8)

    @pl.when(s == 0)
    def _():
        start_gather(0, 0)

    @pl.when(s + 1 < n_steps)
    def _():
        start_gather(s + 1, (s + 1) % 2)

    slot = s % 2
    for k in range(TOP_K):
        pltpu.make_async_copy(ys_ref.at[pl.ds(0, tm), :], buf.at[slot, k], sem.at[slot]).wait()

    gate = gate_ref[...]
    acc = jnp.zeros(h_ref.shape, F32)
    for k in range(TOP_K):
        acc = acc + gate[:, k:k + 1] * buf[slot, k]
    h = h_ref[...] + acc
    ms = jnp.mean(h * h, axis=-1, keepdims=True)
    out_ref[...] = (h * lax.rsqrt(ms + EPS) * g_ref[...]).astype(out_ref.dtype)


def _combine(dest_flat, h2, gate, g, ys, *, batch, seq, lp, row0, tm):
    d = h2.shape[1]
    nb_b = lp // tm
    nb0 = row0 // tm
    nst = seq // tm
    blk = lambda s: (s // nst) * nb_b + nb0 + s % nst
    grid_spec = pltpu.PrefetchScalarGridSpec(
        num_scalar_prefetch=1,
        grid=(batch * nst,),
        in_specs=[pl.BlockSpec((tm, d), lambda s, dest: (blk(s), 0)),
                  pl.BlockSpec((tm, TOP_K), lambda s, dest: (blk(s), 0)),
                  pl.BlockSpec((1, d), lambda s, dest: (0, 0)),
                  pl.BlockSpec(memory_space=pl.ANY)],
        out_specs=pl.BlockSpec((None, tm, d), lambda s, dest: (s // nst, s % nst, 0)),
        scratch_shapes=[pltpu.VMEM((2, TOP_K, tm, d), F32), pltpu.SemaphoreType.DMA((2,))],
    )
    return pl.pallas_call(
        functools.partial(_combine_kernel, tm=tm, lp=lp, row0=row0, nst=nst),
        grid_spec=grid_spec,
        out_shape=jax.ShapeDtypeStruct((batch, seq, d), F32),
        compiler_params=_cparams(1),
        name="moe_combine",
    )(dest_flat, h2, gate, g.reshape(1, d).astype(F32), ys)


def _pick(pref, n):
    t = pref
    while n % t:
        t //= 2
    return t


def _row_tile(n, pref):
    t = pref // ROW_ALIGN * ROW_ALIGN
    while n % t:
        t -= ROW_ALIGN
    return t


def kernel(x, meta_tokens, norm_mix_g, w_in, b_gate, lambda_q1, lambda_k1, lambda_q2, lambda_k2, head_norm_g, w_attn_out, conv_w, conv_b, conv_ln_g, conv_ln_b, w_conv_out, b_conv_out, w_out, norm_ffn_g, w_router, b_router, w_gate_up, b_gate_up, w_down, b_down, final_norm_g):
    batch, seq, d = x.shape
    depth = w_in.shape[0]
    assert depth == 1 and seq % ROW_ALIGN == 0 and N_META <= ROW_ALIGN
    n_heads = d // 256
    hw = 2 * HEAD_DIM
    qk_w = n_heads * hw
    conv_ch = conv_w.shape[2]
    n_pad = ROW_ALIGN - N_META
    lp = n_pad + N_META + seq
    tp = batch * lp
    f = w_down.shape[2]
    layer = 0
    lam_init = 0.8 - 0.6 * math.exp(-0.3 * layer)

    h0, u = _embed_norm(x, meta_tokens, norm_mix_g[layer], n_pad=n_pad)

    proj = _matmul(u, w_in[layer], BF16, _row_tile(tp, 1536), _pick(512, w_in.shape[2]), name="in_proj")

    tq = 384 if lp % 384 == 0 else ROW_ALIGN
    vt = proj[:, 2 * qk_w:3 * qk_w].reshape(batch, lp, n_heads, hw).transpose(0, 2, 3, 1)
    o = _attention(proj, vt, lambda_q1[layer], lambda_k1[layer], lambda_q2[layer], lambda_k2[layer],
                   head_norm_g[layer], batch=batch, lp=lp, n_heads=n_heads, tq=tq, lam_init=lam_init,
                   n_pad=n_pad)
    ca_col = 3 * qk_w
    z = _conv_branch(proj, conv_w[layer], conv_b[layer], conv_ln_g[layer], conv_ln_b[layer],
                     ca_blk=ca_col // conv_ch, cg_blk=ca_col // conv_ch + 1, tm=_pick(256, tp))
    merged = _merge(o, z, proj, w_attn_out[layer], w_conv_out[layer],
                    b_conv_out[layer], b_gate[layer], gate_col=ca_col + 2 * conv_ch,
                    tm=_row_tile(tp, 768), tn=_pick(512, d))
    h2 = _matmul(merged, w_out[layer], F32, _row_tile(tp, 768), _pick(512, d), res=h0, name="out_proj")

    pos = np.arange(tp) % lp
    valid_np = pos >= n_pad
    valid = jnp.asarray(valid_np.astype(np.float32).reshape(1, tp))
    u_packed, top_e, gate_t, rank_t, cnt = _router(h2, norm_ffn_g[layer], w_router[layer], b_router[layer],
                                                   valid, tm=_pick(256, tp))
    counts = cnt[:, 0]
    padded = (counts + MOE_TILE - 1) // MOE_TILE * MOE_TILE
    e_ids = np.arange(N_EXPERTS)
    pad_end = jnp.sum(jnp.where(jnp.asarray(e_ids[None, :] <= e_ids[:, None]), padded[None, :], 0), axis=1)
    pad_start = pad_end - padded
    start_tok = jnp.sum(jnp.where(top_e[:, :, None] == jnp.asarray(e_ids, jnp.int32), pad_start, 0), axis=-1)
    n_real = int(valid_np.sum()) * TOP_K
    n_tiles = -(-(n_real + N_EXPERTS * (MOE_TILE - 1)) // MOE_TILE)
    n_slots = n_tiles * MOE_TILE
    dump = n_slots + (np.cumsum(~valid_np) - 1)[None, :] * TOP_K + np.arange(TOP_K)[:, None]
    dest_t = jnp.where(jnp.asarray(valid_np)[None, :], start_tok + rank_t, jnp.asarray(dump, jnp.int32))
    dest_flat = dest_t.T.reshape(-1).astype(jnp.int32)
    n_dump = int((~valid_np).sum()) * TOP_K
    n_used = (pad_end[-1] // MOE_TILE).astype(jnp.int32).reshape(1)
    tile_start = jnp.asarray(np.arange(n_tiles, dtype=np.int32) * MOE_TILE)
    tile_e = jnp.minimum(jnp.sum((pad_end[None, :] <= tile_start[:, None]).astype(jnp.int32), axis=1),
                         N_EXPERTS - 1).astype(jnp.int32)

    xs = _dispatch(dest_flat, tile_e, n_used, u_packed, n_slots + n_dump, tm=ROW_ALIGN, n_tiles=n_tiles)
    tile_first = (pad_start // MOE_TILE).astype(jnp.int32)
    tile_count = (padded // MOE_TILE).astype(jnp.int32)
    act = _gate_up(tile_first, tile_count, xs, w_gate_up[layer],
                   b_gate_up[layer].reshape(N_EXPERTS, 1, 2 * f).astype(F32), n_tiles=n_tiles, tf=_pick(512, f))
    ys = _down(tile_first, tile_count, act, w_down[layer],
               b_down[layer].reshape(N_EXPERTS, 1, d).astype(F32), n_tiles=n_tiles, tn=_pick(2048, d))
    return _combine(dest_flat, h2, gate_t.T, final_norm_g, ys, batch=batch, seq=seq, lp=lp,
                    row0=ROW_ALIGN, tm=ROW_ALIGN)
```

```python
import functools
import math

import numpy as np
import jax
import jax.numpy as jnp
from jax import lax
from jax.experimental import pallas as pl
from jax.experimental.pallas import tpu as pltpu

N_META = 16
HEAD_DIM = 64
N_EXPERTS = 32
TOP_K = 4
CONV_K = 31
EPS = 1e-5
SWIGLU_LIMIT = 7.0
SWIGLU_ALPHA = 1.702
SUBLANES = 8
LANES = 128
ROW_ALIGN = 128
CONV_HALO = 32
MOE_TILE = 256
ATTN_HEADS_PER_STEP = 4
VMEM_LIMIT = 56 * 1024 * 1024

F32 = jnp.float32
BF16 = jnp.bfloat16


def _cparams(n_axes, flags=None):
    return pltpu.CompilerParams(dimension_semantics=("arbitrary",) * n_axes,
                                vmem_limit_bytes=VMEM_LIMIT, flags=flags)


def _embed_norm_kernel(x_ref, meta_ref, g_ref, h_ref, u_ref, *, n_pad):
    i = pl.program_id(1)

    @pl.when(i == 0)
    def _():
        h_ref[0:n_pad, :] = jnp.zeros((n_pad, h_ref.shape[1]), F32)
        h_ref[n_pad:, :] = meta_ref[...]

    @pl.when(i > 0)
    def _():
        h_ref[...] = x_ref[...]

    h = h_ref[...]
    ms = jnp.mean(h * h, axis=-1, keepdims=True)
    u_ref[...] = (h * lax.rsqrt(ms + EPS) * g_ref[...]).astype(u_ref.dtype)


def _embed_norm(x, meta, g, *, n_pad):
    batch, seq, d = x.shape
    tm = ROW_ALIGN
    nb = (n_pad + N_META + seq) // tm
    out_spec = pl.BlockSpec((tm, d), lambda b, i: (b * nb + i, 0))
    return pl.pallas_call(
        functools.partial(_embed_norm_kernel, n_pad=n_pad),
        grid=(batch, nb),
        in_specs=[pl.BlockSpec((None, tm, d), lambda b, i: (b, jnp.maximum(i - 1, 0), 0)),
                  pl.BlockSpec((N_META, d), lambda b, i: (0, 0)),
                  pl.BlockSpec((1, d), lambda b, i: (0, 0))],
        out_specs=[out_spec, out_spec],
        out_shape=[jax.ShapeDtypeStruct((batch * nb * tm, d), F32),
                   jax.ShapeDtypeStruct((batch * nb * tm, d), BF16)],
        compiler_params=_cparams(2),
        name="embed_norm",
    )(x, meta.astype(F32), g.reshape(1, d).astype(F32))


def _cast_weight_once(w_ref, wb_sc):
    @pl.when(pl.program_id(1) == 0)
    def _():
        wb_sc[...] = w_ref[...].astype(BF16)


def _matmul_kernel(a_ref, w_ref, o_ref, wb_sc):
    _cast_weight_once(w_ref, wb_sc)
    o_ref[...] = jnp.dot(a_ref[...], wb_sc[...], preferred_element_type=F32).astype(o_ref.dtype)


def _matmul_res_kernel(a_ref, w_ref, r_ref, o_ref, wb_sc):
    _cast_weight_once(w_ref, wb_sc)
    acc = jnp.dot(a_ref[...], wb_sc[...], preferred_element_type=F32)
    o_ref[...] = (acc + r_ref[...]).astype(o_ref.dtype)


def _matmul(a, w, out_dtype, tm, tn, res=None, name="matmul"):
    m, k = a.shape
    n = w.shape[1]
    in_specs = [pl.BlockSpec((tm, k), lambda j, i: (i, 0)),
                pl.BlockSpec((k, tn), lambda j, i: (0, j))]
    args = [a, w]
    kern = _matmul_kernel
    if res is not None:
        in_specs.append(pl.BlockSpec((tm, tn), lambda j, i: (i, j)))
        args.append(res)
        kern = _matmul_res_kernel
    return pl.pallas_call(
        kern,
        grid=(n // tn, m // tm),
        in_specs=in_specs,
        out_specs=pl.BlockSpec((tm, tn), lambda j, i: (i, j)),
        out_shape=jax.ShapeDtypeStruct((m, n), out_dtype),
        scratch_shapes=[pltpu.VMEM((k, tn), BF16)],
        compiler_params=_cparams(2),
        name=name,
    )(*args)


def _attn_kernel(lq1_ref, lk1_ref, lq2_ref, lk2_ref, hg_ref, bias_ref, q_ref, k_ref, vt_ref, o_ref,
                 q12_sc, s_sc, m_sc, l_sc, acc_sc, *, tq, lam_init, n_hd):
    hw = 2 * HEAD_DIM
    nq = q_ref.shape[0] // tq
    lam = (jnp.exp(jnp.sum(lq1_ref[...] * lk1_ref[...], axis=-1, keepdims=True))
           - jnp.exp(jnp.sum(lq2_ref[...] * lk2_ref[...], axis=-1, keepdims=True)) + lam_init)

    def q_tile(qi, carry):
        q0 = pl.multiple_of(qi * tq, tq)
        for hd in range(n_hd):
            q = (q_ref[pl.ds(q0, tq), hd * hw:(hd + 1) * hw].astype(F32)
                 * (HEAD_DIM ** -0.5 * math.log2(math.e))).astype(BF16)
            lane = lax.broadcasted_iota(jnp.int32, q.shape, 1)
            zero = jnp.zeros_like(q)
            q12_sc[2 * hd] = jnp.where(lane < HEAD_DIM, q, zero)
            q12_sc[2 * hd + 1] = jnp.where(lane >= HEAD_DIM, q, zero)
        m_sc[...] = jnp.full(m_sc.shape, -jnp.inf, F32)
        l_sc[...] = jnp.zeros(l_sc.shape, F32)
        acc_sc[...] = jnp.zeros(acc_sc.shape, F32)

        def scores(kj, c):
            hd = c // 2
            k0 = pl.multiple_of(kj * tq, tq)
            s = lax.dot_general(k_ref[pl.ds(k0, tq), hd * hw:(hd + 1) * hw], q12_sc[c],
                                (((1,), (1,)), ((), ())), preferred_element_type=F32)
            kind = jnp.where(kj == 0, 1, 0) + jnp.where(kj == qi, 2, 0)
            s_sc[c] = s + bias_ref[kind]

        def update(kj, c):
            hd = c // 2
            k0 = pl.multiple_of(kj * tq, tq)
            s = s_sc[c]
            m_prev = m_sc[c]
            m_new = jnp.maximum(m_prev, jnp.max(s, axis=0, keepdims=True))
            alpha = jnp.exp2(m_prev - m_new)
            p = jnp.exp2(s - m_new)
            l_sc[c] = alpha * l_sc[c] + jnp.sum(p.reshape(tq // 8, 8, tq), axis=0)
            acc_sc[c] = alpha * acc_sc[c] + jnp.dot(vt_ref[hd, :, pl.ds(k0, tq)], p.astype(BF16),
                                                    preferred_element_type=F32)
            m_sc[c] = m_new

        for hd in range(n_hd):
            scores(0, 2 * hd)

        def body(kj, c):
            for hd in range(n_hd):
                scores(kj, 2 * hd + 1)
                update(kj, 2 * hd)
                scores(kj + 1, 2 * hd)
            for hd in range(n_hd):
                update(kj, 2 * hd + 1)
            return c

        lax.fori_loop(0, qi, body, 0)
        for hd in range(n_hd):
            scores(qi, 2 * hd + 1)
            update(qi, 2 * hd)
        for hd in range(n_hd):
            update(qi, 2 * hd + 1)

        for hd in range(n_hd):
            o1 = acc_sc[2 * hd] / jnp.sum(l_sc[2 * hd], axis=0, keepdims=True)
            o2 = acc_sc[2 * hd + 1] / jnp.sum(l_sc[2 * hd + 1], axis=0, keepdims=True)
            o = o1 - lam * o2
            ms = jnp.mean(o * o, axis=0, keepdims=True)
            o = o * lax.rsqrt(ms + EPS) * hg_ref[...] * (1.0 - lam_init)
            o_ref[pl.ds(q0, tq), hd * hw:(hd + 1) * hw] = o.T.astype(o_ref.dtype)
        return carry

    lax.fori_loop(0, nq, q_tile, 0)


def _attn_bias(tq, n_pad):
    neg = np.float32(np.finfo(np.float32).min)
    r = np.arange(tq)[:, None]
    c = np.arange(tq)[None, :]
    pad = np.broadcast_to(r < n_pad, (tq, tq))
    future = r > c
    tiles = [np.zeros((tq, tq), bool), pad, future, pad | future]
    return jnp.asarray(np.stack([np.where(t, neg, np.float32(0)) for t in tiles]).astype(np.float32))


def _attention(proj, vt, lq1, lk1, lq2, lk2, head_g, *, batch, lp, n_heads, tq, lam_init, n_pad):
    hw = 2 * HEAD_DIM
    n_hd = ATTN_HEADS_PER_STEP if n_heads % ATTN_HEADS_PER_STEP == 0 else 1
    assert n_pad <= tq
    koff = n_heads // n_hd
    vec = lambda a: a.reshape(1, -1).astype(F32)
    small = lambda n: pl.BlockSpec((1, n), lambda b, h: (0, 0))
    return pl.pallas_call(
        functools.partial(_attn_kernel, tq=tq, lam_init=lam_init, n_hd=n_hd),
        grid=(batch, n_heads // n_hd),
        in_specs=[small(HEAD_DIM), small(HEAD_DIM), small(HEAD_DIM), small(HEAD_DIM),
                  pl.BlockSpec((hw, 1), lambda b, h: (0, 0)),
                  pl.BlockSpec((4, tq, tq), lambda b, h: (0, 0, 0)),
                  pl.BlockSpec((lp, n_hd * hw), lambda b, h: (b, h)),
                  pl.BlockSpec((lp, n_hd * hw), lambda b, h: (b, koff + h)),
                  pl.BlockSpec((None, n_hd, hw, lp), lambda b, h: (b, h, 0, 0))],
        out_specs=pl.BlockSpec((lp, n_hd * hw), lambda b, h: (b, h)),
        out_shape=jax.ShapeDtypeStruct((batch * lp, n_heads * hw), BF16),
        scratch_shapes=[pltpu.VMEM((2 * n_hd, tq, hw), BF16),
                        pltpu.VMEM((2 * n_hd, tq, tq), F32),
                        pltpu.VMEM((2 * n_hd, 1, tq), F32), pltpu.VMEM((2 * n_hd, 8, tq), F32),
                        pltpu.VMEM((2 * n_hd, hw, tq), F32)],
        compiler_params=_cparams(2),
        name="diff_attention",
    )(vec(lq1), vec(lk1), vec(lq2), vec(lk2), head_g.reshape(hw, 1).astype(F32), _attn_bias(tq, n_pad),
      proj, proj, vt)


def _conv_kernel(ca_ref, cg_ref, ca_h_ref, cg_h_ref, w_ref, b_ref, lg_ref, lb_ref, z_ref,
                 ext_sc, sh_sc, y_sc, *, tm):
    i = pl.program_id(0)
    n_ch = w_ref.shape[1]
    n_ext = CONV_HALO + tm
    glu = lambda a, g: a.astype(F32) * jax.nn.sigmoid(g.astype(F32))
    halo = glu(ca_h_ref[...], cg_h_ref[...])
    ext_sc[0:CONV_HALO, :] = jnp.where(i > 0, halo, jnp.zeros_like(halo))
    ext_sc[CONV_HALO:n_ext, :] = glu(ca_ref[...], cg_ref[...])
    ext_sc[n_ext:n_ext + SUBLANES, :] = jnp.zeros((SUBLANES, n_ch), F32)
    base = CONV_HALO - (CONV_K - 1)

    def slab(lc, carry):
        l0 = pl.multiple_of(lc * LANES, LANES)
        for rho in range(SUBLANES):
            sh_sc[rho] = ext_sc[rho:rho + n_ext, pl.ds(l0, LANES)]
        acc = jnp.zeros((tm, LANES), F32) + b_ref[:, pl.ds(l0, LANES)]
        for j in range(CONV_K):
            rho = (base + j) % SUBLANES
            a = base + j - rho
            acc = acc + w_ref[j:j + 1, pl.ds(l0, LANES)] * sh_sc[rho, a:a + tm, :]
        y_sc[:, pl.ds(l0, LANES)] = acc
        return carry

    lax.fori_loop(0, n_ch // LANES, slab, 0)
    acc = y_sc[...]
    mu = jnp.mean(acc, axis=-1, keepdims=True)
    d = acc - mu
    var = jnp.mean(d * d, axis=-1, keepdims=True)
    y = d * lax.rsqrt(var + EPS) * lg_ref[...] + lb_ref[...]
    z_ref[...] = (y * jax.nn.sigmoid(y)).astype(z_ref.dtype)


def _conv_branch(proj, conv_w, conv_b, ln_g, ln_b, *, ca_blk, cg_blk, tm):
    m = proj.shape[0]
    c = conv_w.shape[1]
    hb = tm // CONV_HALO
    row = lambda a: a.reshape(1, c).astype(F32)
    vec = pl.BlockSpec((1, c), lambda i: (0, 0))
    return pl.pallas_call(
        functools.partial(_conv_kernel, tm=tm),
        grid=(m // tm,),
        in_specs=[pl.BlockSpec((tm, c), lambda i: (i, ca_blk)),
                  pl.BlockSpec((tm, c), lambda i: (i, cg_blk)),
                  pl.BlockSpec((CONV_HALO, c), lambda i: (jnp.maximum(i * hb - 1, 0), ca_blk)),
                  pl.BlockSpec((CONV_HALO, c), lambda i: (jnp.maximum(i * hb - 1, 0), cg_blk)),
                  pl.BlockSpec((CONV_K, c), lambda i: (0, 0)),
                  vec, vec, vec],
        out_specs=pl.BlockSpec((tm, c), lambda i: (i, 0)),
        out_shape=jax.ShapeDtypeStruct((m, c), BF16),
        scratch_shapes=[pltpu.VMEM((CONV_HALO + tm + SUBLANES, c), F32),
                        pltpu.VMEM((SUBLANES, CONV_HALO + tm, LANES), F32),
                        pltpu.VMEM((tm, c), F32)],
        compiler_params=_cparams(1),
        name="conformer_conv",
    )(proj, proj, proj, proj, conv_w.astype(F32), row(conv_b), row(ln_g), row(ln_b))


def _merge_kernel(o_ref, z_ref, wa_ref, wc_ref, bc_ref, g1_ref, g2_ref, bg1_ref, bg2_ref, out_ref,
                  wab_sc, wcb_sc):
    _cast_weight_once(wa_ref, wab_sc)
    _cast_weight_once(wc_ref, wcb_sc)
    ya = jnp.dot(o_ref[...], wab_sc[...], preferred_element_type=F32)
    yc = jnp.dot(z_ref[...], wcb_sc[...], preferred_element_type=F32) + bc_ref[...]
    g1 = jax.nn.sigmoid(g1_ref[...].astype(F32) + bg1_ref[...])
    g2 = jax.nn.sigmoid(g2_ref[...].astype(F32) + bg2_ref[...])
    out_ref[...] = (g1 * ya + g2 * yc).astype(out_ref.dtype)


def _merge(o, z, proj, wa, wc, bc, bg, *, gate_col, tm, tn):
    m, ka = o.shape
    kc = z.shape[1]
    d = wa.shape[1]
    g1_blk = gate_col // tn
    g2_blk = (gate_col + d) // tn
    nb = d // tn
    bg2 = bg.reshape(1, 2 * d).astype(F32)
    return pl.pallas_call(
        _merge_kernel,
        grid=(nb, m // tm),
        in_specs=[pl.BlockSpec((tm, ka), lambda j, i: (i, 0)),
                  pl.BlockSpec((tm, kc), lambda j, i: (i, 0)),
                  pl.BlockSpec((ka, tn), lambda j, i: (0, j)),
                  pl.BlockSpec((kc, tn), lambda j, i: (0, j)),
                  pl.BlockSpec((1, tn), lambda j, i: (0, j)),
                  pl.BlockSpec((tm, tn), lambda j, i: (i, g1_blk + j)),
                  pl.BlockSpec((tm, tn), lambda j, i: (i, g2_blk + j)),
                  pl.BlockSpec((1, tn), lambda j, i: (0, j)),
                  pl.BlockSpec((1, tn), lambda j, i: (0, nb + j))],
        out_specs=pl.BlockSpec((tm, tn), lambda j, i: (i, j)),
        out_shape=jax.ShapeDtypeStruct((m, d), BF16),
        scratch_shapes=[pltpu.VMEM((ka, tn), BF16), pltpu.VMEM((kc, tn), BF16)],
        compiler_params=_cparams(2),
        name="mixer_merge",
    )(o, z, wa, wc, bc.reshape(1, d).astype(F32), proj, proj, bg2, bg2)


def _router_kernel(h_ref, g_ref, wr_ref, br_ref, valid_ref,
                   up_ref, e_ref, gate_ref, rank_ref, cnt_ref, carry_sc, *, tm):
    i = pl.program_id(0)

    @pl.when(i == 0)
    def _():
        carry_sc[...] = jnp.zeros(carry_sc.shape, F32)

    h = h_ref[...]
    ms = jnp.mean(h * h, axis=-1, keepdims=True)
    u = h * lax.rsqrt(ms + EPS) * g_ref[...]

    half = u.shape[1] // 2
    bits = lax.bitcast_convert_type(u.astype(BF16).astype(F32), jnp.uint32)
    up_ref[...] = (bits[:, half:] & jnp.uint32(0xFFFF0000)) | (bits[:, :half] >> 16)

    logits = lax.dot_general(wr_ref[...], u, (((1,), (1,)), ((), ())),
                             precision=lax.Precision.HIGHEST,
                             preferred_element_type=F32) + br_ref[...]
    n_e = logits.shape[0]
    eiota = lax.broadcasted_iota(jnp.int32, logits.shape, 0).astype(F32)
    work = logits
    sel = jnp.zeros(logits.shape, jnp.bool_)
    top_l, top_e = [], []
    for _ in range(TOP_K):
        mx = jnp.max(work, axis=0, keepdims=True)
        idx = jnp.min(jnp.where(work == mx, eiota, float(n_e)), axis=0, keepdims=True)
        hit = eiota == idx
        top_l.append(mx)
        top_e.append(idx)
        sel = sel | hit
        work = jnp.where(hit, -jnp.inf, work)
    ex = [jnp.exp(t - top_l[0]) for t in top_l]
    den = ex[0] + ex[1] + ex[2] + ex[3]
    gate_ref[...] = jnp.concatenate([e / den for e in ex], axis=0)
    e_ref[...] = jnp.concatenate(top_e, axis=0).astype(jnp.int32)

    selv = jnp.where(sel & (valid_ref[...] > 0.0), 1.0, 0.0)
    before = (lax.broadcasted_iota(jnp.int32, (tm, tm), 0)
              < lax.broadcasted_iota(jnp.int32, (tm, tm), 1)).astype(BF16)
    rank_all = jnp.dot(selv.astype(BF16), before, preferred_element_type=F32) + carry_sc[...]
    ranks = [jnp.sum(jnp.where(eiota == idx, rank_all, 0.0), axis=0, keepdims=True) for idx in top_e]
    rank_ref[...] = jnp.concatenate(ranks, axis=0).astype(jnp.int32)
    carry = carry_sc[...] + jnp.sum(selv, axis=1, keepdims=True)
    carry_sc[...] = carry
    cnt_ref[...] = jnp.broadcast_to(carry, cnt_ref.shape).astype(jnp.int32)


def _router(h2, g, w_router, b_router, valid, *, tm):
    m, d = h2.shape
    n_e = w_router.shape[1]
    tok = lambda dt: jax.ShapeDtypeStruct((TOP_K, m), dt)
    tok_spec = pl.BlockSpec((TOP_K, tm), lambda i: (0, i))
    return pl.pallas_call(
        functools.partial(_router_kernel, tm=tm),
        grid=(m // tm,),
        in_specs=[pl.BlockSpec((tm, d), lambda i: (i, 0)),
                  pl.BlockSpec((1, d), lambda i: (0, 0)),
                  pl.BlockSpec((n_e, d), lambda i: (0, 0)),
                  pl.BlockSpec((n_e, 1), lambda i: (0, 0)),
                  pl.BlockSpec((1, tm), lambda i: (0, i))],
        out_specs=[pl.BlockSpec((tm, d // 2), lambda i: (i, 0)),
                   tok_spec, tok_spec, tok_spec,
                   pl.BlockSpec((n_e, 128), lambda i: (0, 0))],
        out_shape=[jax.ShapeDtypeStruct((m, d // 2), jnp.uint32),
                   tok(jnp.int32), tok(F32), tok(jnp.int32),
                   jax.ShapeDtypeStruct((n_e, 128), jnp.int32)],
        scratch_shapes=[pltpu.VMEM((n_e, 1), F32)],
        compiler_params=_cparams(1),
        name="router",
    )(h2, g.reshape(1, d).astype(F32), w_router.T.astype(F32), b_router.reshape(n_e, 1).astype(F32), valid)


def _dispatch_kernel(dest_ref, te_ref, nu_ref, u_ref, xs_ref, zero_sc, sem, zsem, *, tm, n_tiles):
    i = pl.program_id(0)

    @pl.when(i == 0)
    def _():
        zero_sc[...] = jnp.zeros(zero_sc.shape, zero_sc.dtype)
        nu = nu_ref[0]

        def partly_filled(t):
            nxt = te_ref[jnp.minimum(t + 1, n_tiles - 1)]
            return (t >= nu - 1) | (te_ref[t] != nxt)

        def tile_copy(t):
            return pltpu.make_async_copy(zero_sc, xs_ref.at[pl.ds(t * MOE_TILE, MOE_TILE), :], zsem)

        def start(t, c):
            @pl.when(partly_filled(t))
            def _():
                tile_copy(t).start()
            return c

        def wait(t, c):
            @pl.when(partly_filled(t))
            def _():
                tile_copy(t).wait()
            return c

        lax.fori_loop(0, n_tiles, start, 0)
        lax.fori_loop(0, n_tiles, wait, 0)

    def start_row(r, c):
        for k in range(TOP_K):
            d = dest_ref[(i * tm + r) * TOP_K + k]
            pltpu.make_async_copy(u_ref.at[pl.ds(r, 1), :], xs_ref.at[pl.ds(d, 1), :], sem).start(priority=k % 2)
        return c

    lax.fori_loop(0, tm, start_row, 0, unroll=8)
    for k in range(TOP_K):
        pltpu.make_async_copy(u_ref, xs_ref.at[pl.ds(0, tm), :], sem).wait()


def _dispatch(dest_flat, tile_e, n_used, u_packed, n_rows, *, tm, n_tiles):
    m, w = u_packed.shape
    grid_spec = pltpu.PrefetchScalarGridSpec(
        num_scalar_prefetch=3,
        grid=(m // tm,),
        in_specs=[pl.BlockSpec((tm, w), lambda i, dest, te, nu: (i, 0))],
        out_specs=pl.BlockSpec(memory_space=pl.ANY),
        scratch_shapes=[pltpu.VMEM((MOE_TILE, w), jnp.uint32),
                        pltpu.SemaphoreType.DMA(()), pltpu.SemaphoreType.DMA(())],
    )
    return pl.pallas_call(
        functools.partial(_dispatch_kernel, tm=tm, n_tiles=n_tiles),
        grid_spec=grid_spec,
        out_shape=jax.ShapeDtypeStruct((n_rows, w), jnp.uint32),
        compiler_params=_cparams(1),
        name="moe_dispatch",
    )(dest_flat, tile_e, n_used, u_packed)


def _stream_expert_tiles(ts_ref, tc_ref, g_ref, obuf, in_copy, out_copy, compute, *, nj, n_e, n_tiles):
    j = pl.program_id(0)
    e = pl.program_id(1)
    n_used = ts_ref[n_e - 1] + tc_ref[n_e - 1]
    t0 = ts_ref[e]

    @pl.when((j == 0) & (e == 0))
    def _():
        g_ref[0] = 0
        in_copy(0, 0).start()
        in_copy(1, 1).start()

    def body(i, c):
        g = g_ref[0]
        t = t0 + i
        t2 = t + 2
        wraps = t2 >= n_used
        t2 = jnp.where(wraps, t2 - n_used, t2)

        @pl.when(jnp.logical_not(wraps & (j == nj - 1)))
        def _():
            in_copy(t2, (g + 2) % 3).start()

        in_copy(t, g % 3).wait()

        @pl.when(g >= 2)
        def _():
            out_copy(j, t, g % 2).wait()

        compute(g % 3, g % 2)
        out_copy(j, t, g % 2).start()
        g_ref[0] = g + 1
        return c

    lax.fori_loop(0, tc_ref[e], body, 0)

    @pl.when((j == nj - 1) & (e == n_e - 1))
    def _():
        g = g_ref[0]

        @pl.when(g >= 2)
        def _():
            out_copy(j, 0, g % 2).wait()

        @pl.when(g >= 1)
        def _():
            out_copy(j, 0, (g - 1) % 2).wait()

        obuf[0] = jnp.zeros(obuf.shape[1:], obuf.dtype)

        def zero_tile(t, c):
            for jj in range(nj):
                cp = out_copy(jj, t, 0)
                cp.start()
                cp.wait()
            return c

        lax.fori_loop(n_used, n_tiles, zero_tile, 0)


def _gate_up_kernel(ts_ref, tc_ref, xs_ref, wg_ref, wu_ref, bg_ref, bu_ref, act_ref,
                    wgb_sc, wub_sc, xbuf, obuf, g_ref, xsem, osem, *, tf, nj, n_e, n_tiles):
    def in_copy(t, slot):
        return pltpu.make_async_copy(xs_ref.at[pl.ds(t * MOE_TILE, MOE_TILE), :], xbuf.at[slot], xsem.at[slot])

    def out_copy(jj, t, slot):
        return pltpu.make_async_copy(obuf.at[slot],
                                     act_ref.at[pl.ds(t * MOE_TILE, MOE_TILE), pl.ds(jj * tf, tf)], osem.at[slot])

    @pl.when(tc_ref[pl.program_id(1)] > 0)
    def _():
        wgb_sc[...] = wg_ref[...].astype(BF16)
        wub_sc[...] = wu_ref[...].astype(BF16)

    def compute(in_slot, out_slot):
        w = xbuf[in_slot]
        half = w.shape[1]
        lo = lax.bitcast_convert_type(w << 16, F32).astype(BF16)
        hi = lax.bitcast_convert_type(w & jnp.uint32(0xFFFF0000), F32).astype(BF16)
        g = (jnp.dot(lo, wgb_sc[:half, :], preferred_element_type=F32)
             + jnp.dot(hi, wgb_sc[half:, :], preferred_element_type=F32) + bg_ref[...])
        u = (jnp.dot(lo, wub_sc[:half, :], preferred_element_type=F32)
             + jnp.dot(hi, wub_sc[half:, :], preferred_element_type=F32) + bu_ref[...])
        g = jnp.minimum(g, SWIGLU_LIMIT)
        u = jnp.clip(u, -SWIGLU_LIMIT, SWIGLU_LIMIT)
        obuf[out_slot] = ((u + 1.0) * (g * jax.nn.sigmoid(SWIGLU_ALPHA * g))).astype(obuf.dtype)

    _stream_expert_tiles(ts_ref, tc_ref, g_ref, obuf, in_copy, out_copy, compute, nj=nj, n_e=n_e, n_tiles=n_tiles)


def _gate_up(tile_start, tile_count, xs, w_gu, b_gu, *, n_tiles, tf):
    n_e, d, f2 = w_gu.shape
    f = f2 // 2
    nj = f // tf
    grid_spec = pltpu.PrefetchScalarGridSpec(
        num_scalar_prefetch=2,
        grid=(nj, n_e),
        in_specs=[pl.BlockSpec(memory_space=pl.ANY),
                  pl.BlockSpec((None, d, tf), lambda j, e, ts, tc: (e, 0, j)),
                  pl.BlockSpec((None, d, tf), lambda j, e, ts, tc: (e, 0, nj + j)),
                  pl.BlockSpec((None, 1, tf), lambda j, e, ts, tc: (e, 0, j)),
                  pl.BlockSpec((None, 1, tf), lambda j, e, ts, tc: (e, 0, nj + j))],
        out_specs=pl.BlockSpec(memory_space=pl.ANY),
        scratch_shapes=[pltpu.VMEM((d, tf), BF16), pltpu.VMEM((d, tf), BF16),
                        pltpu.VMEM((3, MOE_TILE, d // 2), jnp.uint32), pltpu.VMEM((2, MOE_TILE, tf), BF16),
                        pltpu.SMEM((1,), jnp.int32),
                        pltpu.SemaphoreType.DMA((3,)), pltpu.SemaphoreType.DMA((2,))],
    )
    return pl.pallas_call(
        functools.partial(_gate_up_kernel, tf=tf, nj=nj, n_e=n_e, n_tiles=n_tiles),
        grid_spec=grid_spec,
        out_shape=jax.ShapeDtypeStruct((n_tiles * MOE_TILE, f), BF16),
        compiler_params=_cparams(2),
        name="moe_gate_up",
    )(tile_start, tile_count, xs, w_gu, w_gu, b_gu, b_gu)


def _down_kernel(ts_ref, tc_ref, act_ref, wd_ref, bd_ref, y_ref, wdb_sc, abuf, obuf, g_ref, asem, osem,
                 *, tn, nj, n_e, n_tiles):
    def in_copy(t, slot):
        return pltpu.make_async_copy(act_ref.at[pl.ds(t * MOE_TILE, MOE_TILE), :], abuf.at[slot], asem.at[slot])

    def out_copy(jj, t, slot):
        return pltpu.make_async_copy(obuf.at[slot],
                                     y_ref.at[pl.ds(t * MOE_TILE, MOE_TILE), pl.ds(jj * tn, tn)], osem.at[slot])

    @pl.when(tc_ref[pl.program_id(1)] > 0)
    def _():
        wdb_sc[...] = wd_ref[...].astype(BF16)

    def compute(in_slot, out_slot):
        obuf[out_slot] = jnp.dot(abuf[in_slot], wdb_sc[...], preferred_element_type=F32) + bd_ref[...]

    _stream_expert_tiles(ts_ref, tc_ref, g_ref, obuf, in_copy, out_copy, compute, nj=nj, n_e=n_e, n_tiles=n_tiles)


def _down(tile_start, tile_count, act, w_d, b_d, *, n_tiles, tn):
    n_e, f, d = w_d.shape
    grid_spec = pltpu.PrefetchScalarGridSpec(
        num_scalar_prefetch=2,
        grid=(d // tn, n_e),
        in_specs=[pl.BlockSpec(memory_space=pl.ANY),
                  pl.BlockSpec((None, f, tn), lambda j, e, ts, tc: (e, 0, j)),
                  pl.BlockSpec((None, 1, tn), lambda j, e, ts, tc: (e, 0, j))],
        out_specs=pl.BlockSpec(memory_space=pl.ANY),
        scratch_shapes=[pltpu.VMEM((f, tn), BF16),
                        pltpu.VMEM((3, MOE_TILE, f), BF16), pltpu.VMEM((2, MOE_TILE, tn), F32),
                        pltpu.SMEM((1,), jnp.int32),
                        pltpu.SemaphoreType.DMA((3,)), pltpu.SemaphoreType.DMA((2,))],
    )
    return pl.pallas_call(
        functools.partial(_down_kernel, tn=tn, nj=d // tn, n_e=n_e, n_tiles=n_tiles),
        grid_spec=grid_spec,
        out_shape=jax.ShapeDtypeStruct((n_tiles * MOE_TILE, d), F32),
        compiler_params=_cparams(2),
        name="moe_down",
    )(tile_start, tile_count, act, w_d, b_d)


def _combine_kernel(dest_ref, h_ref, gate_ref, g_ref, ys_ref, out_ref, buf, sem, *, tm, lp, row0, nst):
    s = pl.program_id(0)
    n_steps = pl.num_programs(0)

    def start_gather(step, slot):
        t0 = (step // nst) * lp + row0 + (step % nst) * tm

        def start_row(r, c):
            for k in range(TOP_K):
                d = dest_ref[(t0 + r) * TOP_K + k]
                pltpu.make_async_copy(ys_ref.at[pl.ds(d, 1), :], buf.at[slot, k, pl.ds(r, 1), :],
                                      sem.at[slot]).start(priority=k % 2)
            return c

        lax.fori_loop(0, tm, start_row, 0, unroll=8)

    @pl.when(s == 0)
    def _():
        start_gather(0, 0)

    @pl.when(s + 1 < n_steps)
    def _():
        start_gather(s + 1, (s + 1) % 2)

    slot = s % 2
    for k in range(TOP_K):
        pltpu.make_async_copy(ys_ref.at[pl.ds(0, tm), :], buf.at[slot, k], sem.at[slot]).wait()

    gate = gate_ref[...]
    acc = jnp.zeros(h_ref.shape, F32)
    for k in range(TOP_K):
        acc = acc + gate[:, k:k + 1] * buf[slot, k]
    h = h_ref[...] + acc
    ms = jnp.mean(h * h, axis=-1, keepdims=True)
    out_ref[...] = (h * lax.rsqrt(ms + EPS) * g_ref[...]).astype(out_ref.dtype)


def _combine(dest_flat, h2, gate, g, ys, *, batch, seq, lp, row0, tm):
    d = h2.shape[1]
    nb_b = lp // tm
    nb0 = row0 // tm
    nst = seq // tm
    blk = lambda s: (s // nst) * nb_b + nb0 + s % nst
    grid_spec = pltpu.PrefetchScalarGridSpec(
        num_scalar_prefetch=1,
        grid=(batch * nst,),
        in_specs=[pl.BlockSpec((tm, d), lambda s, dest: (blk(s), 0)),
                  pl.BlockSpec((tm, TOP_K), lambda s, dest: (blk(s), 0)),
                  pl.BlockSpec((1, d), lambda s, dest: (0, 0)),
                  pl.BlockSpec(memory_space=pl.ANY)],
        out_specs=pl.BlockSpec((None, tm, d), lambda s, dest: (s // nst, s % nst, 0)),
        scratch_shapes=[pltpu.VMEM((2, TOP_K, tm, d), F32), pltpu.SemaphoreType.DMA((2,))],
    )
    return pl.pallas_call(
        functools.partial(_combine_kernel, tm=tm, lp=lp, row0=row0, nst=nst),
        grid_spec=grid_spec,
        out_shape=jax.ShapeDtypeStruct((batch, seq, d), F32),
        compiler_params=_cparams(1),
        name="moe_combine",
    )(dest_flat, h2, gate, g.reshape(1, d).astype(F32), ys)


def _pick(pref, n):
    t = pref
    while n % t:
        t //= 2
    return t


def _row_tile(n, pref):
    t = pref // ROW_ALIGN * ROW_ALIGN
    while n % t:
        t -= ROW_ALIGN
    return t


def kernel(x, meta_tokens, norm_mix_g, w_in, b_gate, lambda_q1, lambda_k1, lambda_q2, lambda_k2, head_norm_g, w_attn_out, conv_w, conv_b, conv_ln_g, conv_ln_b, w_conv_out, b_conv_out, w_out, norm_ffn_g, w_router, b_router, w_gate_up, b_gate_up, w_down, b_down, final_norm_g):
    batch, seq, d = x.shape
    depth = w_in.shape[0]
    assert depth == 1 and seq % ROW_ALIGN == 0 and N_META <= ROW_ALIGN
    n_heads = d // 256
    hw = 2 * HEAD_DIM
    qk_w = n_heads * hw
    conv_ch = conv_w.shape[2]
    n_pad = ROW_ALIGN - N_META
    lp = n_pad + N_META + seq
    tp = batch * lp
    f = w_down.shape[2]
    layer = 0
    lam_init = 0.8 - 0.6 * math.exp(-0.3 * layer)

    h0, u = _embed_norm(x, meta_tokens, norm_mix_g[layer], n_pad=n_pad)

    proj = _matmul(u, w_in[layer], BF16, _row_tile(tp, 1536), _pick(512, w_in.shape[2]), name="in_proj")

    tq = 384 if lp % 384 == 0 else ROW_ALIGN
    vt = proj[:, 2 * qk_w:3 * qk_w].reshape(batch, lp, n_heads, hw).transpose(0, 2, 3, 1)
    o = _attention(proj, vt, lambda_q1[layer], lambda_k1[layer], lambda_q2[layer], lambda_k2[layer],
                   head_norm_g[layer], batch=batch, lp=lp, n_heads=n_heads, tq=tq, lam_init=lam_init,
                   n_pad=n_pad)
    ca_col = 3 * qk_w
    z = _conv_branch(proj, conv_w[layer], conv_b[layer], conv_ln_g[layer], conv_ln_b[layer],
                     ca_blk=ca_col // conv_ch, cg_blk=ca_col // conv_ch + 1, tm=_pick(256, tp))
    merged = _merge(o, z, proj, w_attn_out[layer], w_conv_out[layer],
                    b_conv_out[layer], b_gate[layer], gate_col=ca_col + 2 * conv_ch,
                    tm=_row_tile(tp, 768), tn=_pick(512, d))
    h2 = _matmul(merged, w_out[layer], F32, _row_tile(tp, 768), _pick(512, d), res=h0, name="out_proj")

    pos = np.arange(tp) % lp
    valid_np = pos >= n_pad
    valid = jnp.asarray(valid_np.astype(np.float32).reshape(1, tp))
    u_packed, top_e, gate_t, rank_t, cnt = _router(h2, norm_ffn_g[layer], w_router[layer], b_router[layer],
                                                   valid, tm=_pick(256, tp))
    counts = cnt[:, 0]
    padded = (counts + MOE_TILE - 1) // MOE_TILE * MOE_TILE
    e_ids = np.arange(N_EXPERTS)
    pad_end = jnp.sum(jnp.where(jnp.asarray(e_ids[None, :] <= e_ids[:, None]), padded[None, :], 0), axis=1)
    pad_start = pad_end - padded
    start_tok = jnp.sum(jnp.where(top_e[:, :, None] == jnp.asarray(e_ids, jnp.int32), pad_start, 0), axis=-1)
    n_real = int(valid_np.sum()) * TOP_K
    assert n_real >= 2 * MOE_TILE
    n_tiles =-(-(n_real + N_EXPERTS * (MOE_TILE - 1)) // MOE_TILE)
    n_slots = n_tiles * MOE_TILE
    dump = n_slots + (np.cumsum(~valid_np) - 1)[None, :] * TOP_K + np.arange(TOP_K)[:, None]
    dest_t = jnp.where(jnp.asarray(valid_np)[None, :], start_tok + rank_t, jnp.asarray(dump, jnp.int32))
    dest_flat = dest_t.T.reshape(-1).astype(jnp.int32)
    n_dump = int((~valid_np).sum()) * TOP_K
    n_used = (pad_end[-1] // MOE_TILE).astype(jnp.int32).reshape(1)
    tile_start = jnp.asarray(np.arange(n_tiles, dtype=np.int32) * MOE_TILE)
    tile_e = jnp.minimum(jnp.sum((pad_end[None, :] <= tile_start[:, None]).astype(jnp.int32), axis=1),
                         N_EXPERTS - 1).astype(jnp.int32)

    xs = _dispatch(dest_flat, tile_e, n_used, u_packed, n_slots + n_dump, tm=ROW_ALIGN, n_tiles=n_tiles)
    tile_first = (pad_start // MOE_TILE).astype(jnp.int32)
    tile_count = (padded // MOE_TILE).astype(jnp.int32)
    act = _gate_up(tile_first, tile_count, xs, w_gate_up[layer],
                   b_gate_up[layer].reshape(N_EXPERTS, 1, 2 * f).astype(F32), n_tiles=n_tiles, tf=_pick(512, f))
    ys = _down(tile_first, tile_count, act, w_down[layer],
               b_down[layer].reshape(N_EXPERTS, 1, d).astype(F32), n_tiles=n_tiles, tn=_pick(2048, d))
    return _combine(dest_flat, h2, gate_t.T, final_norm_g, ys, batch=batch, seq=seq, lp=lp,
                    row0=ROW_ALIGN, tm=ROW_ALIGN)
```

```python
import functools
import math

import numpy as np
import jax
import jax.numpy as jnp
from jax import lax
from jax.experimental import pallas as pl
from jax.experimental.pallas import tpu as pltpu

N_META = 16
HEAD_DIM = 64
N_EXPERTS = 32
TOP_K = 4
CONV_K = 31
EPS = 1e-5
SWIGLU_LIMIT = 7.0
SWIGLU_ALPHA = 1.702
SUBLANES = 8
LANES = 128
ROW_ALIGN = 128
CONV_HALO = 32
MOE_TILE = 256
ATTN_HEADS_PER_STEP = 4
VMEM_LIMIT = 56 * 1024 * 1024

F32 = jnp.float32
BF16 = jnp.bfloat16


def _cparams(n_axes, flags=None):
    return pltpu.CompilerParams(dimension_semantics=("arbitrary",) * n_axes,
                                vmem_limit_bytes=VMEM_LIMIT, flags=flags)


def _embed_norm_kernel(x_ref, meta_ref, g_ref, h_ref, u_ref, *, n_pad):
    i = pl.program_id(1)

    @pl.when(i == 0)
    def _():
        h_ref[0:n_pad, :] = jnp.zeros((n_pad, h_ref.shape[1]), F32)
        h_ref[n_pad:, :] = meta_ref[...]

    @pl.when(i > 0)
    def _():
        h_ref[...] = x_ref[...]

    h = h_ref[...]
    ms = jnp.mean(h * h, axis=-1, keepdims=True)
    u_ref[...] = (h * lax.rsqrt(ms + EPS) * g_ref[...]).astype(u_ref.dtype)


def _embed_norm(x, meta, g, *, n_pad):
    batch, seq, d = x.shape
    tm = ROW_ALIGN
    nb = (n_pad + N_META + seq) // tm
    out_spec = pl.BlockSpec((tm, d), lambda b, i: (b * nb + i, 0))
    return pl.pallas_call(
        functools.partial(_embed_norm_kernel, n_pad=n_pad),
        grid=(batch, nb),
        in_specs=[pl.BlockSpec((None, tm, d), lambda b, i: (b, jnp.maximum(i - 1, 0), 0)),
                  pl.BlockSpec((N_META, d), lambda b, i: (0, 0)),
                  pl.BlockSpec((1, d), lambda b, i: (0, 0))],
        out_specs=[out_spec, out_spec],
        out_shape=[jax.ShapeDtypeStruct((batch * nb * tm, d), F32),
                   jax.ShapeDtypeStruct((batch * nb * tm, d), BF16)],
        compiler_params=_cparams(2),
        name="embed_norm",
    )(x, meta.astype(F32), g.reshape(1, d).astype(F32))


def _cast_weight_once(w_ref, wb_sc):
    @pl.when(pl.program_id(1) == 0)
    def _():
        wb_sc[...] = w_ref[...].astype(BF16)


def _matmul_kernel(a_ref, w_ref, o_ref, wb_sc):
    _cast_weight_once(w_ref, wb_sc)
    o_ref[...] = jnp.dot(a_ref[...], wb_sc[...], preferred_element_type=F32).astype(o_ref.dtype)


def _matmul_res_kernel(a_ref, w_ref, r_ref, o_ref, wb_sc):
    _cast_weight_once(w_ref, wb_sc)
    acc = jnp.dot(a_ref[...], wb_sc[...], preferred_element_type=F32)
    o_ref[...] = (acc + r_ref[...]).astype(o_ref.dtype)


def _matmul(a, w, out_dtype, tm, tn, res=None, name="matmul"):
    m, k = a.shape
    n = w.shape[1]
    in_specs = [pl.BlockSpec((tm, k), lambda j, i: (i, 0)),
                pl.BlockSpec((k, tn), lambda j, i: (0, j))]
    args = [a, w]
    kern = _matmul_kernel
    if res is not None:
        in_specs.append(pl.BlockSpec((tm, tn), lambda j, i: (i, j)))
        args.append(res)
        kern = _matmul_res_kernel
    return pl.pallas_call(
        kern,
        grid=(n // tn, m // tm),
        in_specs=in_specs,
        out_specs=pl.BlockSpec((tm, tn), lambda j, i: (i, j)),
        out_shape=jax.ShapeDtypeStruct((m, n), out_dtype),
        scratch_shapes=[pltpu.VMEM((k, tn), BF16)],
        compiler_params=_cparams(2),
        name=name,
    )(*args)


def _attn_kernel(lq1_ref, lk1_ref, lq2_ref, lk2_ref, hg_ref, bias_ref, q_ref, k_ref, vt_ref, o_ref,
                 q12_sc, s_sc, m_sc, l_sc, acc_sc, *, tq, lam_init, n_hd):
    hw = 2 * HEAD_DIM
    nq = q_ref.shape[0] // tq
    lam = (jnp.exp(jnp.sum(lq1_ref[...] * lk1_ref[...], axis=-1, keepdims=True))
           - jnp.exp(jnp.sum(lq2_ref[...] * lk2_ref[...], axis=-1, keepdims=True)) + lam_init)

    def q_tile(qi, carry):
        q0 = pl.multiple_of(qi * tq, tq)
        for hd in range(n_hd):
            q = (q_ref[pl.ds(q0, tq), hd * hw:(hd + 1) * hw].astype(F32)
                 * (HEAD_DIM ** -0.5 * math.log2(math.e))).astype(BF16)
            lane = lax.broadcasted_iota(jnp.int32, q.shape, 1)
            zero = jnp.zeros_like(q)
            q12_sc[2 * hd] = jnp.where(lane < HEAD_DIM, q, zero)
            q12_sc[2 * hd + 1] = jnp.where(lane >= HEAD_DIM, q, zero)
        m_sc[...] = jnp.full(m_sc.shape, -jnp.inf, F32)
        l_sc[...] = jnp.zeros(l_sc.shape, F32)
        acc_sc[...] = jnp.zeros(acc_sc.shape, F32)

        def scores(kj, c):
            hd = c // 2
            k0 = pl.multiple_of(kj * tq, tq)
            s = lax.dot_general(k_ref[pl.ds(k0, tq), hd * hw:(hd + 1) * hw], q12_sc[c],
                                (((1,), (1,)), ((), ())), preferred_element_type=F32)
            kind = jnp.where(kj == 0, 1, 0) + jnp.where(kj == qi, 2, 0)
            s_sc[c] = s + bias_ref[kind]

        def update(kj, c):
            hd = c // 2
            k0 = pl.multiple_of(kj * tq, tq)
            s = s_sc[c]
            m_prev = m_sc[c]
            m_new = jnp.maximum(m_prev, jnp.max(s, axis=0, keepdims=True))
            alpha = jnp.exp2(m_prev - m_new)
            p = jnp.exp2(s - m_new)
            l_sc[c] = alpha * l_sc[c] + jnp.sum(p.reshape(tq // 8, 8, tq), axis=0)
            acc_sc[c] = alpha * acc_sc[c] + jnp.dot(vt_ref[hd, :, pl.ds(k0, tq)], p.astype(BF16),
                                                    preferred_element_type=F32)
            m_sc[c] = m_new

        for hd in range(n_hd):
            scores(0, 2 * hd)

        def body(kj, c):
            for hd in range(n_hd):
                scores(kj, 2 * hd + 1)
                update(kj, 2 * hd)
                scores(kj + 1, 2 * hd)
            for hd in range(n_hd):
                update(kj, 2 * hd + 1)
            return c

        lax.fori_loop(0, qi, body, 0)
        for hd in range(n_hd):
            scores(qi, 2 * hd + 1)
            update(qi, 2 * hd)
        for hd in range(n_hd):
            update(qi, 2 * hd + 1)

        for hd in range(n_hd):
            o1 = acc_sc[2 * hd] / jnp.sum(l_sc[2 * hd], axis=0, keepdims=True)
            o2 = acc_sc[2 * hd + 1] / jnp.sum(l_sc[2 * hd + 1], axis=0, keepdims=True)
            o = o1 - lam * o2
            ms = jnp.mean(o * o, axis=0, keepdims=True)
            o = o * lax.rsqrt(ms + EPS) * hg_ref[...] * (1.0 - lam_init)
            o_ref[pl.ds(q0, tq), hd * hw:(hd + 1) * hw] = o.T.astype(o_ref.dtype)
        return carry

    lax.fori_loop(0, nq, q_tile, 0)


def _attn_bias(tq, n_pad):
    neg = np.float32(np.finfo(np.float32).min)
    r = np.arange(tq)[:, None]
    c = np.arange(tq)[None, :]
    pad = np.broadcast_to(r < n_pad, (tq, tq))
    future = r > c
    tiles = [np.zeros((tq, tq), bool), pad, future, pad | future]
    return jnp.asarray(np.stack([np.where(t, neg, np.float32(0)) for t in tiles]).astype(np.float32))


def _attention(proj, vt, lq1, lk1, lq2, lk2, head_g, *, batch, lp, n_heads, tq, lam_init, n_pad):
    hw = 2 * HEAD_DIM
    n_hd = ATTN_HEADS_PER_STEP if n_heads % ATTN_HEADS_PER_STEP == 0 else 1
    assert n_pad <= tq
    koff = n_heads // n_hd
    vec = lambda a: a.reshape(1, -1).astype(F32)
    small = lambda n: pl.BlockSpec((1, n), lambda b, h: (0, 0))
    return pl.pallas_call(
        functools.partial(_attn_kernel, tq=tq, lam_init=lam_init, n_hd=n_hd),
        grid=(batch, n_heads // n_hd),
        in_specs=[small(HEAD_DIM), small(HEAD_DIM), small(HEAD_DIM), small(HEAD_DIM),
                  pl.BlockSpec((hw, 1), lambda b, h: (0, 0)),
                  pl.BlockSpec((4, tq, tq), lambda b, h: (0, 0, 0)),
                  pl.BlockSpec((lp, n_hd * hw), lambda b, h: (b, h)),
                  pl.BlockSpec((lp, n_hd * hw), lambda b, h: (b, koff + h)),
                  pl.BlockSpec((None, n_hd, hw, lp), lambda b, h: (b, h, 0, 0))],
        out_specs=pl.BlockSpec((lp, n_hd * hw), lambda b, h: (b, h)),
        out_shape=jax.ShapeDtypeStruct((batch * lp, n_heads * hw), BF16),
        scratch_shapes=[pltpu.VMEM((2 * n_hd, tq, hw), BF16),
                        pltpu.VMEM((2 * n_hd, tq, tq), F32),
                        pltpu.VMEM((2 * n_hd, 1, tq), F32), pltpu.VMEM((2 * n_hd, 8, tq), F32),
                        pltpu.VMEM((2 * n_hd, hw, tq), F32)],
        compiler_params=_cparams(2),
        name="diff_attention",
    )(vec(lq1), vec(lk1), vec(lq2), vec(lk2), head_g.reshape(hw, 1).astype(F32), _attn_bias(tq, n_pad),
      proj, proj, vt)


def _conv_kernel(ca_ref, cg_ref, ca_h_ref, cg_h_ref, w_ref, b_ref, lg_ref, lb_ref, z_ref,
                 ext_sc, sh_sc, y_sc, *, tm):
    i = pl.program_id(0)
    n_ch = w_ref.shape[1]
    n_ext = CONV_HALO + tm
    glu = lambda a, g: a.astype(F32) * jax.nn.sigmoid(g.astype(F32))
    halo = glu(ca_h_ref[...], cg_h_ref[...])
    ext_sc[0:CONV_HALO, :] = jnp.where(i > 0, halo, jnp.zeros_like(halo))
    ext_sc[CONV_HALO:n_ext, :] = glu(ca_ref[...], cg_ref[...])
    ext_sc[n_ext:n_ext + SUBLANES, :] = jnp.zeros((SUBLANES, n_ch), F32)
    base = CONV_HALO - (CONV_K - 1)

    def slab(lc, carry):
        l0 = pl.multiple_of(lc * LANES, LANES)
        for rho in range(SUBLANES):
            sh_sc[rho] = ext_sc[rho:rho + n_ext, pl.ds(l0, LANES)]
        acc = jnp.zeros((tm, LANES), F32) + b_ref[:, pl.ds(l0, LANES)]
        for j in range(CONV_K):
            rho = (base + j) % SUBLANES
            a = base + j - rho
            acc = acc + w_ref[j:j + 1, pl.ds(l0, LANES)] * sh_sc[rho, a:a + tm, :]
        y_sc[:, pl.ds(l0, LANES)] = acc
        return carry

    lax.fori_loop(0, n_ch // LANES, slab, 0)
    acc = y_sc[...]
    mu = jnp.mean(acc, axis=-1, keepdims=True)
    d = acc - mu
    var = jnp.mean(d * d, axis=-1, keepdims=True)
    y = d * lax.rsqrt(var + EPS) * lg_ref[...] + lb_ref[...]
    z_ref[...] = (y * jax.nn.sigmoid(y)).astype(z_ref.dtype)


def _conv_branch(proj, conv_w, conv_b, ln_g, ln_b, *, ca_blk, cg_blk, tm):
    m = proj.shape[0]
    c = conv_w.shape[1]
    hb = tm // CONV_HALO
    row = lambda a: a.reshape(1, c).astype(F32)
    vec = pl.BlockSpec((1, c), lambda i: (0, 0))
    return pl.pallas_call(
        functools.partial(_conv_kernel, tm=tm),
        grid=(m // tm,),
        in_specs=[pl.BlockSpec((tm, c), lambda i: (i, ca_blk)),
                  pl.BlockSpec((tm, c), lambda i: (i, cg_blk)),
                  pl.BlockSpec((CONV_HALO, c), lambda i: (jnp.maximum(i * hb - 1, 0), ca_blk)),
                  pl.BlockSpec((CONV_HALO, c), lambda i: (jnp.maximum(i * hb - 1, 0), cg_blk)),
                  pl.BlockSpec((CONV_K, c), lambda i: (0, 0)),
                  vec, vec, vec],
        out_specs=pl.BlockSpec((tm, c), lambda i: (i, 0)),
        out_shape=jax.ShapeDtypeStruct((m, c), BF16),
        scratch_shapes=[pltpu.VMEM((CONV_HALO + tm + SUBLANES, c), F32),
                        pltpu.VMEM((SUBLANES, CONV_HALO + tm, LANES), F32),
                        pltpu.VMEM((tm, c), F32)],
        compiler_params=_cparams(1),
        name="conformer_conv",
    )(proj, proj, proj, proj, conv_w.astype(F32), row(conv_b), row(ln_g), row(ln_b))


def _merge_kernel(o_ref, z_ref, wa_ref, wc_ref, bc_ref, g1_ref, g2_ref, bg1_ref, bg2_ref, out_ref,
                  wab_sc, wcb_sc):
    _cast_weight_once(wa_ref, wab_sc)
    _cast_weight_once(wc_ref, wcb_sc)
    ya = jnp.dot(o_ref[...], wab_sc[...], preferred_element_type=F32)
    yc = jnp.dot(z_ref[...], wcb_sc[...], preferred_element_type=F32) + bc_ref[...]
    g1 = jax.nn.sigmoid(g1_ref[...].astype(F32) + bg1_ref[...])
    g2 = jax.nn.sigmoid(g2_ref[...].astype(F32) + bg2_ref[...])
    out_ref[...] = (g1 * ya + g2 * yc).astype(out_ref.dtype)


def _merge(o, z, proj, wa, wc, bc, bg, *, gate_col, tm, tn):
    m, ka = o.shape
    kc = z.shape[1]
    d = wa.shape[1]
    g1_blk = gate_col // tn
    g2_blk = (gate_col + d) // tn
    nb = d // tn
    bg2 = bg.reshape(1, 2 * d).astype(F32)
    return pl.pallas_call(
        _merge_kernel,
        grid=(nb, m // tm),
        in_specs=[pl.BlockSpec((tm, ka), lambda j, i: (i, 0)),
                  pl.BlockSpec((tm, kc), lambda j, i: (i, 0)),
                  pl.BlockSpec((ka, tn), lambda j, i: (0, j)),
                  pl.BlockSpec((kc, tn), lambda j, i: (0, j)),
                  pl.BlockSpec((1, tn), lambda j, i: (0, j)),
                  pl.BlockSpec((tm, tn), lambda j, i: (i, g1_blk + j)),
                  pl.BlockSpec((tm, tn), lambda j, i: (i, g2_blk + j)),
                  pl.BlockSpec((1, tn), lambda j, i: (0, j)),
                  pl.BlockSpec((1, tn), lambda j, i: (0, nb + j))],
        out_specs=pl.BlockSpec((tm, tn), lambda j, i: (i, j)),
        out_shape=jax.ShapeDtypeStruct((m, d), BF16),
        scratch_shapes=[pltpu.VMEM((ka, tn), BF16), pltpu.VMEM((kc, tn), BF16)],
        compiler_params=_cparams(2),
        name="mixer_merge",
    )(o, z, wa, wc, bc.reshape(1, d).astype(F32), proj, proj, bg2, bg2)


def _router_kernel(h_ref, g_ref, wr_ref, br_ref, valid_ref,
                   up_ref, e_ref, gate_ref, rank_ref, cnt_ref, carry_sc, *, tm):
    i = pl.program_id(0)

    @pl.when(i == 0)
    def _():
        carry_sc[...] = jnp.zeros(carry_sc.shape, F32)

    h = h_ref[...]
    ms = jnp.mean(h * h, axis=-1, keepdims=True)
    u = h * lax.rsqrt(ms + EPS) * g_ref[...]

    half = u.shape[1] // 2
    bits = lax.bitcast_convert_type(u.astype(BF16).astype(F32), jnp.uint32)
    up_ref[...] = (bits[:, half:] & jnp.uint32(0xFFFF0000)) | (bits[:, :half] >> 16)

    logits = lax.dot_general(wr_ref[...], u, (((1,), (1,)), ((), ())),
                             precision=lax.Precision.HIGHEST,
                             preferred_element_type=F32) + br_ref[...]
    n_e = logits.shape[0]
    eiota = lax.broadcasted_iota(jnp.int32, logits.shape, 0).astype(F32)
    work = logits
    sel = jnp.zeros(logits.shape, jnp.bool_)
    top_l, top_e = [], []
    for _ in range(TOP_K):
        mx = jnp.max(work, axis=0, keepdims=True)
        idx = jnp.min(jnp.where(work == mx, eiota, float(n_e)), axis=0, keepdims=True)
        hit = eiota == idx
        top_l.append(mx)
        top_e.append(idx)
        sel = sel | hit
        work = jnp.where(hit, -jnp.inf, work)
    ex = [jnp.exp(t - top_l[0]) for t in top_l]
    den = ex[0] + ex[1] + ex[2] + ex[3]
    gate_ref[...] = jnp.concatenate([e / den for e in ex], axis=0)
    e_ref[...] = jnp.concatenate(top_e, axis=0).astype(jnp.int32)

    selv = jnp.where(sel & (valid_ref[...] > 0.0), 1.0, 0.0)
    before = (lax.broadcasted_iota(jnp.int32, (tm, tm), 0)
              < lax.broadcasted_iota(jnp.int32, (tm, tm), 1)).astype(BF16)
    rank_all = jnp.dot(selv.astype(BF16), before, preferred_element_type=F32) + carry_sc[...]
    ranks = [jnp.sum(jnp.where(eiota == idx, rank_all, 0.0), axis=0, keepdims=True) for idx in top_e]
    rank_ref[...] = jnp.concatenate(ranks, axis=0).astype(jnp.int32)
    carry = carry_sc[...] + jnp.sum(selv, axis=1, keepdims=True)
    carry_sc[...] = carry
    cnt_ref[...] = jnp.broadcast_to(carry, cnt_ref.shape).astype(jnp.int32)


def _router(h2, g, w_router, b_router, valid, *, tm):
    m, d = h2.shape
    n_e = w_router.shape[1]
    tok = lambda dt: jax.ShapeDtypeStruct((TOP_K, m), dt)
    tok_spec = pl.BlockSpec((TOP_K, tm), lambda i: (0, i))
    return pl.pallas_call(
        functools.partial(_router_kernel, tm=tm),
        grid=(m // tm,),
        in_specs=[pl.BlockSpec((tm, d), lambda i: (i, 0)),
                  pl.BlockSpec((1, d), lambda i: (0, 0)),
                  pl.BlockSpec((n_e, d), lambda i: (0, 0)),
                  pl.BlockSpec((n_e, 1), lambda i: (0, 0)),
                  pl.BlockSpec((1, tm), lambda i: (0, i))],
        out_specs=[pl.BlockSpec((tm, d // 2), lambda i: (i, 0)),
                   tok_spec, tok_spec, tok_spec,
                   pl.BlockSpec((n_e, 128), lambda i: (0, 0))],
        out_shape=[jax.ShapeDtypeStruct((m, d // 2), jnp.uint32),
                   tok(jnp.int32), tok(F32), tok(jnp.int32),
                   jax.ShapeDtypeStruct((n_e, 128), jnp.int32)],
        scratch_shapes=[pltpu.VMEM((n_e, 1), F32)],
        compiler_params=_cparams(1),
        name="router",
    )(h2, g.reshape(1, d).astype(F32), w_router.T.astype(F32), b_router.reshape(n_e, 1).astype(F32), valid)


def _dispatch_kernel(dest_ref, te_ref, nu_ref, u_ref, xs_ref, zero_sc, sem, zsem, *, tm, n_tiles):
    i = pl.program_id(0)

    @pl.when(i == 0)
    def _():
        zero_sc[...] = jnp.zeros(zero_sc.shape, zero_sc.dtype)
        nu = nu_ref[0]

        def partly_filled(t):
            nxt = te_ref[jnp.minimum(t + 1, n_tiles - 1)]
            return (t >= nu - 1) | (te_ref[t] != nxt)

        def tile_copy(t):
            return pltpu.make_async_copy(zero_sc, xs_ref.at[pl.ds(t * MOE_TILE, MOE_TILE), :], zsem)

        def start(t, c):
            @pl.when(partly_filled(t))
            def _():
                tile_copy(t).start()
            return c

        def wait(t, c):
            @pl.when(partly_filled(t))
            def _():
                tile_copy(t).wait()
            return c

        lax.fori_loop(0, n_tiles, start, 0)
        lax.fori_loop(0, n_tiles, wait, 0)

    def start_row(r, c):
        for k in range(TOP_K):
            d = dest_ref[(i * tm + r) * TOP_K + k]
            pltpu.make_async_copy(u_ref.at[pl.ds(r, 1), :], xs_ref.at[pl.ds(d, 1), :], sem).start(priority=k % 2)
        return c

    lax.fori_loop(0, tm, start_row, 0, unroll=8)
    for k in range(TOP_K):
        pltpu.make_async_copy(u_ref, xs_ref.at[pl.ds(0, tm), :], sem).wait()


def _dispatch(dest_flat, tile_e, n_used, u_packed, n_rows, *, tm, n_tiles):
    m, w = u_packed.shape
    grid_spec = pltpu.PrefetchScalarGridSpec(
        num_scalar_prefetch=3,
        grid=(m // tm,),
        in_specs=[pl.BlockSpec((tm, w), lambda i, dest, te, nu: (i, 0))],
        out_specs=pl.BlockSpec(memory_space=pl.ANY),
        scratch_shapes=[pltpu.VMEM((MOE_TILE, w), jnp.uint32),
                        pltpu.SemaphoreType.DMA(()), pltpu.SemaphoreType.DMA(())],
    )
    return pl.pallas_call(
        functools.partial(_dispatch_kernel, tm=tm, n_tiles=n_tiles),
        grid_spec=grid_spec,
        out_shape=jax.ShapeDtypeStruct((n_rows, w), jnp.uint32),
        compiler_params=_cparams(1),
        name="moe_dispatch",
    )(dest_flat, tile_e, n_used, u_packed)


def _stream_expert_tiles(ts_ref, tc_ref, g_ref, obuf, in_copy, out_copy, compute, *, nj, n_e, n_tiles):
    j = pl.program_id(0)
    e = pl.program_id(1)
    n_used = ts_ref[n_e - 1] + tc_ref[n_e - 1]
    t0 = ts_ref[e]

    @pl.when((j == 0) & (e == 0))
    def _():
        g_ref[0] = 0
        in_copy(0, 0).start()
        in_copy(1, 1).start()

    def body(i, c):
        g = g_ref[0]
        t = t0 + i
        t2 = t + 2
        wraps = t2 >= n_used
        t2 = jnp.where(wraps, t2 - n_used, t2)

        @pl.when(jnp.logical_not(wraps & (j == nj - 1)))
        def _():
            in_copy(t2, (g + 2) % 3).start()

        in_copy(t, g % 3).wait()

        @pl.when(g >= 2)
        def _():
            out_copy(j, t, g % 2).wait()

        compute(g % 3, g % 2)
        out_copy(j, t, g % 2).start()
        g_ref[0] = g + 1
        return c

    lax.fori_loop(0, tc_ref[e], body, 0)

    @pl.when((j == nj - 1) & (e == n_e - 1))
    def _():
        g = g_ref[0]

        @pl.when(g >= 2)
        def _():
            out_copy(j, 0, g % 2).wait()

        @pl.when(g >= 1)
        def _():
            out_copy(j, 0, (g - 1) % 2).wait()

        obuf[0] = jnp.zeros(obuf.shape[1:], obuf.dtype)

        def zero_tile(t, c):
            for jj in range(nj):
                cp = out_copy(jj, t, 0)
                cp.start()
                cp.wait()
            return c

        lax.fori_loop(n_used, n_tiles, zero_tile, 0)


def _gate_up_kernel(ts_ref, tc_ref, xs_ref, wg_ref, wu_ref, bg_ref, bu_ref, act_ref,
                    wgb_sc, wub_sc, xbuf, obuf, g_ref, xsem, osem, *, tf, nj, n_e, n_tiles):
    def in_copy(t, slot):
        return pltpu.make_async_copy(xs_ref.at[pl.ds(t * MOE_TILE, MOE_TILE), :], xbuf.at[slot], xsem.at[slot])

    def out_copy(jj, t, slot):
        return pltpu.make_async_copy(obuf.at[slot],
                                     act_ref.at[pl.ds(t * MOE_TILE, MOE_TILE), pl.ds(jj * tf, tf)], osem.at[slot])

    @pl.when(tc_ref[pl.program_id(1)] > 0)
    def _():
        wgb_sc[...] = wg_ref[...].astype(BF16)
        wub_sc[...] = wu_ref[...].astype(BF16)

    def compute(in_slot, out_slot):
        w = xbuf[in_slot]
        half = w.shape[1]
        lo = lax.bitcast_convert_type(w << 16, F32).astype(BF16)
        hi = lax.bitcast_convert_type(w & jnp.uint32(0xFFFF0000), F32).astype(BF16)
        g = (jnp.dot(lo, wgb_sc[:half, :], preferred_element_type=F32)
             + jnp.dot(hi, wgb_sc[half:, :], preferred_element_type=F32) + bg_ref[...])
        u = (jnp.dot(lo, wub_sc[:half, :], preferred_element_type=F32)
             + jnp.dot(hi, wub_sc[half:, :], preferred_element_type=F32) + bu_ref[...])
        g = jnp.minimum(g, SWIGLU_LIMIT)
        u = jnp.clip(u, -SWIGLU_LIMIT, SWIGLU_LIMIT)
        obuf[out_slot] = ((u + 1.0) * (g * jax.nn.sigmoid(SWIGLU_ALPHA * g))).astype(obuf.dtype)

    _stream_expert_tiles(ts_ref, tc_ref, g_ref, obuf, in_copy, out_copy, compute, nj=nj, n_e=n_e, n_tiles=n_tiles)


def _gate_up(tile_start, tile_count, xs, w_gu, b_gu, *, n_tiles, tf):
    n_e, d, f2 = w_gu.shape
    f = f2 // 2
    nj = f // tf
    grid_spec = pltpu.PrefetchScalarGridSpec(
        num_scalar_prefetch=2,
        grid=(nj, n_e),
        in_specs=[pl.BlockSpec(memory_space=pl.ANY),
                  pl.BlockSpec((None, d, tf), lambda j, e, ts, tc: (e, 0, j)),
                  pl.BlockSpec((None, d, tf), lambda j, e, ts, tc: (e, 0, nj + j)),
                  pl.BlockSpec((None, 1, tf), lambda j, e, ts, tc: (e, 0, j)),
                  pl.BlockSpec((None, 1, tf), lambda j, e, ts, tc: (e, 0, nj + j))],
        out_specs=pl.BlockSpec(memory_space=pl.ANY),
        scratch_shapes=[pltpu.VMEM((d, tf), BF16), pltpu.VMEM((d, tf), BF16),
                        pltpu.VMEM((3, MOE_TILE, d // 2), jnp.uint32), pltpu.VMEM((2, MOE_TILE, tf), BF16),
                        pltpu.SMEM((1,), jnp.int32),
                        pltpu.SemaphoreType.DMA((3,)), pltpu.SemaphoreType.DMA((2,))],
    )
    return pl.pallas_call(
        functools.partial(_gate_up_kernel, tf=tf, nj=nj, n_e=n_e, n_tiles=n_tiles),
        grid_spec=grid_spec,
        out_shape=jax.ShapeDtypeStruct((n_tiles * MOE_TILE, f), BF16),
        compiler_params=_cparams(2),
        name="moe_gate_up",
    )(tile_start, tile_count, xs, w_gu, w_gu, b_gu, b_gu)


def _down_kernel(ts_ref, tc_ref, act_ref, wd_ref, bd_ref, y_ref, wdb_sc, abuf, obuf, g_ref, asem, osem,
                 *, tn, nj, n_e, n_tiles):
    def in_copy(t, slot):
        return pltpu.make_async_copy(act_ref.at[pl.ds(t * MOE_TILE, MOE_TILE), :], abuf.at[slot], asem.at[slot])

    half = tn // 2

    def out_copy(jj, t, slot):
        return pltpu.make_async_copy(obuf.at[slot],
                                     y_ref.at[pl.ds(t * MOE_TILE, MOE_TILE), pl.ds(jj * half, half)], osem.at[slot])

    @pl.when(tc_ref[pl.program_id(1)] > 0)
    def _():
        wdb_sc[...] = wd_ref[...].astype(BF16)

    def compute(in_slot, out_slot):
        y = jnp.dot(abuf[in_slot], wdb_sc[...], preferred_element_type=F32) + bd_ref[...]
        bits = lax.bitcast_convert_type(y.astype(BF16).astype(F32), jnp.uint32)
        obuf[out_slot] = (bits[:, half:] & jnp.uint32(0xFFFF0000)) | (bits[:, :half] >> 16)

    _stream_expert_tiles(ts_ref, tc_ref, g_ref, obuf, in_copy, out_copy, compute, nj=nj, n_e=n_e, n_tiles=n_tiles)


def _down(tile_start, tile_count, act, w_d, b_d, *, n_tiles, tn):
    n_e, f, d = w_d.shape
    grid_spec = pltpu.PrefetchScalarGridSpec(
        num_scalar_prefetch=2,
        grid=(d // tn, n_e),
        in_specs=[pl.BlockSpec(memory_space=pl.ANY),
                  pl.BlockSpec((None, f, tn), lambda j, e, ts, tc: (e, 0, j)),
                  pl.BlockSpec((None, 1, tn), lambda j, e, ts, tc: (e, 0, j))],
        out_specs=pl.BlockSpec(memory_space=pl.ANY),
        scratch_shapes=[pltpu.VMEM((f, tn), BF16),
                        pltpu.VMEM((3, MOE_TILE, f), BF16), pltpu.VMEM((2, MOE_TILE, tn // 2), jnp.uint32),
                        pltpu.SMEM((1,), jnp.int32),
                        pltpu.SemaphoreType.DMA((3,)), pltpu.SemaphoreType.DMA((2,))],
    )
    return pl.pallas_call(
        functools.partial(_down_kernel, tn=tn, nj=d // tn, n_e=n_e, n_tiles=n_tiles),
        grid_spec=grid_spec,
        out_shape=jax.ShapeDtypeStruct((n_tiles * MOE_TILE, d // 2), jnp.uint32),
        compiler_params=_cparams(2),
        name="moe_down",
    )(tile_start, tile_count, act, w_d, b_d)


def _combine_kernel(dest_ref, h_ref, gate_ref, g_ref, ys_ref, out_ref, buf, sem, *, tm, lp, row0, nst, pack_w):
    s = pl.program_id(0)
    n_steps = pl.num_programs(0)

    def start_gather(step, slot):
        t0 = (step // nst) * lp + row0 + (step % nst) * tm

        def start_row(r, c):
            for k in range(TOP_K):
                d = dest_ref[(t0 + r) * TOP_K + k]
                pltpu.make_async_copy(ys_ref.at[pl.ds(d, 1), :], buf.at[slot, k, pl.ds(r, 1), :],
                                      sem.at[slot]).start(priority=k % 2)
            return c

        lax.fori_loop(0, tm, start_row, 0, unroll=8)

    @pl.when(s == 0)
    def _():
        start_gather(0, 0)

    @pl.when(s + 1 < n_steps)
    def _():
        start_gather(s + 1, (s + 1) % 2)

    slot = s % 2
    for k in range(TOP_K):
        pltpu.make_async_copy(ys_ref.at[pl.ds(0, tm), :], buf.at[slot, k], sem.at[slot]).wait()

    gate = gate_ref[...]
    half = pack_w // 2
    acc = jnp.zeros(h_ref.shape, F32)
    for k in range(TOP_K):
        w = buf[slot, k]
        lo = lax.bitcast_convert_type(w << 16, F32)
        hi = lax.bitcast_convert_type(w & jnp.uint32(0xFFFF0000), F32)
        cols = []
        for c in range(w.shape[1] // half):
            cols += [lo[:, c * half:(c + 1) * half], hi[:, c * half:(c + 1) * half]]
        acc = acc + gate[:, k:k + 1] * jnp.concatenate(cols, axis=1)
    h = h_ref[...] + acc
    ms = jnp.mean(h * h, axis=-1, keepdims=True)
    out_ref[...] = (h * lax.rsqrt(ms + EPS) * g_ref[...]).astype(out_ref.dtype)


def _combine(dest_flat, h2, gate, g, ys, *, batch, seq, lp, row0, tm, pack_w):
    d = h2.shape[1]
    nb_b = lp // tm
    nb0 = row0 // tm
    nst = seq // tm
    blk = lambda s: (s // nst) * nb_b + nb0 + s % nst
    grid_spec = pltpu.PrefetchScalarGridSpec(
        num_scalar_prefetch=1,
        grid=(batch * nst,),
        in_specs=[pl.BlockSpec((tm, d), lambda s, dest: (blk(s), 0)),
                  pl.BlockSpec((tm, TOP_K), lambda s, dest: (blk(s), 0)),
                  pl.BlockSpec((1, d), lambda s, dest: (0, 0)),
                  pl.BlockSpec(memory_space=pl.ANY)],
        out_specs=pl.BlockSpec((None, tm, d), lambda s, dest: (s // nst, s % nst, 0)),
        scratch_shapes=[pltpu.VMEM((2, TOP_K, tm, d // 2), jnp.uint32), pltpu.SemaphoreType.DMA((2,))],
    )
    return pl.pallas_call(
        functools.partial(_combine_kernel, tm=tm, lp=lp, row0=row0, nst=nst, pack_w=pack_w),
        grid_spec=grid_spec,
        out_shape=jax.ShapeDtypeStruct((batch, seq, d), F32),
        compiler_params=_cparams(1),
        name="moe_combine",
    )(dest_flat, h2, gate, g.reshape(1, d).astype(F32), ys)


def _pick(pref, n):
    t = pref
    while n % t:
        t //= 2
    return t


def _row_tile(n, pref):
    t = pref // ROW_ALIGN * ROW_ALIGN
    while n % t:
        t -= ROW_ALIGN
    return t


def kernel(x, meta_tokens, norm_mix_g, w_in, b_gate, lambda_q1, lambda_k1, lambda_q2, lambda_k2, head_norm_g, w_attn_out, conv_w, conv_b, conv_ln_g, conv_ln_b, w_conv_out, b_conv_out, w_out, norm_ffn_g, w_router, b_router, w_gate_up, b_gate_up, w_down, b_down, final_norm_g):
    batch, seq, d = x.shape
    depth = w_in.shape[0]
    assert depth == 1 and seq % ROW_ALIGN == 0 and N_META <= ROW_ALIGN
    n_heads = d // 256
    hw = 2 * HEAD_DIM
    qk_w = n_heads * hw
    conv_ch = conv_w.shape[2]
    n_pad = ROW_ALIGN - N_META
    lp = n_pad + N_META + seq
    tp = batch * lp
    f = w_down.shape[2]
    layer = 0
    lam_init = 0.8 - 0.6 * math.exp(-0.3 * layer)

    h0, u = _embed_norm(x, meta_tokens, norm_mix_g[layer], n_pad=n_pad)

    proj = _matmul(u, w_in[layer], BF16, _row_tile(tp, 1536), _pick(512, w_in.shape[2]), name="in_proj")

    tq = 384 if lp % 384 == 0 else ROW_ALIGN
    vt = proj[:, 2 * qk_w:3 * qk_w].reshape(batch, lp, n_heads, hw).transpose(0, 2, 3, 1)
    o = _attention(proj, vt, lambda_q1[layer], lambda_k1[layer], lambda_q2[layer], lambda_k2[layer],
                   head_norm_g[layer], batch=batch, lp=lp, n_heads=n_heads, tq=tq, lam_init=lam_init,
                   n_pad=n_pad)
    ca_col = 3 * qk_w
    z = _conv_branch(proj, conv_w[layer], conv_b[layer], conv_ln_g[layer], conv_ln_b[layer],
                     ca_blk=ca_col // conv_ch, cg_blk=ca_col // conv_ch + 1, tm=_pick(256, tp))
    merged = _merge(o, z, proj, w_attn_out[layer], w_conv_out[layer],
                    b_conv_out[layer], b_gate[layer], gate_col=ca_col + 2 * conv_ch,
                    tm=_row_tile(tp, 768), tn=_pick(512, d))
    h2 = _matmul(merged, w_out[layer], F32, _row_tile(tp, 768), _pick(512, d), res=h0, name="out_proj")

    pos = np.arange(tp) % lp
    valid_np = pos >= n_pad
    valid = jnp.asarray(valid_np.astype(np.float32).reshape(1, tp))
    u_packed, top_e, gate_t, rank_t, cnt = _router(h2, norm_ffn_g[layer], w_router[layer], b_router[layer],
                                                   valid, tm=_pick(256, tp))
    counts = cnt[:, 0]
    padded = (counts + MOE_TILE - 1) // MOE_TILE * MOE_TILE
    e_ids = np.arange(N_EXPERTS)
    pad_end = jnp.sum(jnp.where(jnp.asarray(e_ids[None, :] <= e_ids[:, None]), padded[None, :], 0), axis=1)
    pad_start = pad_end - padded
    start_tok = jnp.sum(jnp.where(top_e[:, :, None] == jnp.asarray(e_ids, jnp.int32), pad_start, 0), axis=-1)
    n_real = int(valid_np.sum()) * TOP_K
    assert n_real >= 2 * MOE_TILE
    n_tiles =-(-(n_real + N_EXPERTS * (MOE_TILE - 1)) // MOE_TILE)
    n_slots = n_tiles * MOE_TILE
    dump = n_slots + (np.cumsum(~valid_np) - 1)[None, :] * TOP_K + np.arange(TOP_K)[:, None]
    dest_t = jnp.where(jnp.asarray(valid_np)[None, :], start_tok + rank_t, jnp.asarray(dump, jnp.int32))
    dest_flat = dest_t.T.reshape(-1).astype(jnp.int32)
    n_dump = int((~valid_np).sum()) * TOP_K
    n_used = (pad_end[-1] // MOE_TILE).astype(jnp.int32).reshape(1)
    tile_start = jnp.asarray(np.arange(n_tiles, dtype=np.int32) * MOE_TILE)
    tile_e = jnp.minimum(jnp.sum((pad_end[None, :] <= tile_start[:, None]).astype(jnp.int32), axis=1),
                         N_EXPERTS - 1).astype(jnp.int32)

    xs = _dispatch(dest_flat, tile_e, n_used, u_packed, n_slots + n_dump, tm=ROW_ALIGN, n_tiles=n_tiles)
    tile_first = (pad_start // MOE_TILE).astype(jnp.int32)
    tile_count = (padded // MOE_TILE).astype(jnp.int32)
    act = _gate_up(tile_first, tile_count, xs, w_gate_up[layer],
                   b_gate_up[layer].reshape(N_EXPERTS, 1, 2 * f).astype(F32), n_tiles=n_tiles, tf=_pick(512, f))
    down_tn = _pick(2048, d)
    ys = _down(tile_first, tile_count, act, w_down[layer],
               b_down[layer].reshape(N_EXPERTS, 1, d).astype(F32), n_tiles=n_tiles, tn=down_tn)
    return _combine(dest_flat, h2, gate_t.T, final_norm_g, ys, batch=batch, seq=seq, lp=lp,
                    row0=ROW_ALIGN, tm=ROW_ALIGN, pack_w=down_tn)
```

```python
import functools
import math

import numpy as np
import jax
import jax.numpy as jnp
from jax import lax
from jax.experimental import pallas as pl
from jax.experimental.pallas import tpu as pltpu

N_META = 16
HEAD_DIM = 64
N_EXPERTS = 32
TOP_K = 4
CONV_K = 31
EPS = 1e-5
SWIGLU_LIMIT = 7.0
SWIGLU_ALPHA = 1.702
SUBLANES = 8
LANES = 128
ROW_ALIGN = 128
CONV_HALO = 32
MOE_TILE = 256
ATTN_HEADS_PER_STEP = 4
VMEM_LIMIT = 56 * 1024 * 1024

IN_PROJ_ROWS, IN_PROJ_COLS = 1536, 512
MIX_OUT_ROWS, MIX_OUT_COLS = 768, 512
ROW_TILE = 256
ATTN_Q_TILE = 384
GATE_UP_COLS = 512
DOWN_COLS = 2048

F32 = jnp.float32
BF16 = jnp.bfloat16


def _cparams(n_axes):
    return pltpu.CompilerParams(dimension_semantics=("arbitrary",) * n_axes,
                                vmem_limit_bytes=VMEM_LIMIT)


def _embed_norm_kernel(x_ref, meta_ref, g_ref, h_ref, u_ref, *, n_pad):
    i = pl.program_id(1)

    @pl.when(i == 0)
    def _():
        h_ref[0:n_pad, :] = jnp.zeros((n_pad, h_ref.shape[1]), F32)
        h_ref[n_pad:, :] = meta_ref[...]

    @pl.when(i > 0)
    def _():
        h_ref[...] = x_ref[...]

    h = h_ref[...]
    ms = jnp.mean(h * h, axis=-1, keepdims=True)
    u_ref[...] = (h * lax.rsqrt(ms + EPS) * g_ref[...]).astype(u_ref.dtype)


def _embed_norm(x, meta, g, *, n_pad):
    batch, seq, d = x.shape
    tm = ROW_ALIGN
    nb = (n_pad + N_META + seq) // tm
    out_spec = pl.BlockSpec((tm, d), lambda b, i: (b * nb + i, 0))
    return pl.pallas_call(
        functools.partial(_embed_norm_kernel, n_pad=n_pad),
        grid=(batch, nb),
        in_specs=[pl.BlockSpec((None, tm, d), lambda b, i: (b, jnp.maximum(i - 1, 0), 0)),
                  pl.BlockSpec((N_META, d), lambda b, i: (0, 0)),
                  pl.BlockSpec((1, d), lambda b, i: (0, 0))],
        out_specs=[out_spec, out_spec],
        out_shape=[jax.ShapeDtypeStruct((batch * nb * tm, d), F32),
                   jax.ShapeDtypeStruct((batch * nb * tm, d), BF16)],
        compiler_params=_cparams(2),
        name="embed_norm",
    )(x, meta.astype(F32), g.reshape(1, d).astype(F32))


def _cast_weight_once(w_ref, wb_sc):
    @pl.when(pl.program_id(1) == 0)
    def _():
        wb_sc[...] = w_ref[...].astype(BF16)


def _matmul_kernel(a_ref, w_ref, o_ref, wb_sc):
    _cast_weight_once(w_ref, wb_sc)
    o_ref[...] = jnp.dot(a_ref[...], wb_sc[...], preferred_element_type=F32).astype(o_ref.dtype)


def _matmul_res_kernel(a_ref, w_ref, r_ref, o_ref, wb_sc):
    _cast_weight_once(w_ref, wb_sc)
    acc = jnp.dot(a_ref[...], wb_sc[...], preferred_element_type=F32)
    o_ref[...] = (acc + r_ref[...]).astype(o_ref.dtype)


def _matmul(a, w, out_dtype, tm, tn, res=None, name="matmul"):
    m, k = a.shape
    n = w.shape[1]
    in_specs = [pl.BlockSpec((tm, k), lambda j, i: (i, 0)),
                pl.BlockSpec((k, tn), lambda j, i: (0, j))]
    args = [a, w]
    kern = _matmul_kernel
    if res is not None:
        in_specs.append(pl.BlockSpec((tm, tn), lambda j, i: (i, j)))
        args.append(res)
        kern = _matmul_res_kernel
    return pl.pallas_call(
        kern,
        grid=(n // tn, m // tm),
        in_specs=in_specs,
        out_specs=pl.BlockSpec((tm, tn), lambda j, i: (i, j)),
        out_shape=jax.ShapeDtypeStruct((m, n), out_dtype),
        scratch_shapes=[pltpu.VMEM((k, tn), BF16)],
        compiler_params=_cparams(2),
        name=name,
    )(*args)


def _attn_kernel(lq1_ref, lk1_ref, lq2_ref, lk2_ref, hg_ref, bias_ref, q_ref, k_ref, vt_ref, o_ref,
                 q12_sc, s_sc, m_sc, l_sc, acc_sc, *, tq, lam_init, n_hd):
    hw = 2 * HEAD_DIM
    nq = q_ref.shape[0] // tq
    lam = (jnp.exp(jnp.sum(lq1_ref[...] * lk1_ref[...], axis=-1, keepdims=True))
           - jnp.exp(jnp.sum(lq2_ref[...] * lk2_ref[...], axis=-1, keepdims=True)) + lam_init)

    def q_tile(qi, carry):
        q0 = pl.multiple_of(qi * tq, tq)
        for hd in range(n_hd):
            q = (q_ref[pl.ds(q0, tq), hd * hw:(hd + 1) * hw].astype(F32)
                 * (HEAD_DIM ** -0.5 * math.log2(math.e))).astype(BF16)
            lane = lax.broadcasted_iota(jnp.int32, q.shape, 1)
            zero = jnp.zeros_like(q)
            q12_sc[2 * hd] = jnp.where(lane < HEAD_DIM, q, zero)
            q12_sc[2 * hd + 1] = jnp.where(lane >= HEAD_DIM, q, zero)
        m_sc[...] = jnp.full(m_sc.shape, -jnp.inf, F32)
        l_sc[...] = jnp.zeros(l_sc.shape, F32)
        acc_sc[...] = jnp.zeros(acc_sc.shape, F32)

        def scores(kj, c):
            hd = c // 2
            k0 = pl.multiple_of(kj * tq, tq)
            s = lax.dot_general(k_ref[pl.ds(k0, tq), hd * hw:(hd + 1) * hw], q12_sc[c],
                                (((1,), (1,)), ((), ())), preferred_element_type=F32)
            kind = jnp.where(kj == 0, 1, 0) + jnp.where(kj == qi, 2, 0)
            s_sc[c] = s + bias_ref[kind]

        def update(kj, c):
            hd = c // 2
            k0 = pl.multiple_of(kj * tq, tq)
            s = s_sc[c]
            m_prev = m_sc[c]
            m_new = jnp.maximum(m_prev, jnp.max(s, axis=0, keepdims=True))
            alpha = jnp.exp2(m_prev - m_new)
            p = jnp.exp2(s - m_new)
            l_sc[c] = alpha * l_sc[c] + jnp.sum(p.reshape(tq // 8, 8, tq), axis=0)
            acc_sc[c] = alpha * acc_sc[c] + jnp.dot(vt_ref[hd, :, pl.ds(k0, tq)], p.astype(BF16),
                                                    preferred_element_type=F32)
            m_sc[c] = m_new

        for hd in range(n_hd):
            scores(0, 2 * hd)

        def body(kj, c):
            for hd in range(n_hd):
                scores(kj, 2 * hd + 1)
                update(kj, 2 * hd)
                scores(kj + 1, 2 * hd)
            for hd in range(n_hd):
                update(kj, 2 * hd + 1)
            return c

        lax.fori_loop(0, qi, body, 0)
        for hd in range(n_hd):
            scores(qi, 2 * hd + 1)
            update(qi, 2 * hd)
        for hd in range(n_hd):
            update(qi, 2 * hd + 1)

        for hd in range(n_hd):
            o1 = acc_sc[2 * hd] / jnp.sum(l_sc[2 * hd], axis=0, keepdims=True)
            o2 = acc_sc[2 * hd + 1] / jnp.sum(l_sc[2 * hd + 1], axis=0, keepdims=True)
            o = o1 - lam * o2
            ms = jnp.mean(o * o, axis=0, keepdims=True)
            o = o * lax.rsqrt(ms + EPS) * hg_ref[...] * (1.0 - lam_init)
            o_ref[pl.ds(q0, tq), hd * hw:(hd + 1) * hw] = o.T.astype(o_ref.dtype)
        return carry

    lax.fori_loop(0, nq, q_tile, 0)


def _attn_bias(tq, n_pad):
    neg = np.float32(np.finfo(np.float32).min)
    r = np.arange(tq)[:, None]
    c = np.arange(tq)[None, :]
    pad = np.broadcast_to(r < n_pad, (tq, tq))
    future = r > c
    tiles = [np.zeros((tq, tq), bool), pad, future, pad | future]
    return jnp.asarray(np.stack([np.where(t, neg, np.float32(0)) for t in tiles]).astype(np.float32))


def _attention(proj, vt, lq1, lk1, lq2, lk2, head_g, *, batch, lp, n_heads, tq, lam_init, n_pad):
    hw = 2 * HEAD_DIM
    n_hd = ATTN_HEADS_PER_STEP if n_heads % ATTN_HEADS_PER_STEP == 0 else 1
    assert n_pad <= tq
    koff = n_heads // n_hd
    vec = lambda a: a.reshape(1, -1).astype(F32)
    small = lambda n: pl.BlockSpec((1, n), lambda b, h: (0, 0))
    return pl.pallas_call(
        functools.partial(_attn_kernel, tq=tq, lam_init=lam_init, n_hd=n_hd),
        grid=(batch, n_heads // n_hd),
        in_specs=[small(HEAD_DIM), small(HEAD_DIM), small(HEAD_DIM), small(HEAD_DIM),
                  pl.BlockSpec((hw, 1), lambda b, h: (0, 0)),
                  pl.BlockSpec((4, tq, tq), lambda b, h: (0, 0, 0)),
                  pl.BlockSpec((lp, n_hd * hw), lambda b, h: (b, h)),
                  pl.BlockSpec((lp, n_hd * hw), lambda b, h: (b, koff + h)),
                  pl.BlockSpec((None, n_hd, hw, lp), lambda b, h: (b, h, 0, 0))],
        out_specs=pl.BlockSpec((lp, n_hd * hw), lambda b, h: (b, h)),
        out_shape=jax.ShapeDtypeStruct((batch * lp, n_heads * hw), BF16),
        scratch_shapes=[pltpu.VMEM((2 * n_hd, tq, hw), BF16),
                        pltpu.VMEM((2 * n_hd, tq, tq), F32),
                        pltpu.VMEM((2 * n_hd, 1, tq), F32), pltpu.VMEM((2 * n_hd, 8, tq), F32),
                        pltpu.VMEM((2 * n_hd, hw, tq), F32)],
        compiler_params=_cparams(2),
        name="diff_attention",
    )(vec(lq1), vec(lk1), vec(lq2), vec(lk2), head_g.reshape(hw, 1).astype(F32), _attn_bias(tq, n_pad),
      proj, proj, vt)


def _conv_kernel(ca_ref, cg_ref, ca_h_ref, cg_h_ref, w_ref, b_ref, lg_ref, lb_ref, z_ref,
                 ext_sc, sh_sc, y_sc, *, tm):
    i = pl.program_id(0)
    n_ch = w_ref.shape[1]
    n_ext = CONV_HALO + tm
    glu = lambda a, g: a.astype(F32) * jax.nn.sigmoid(g.astype(F32))
    halo = glu(ca_h_ref[...], cg_h_ref[...])
    ext_sc[0:CONV_HALO, :] = jnp.where(i > 0, halo, jnp.zeros_like(halo))
    ext_sc[CONV_HALO:n_ext, :] = glu(ca_ref[...], cg_ref[...])
    ext_sc[n_ext:n_ext + SUBLANES, :] = jnp.zeros((SUBLANES, n_ch), F32)
    base = CONV_HALO - (CONV_K - 1)

    def slab(lc, carry):
        l0 = pl.multiple_of(lc * LANES, LANES)
        for rho in range(SUBLANES):
            sh_sc[rho] = ext_sc[rho:rho + n_ext, pl.ds(l0, LANES)]
        acc = jnp.zeros((tm, LANES), F32) + b_ref[:, pl.ds(l0, LANES)]
        for j in range(CONV_K):
            rho = (base + j) % SUBLANES
            a = base + j - rho
            acc = acc + w_ref[j:j + 1, pl.ds(l0, LANES)] * sh_sc[rho, a:a + tm, :]
        y_sc[:, pl.ds(l0, LANES)] = acc
        return carry

    lax.fori_loop(0, n_ch // LANES, slab, 0)
    acc = y_sc[...]
    mu = jnp.mean(acc, axis=-1, keepdims=True)
    d = acc - mu
    var = jnp.mean(d * d, axis=-1, keepdims=True)
    y = d * lax.rsqrt(var + EPS) * lg_ref[...] + lb_ref[...]
    z_ref[...] = (y * jax.nn.sigmoid(y)).astype(z_ref.dtype)


def _conv_branch(proj, conv_w, conv_b, ln_g, ln_b, *, ca_blk, cg_blk, tm):
    m = proj.shape[0]
    c = conv_w.shape[1]
    hb = tm // CONV_HALO
    row = lambda a: a.reshape(1, c).astype(F32)
    vec = pl.BlockSpec((1, c), lambda i: (0, 0))
    return pl.pallas_call(
        functools.partial(_conv_kernel, tm=tm),
        grid=(m // tm,),
        in_specs=[pl.BlockSpec((tm, c), lambda i: (i, ca_blk)),
                  pl.BlockSpec((tm, c), lambda i: (i, cg_blk)),
                  pl.BlockSpec((CONV_HALO, c), lambda i: (jnp.maximum(i * hb - 1, 0), ca_blk)),
                  pl.BlockSpec((CONV_HALO, c), lambda i: (jnp.maximum(i * hb - 1, 0), cg_blk)),
                  pl.BlockSpec((CONV_K, c), lambda i: (0, 0)),
                  vec, vec, vec],
        out_specs=pl.BlockSpec((tm, c), lambda i: (i, 0)),
        out_shape=jax.ShapeDtypeStruct((m, c), BF16),
        scratch_shapes=[pltpu.VMEM((CONV_HALO + tm + SUBLANES, c), F32),
                        pltpu.VMEM((SUBLANES, CONV_HALO + tm, LANES), F32),
                        pltpu.VMEM((tm, c), F32)],
        compiler_params=_cparams(1),
        name="conformer_conv",
    )(proj, proj, proj, proj, conv_w.astype(F32), row(conv_b), row(ln_g), row(ln_b))


def _merge_kernel(o_ref, z_ref, wa_ref, wc_ref, bc_ref, g1_ref, g2_ref, bg1_ref, bg2_ref, out_ref,
                  wab_sc, wcb_sc):
    _cast_weight_once(wa_ref, wab_sc)
    _cast_weight_once(wc_ref, wcb_sc)
    ya = jnp.dot(o_ref[...], wab_sc[...], preferred_element_type=F32)
    yc = jnp.dot(z_ref[...], wcb_sc[...], preferred_element_type=F32) + bc_ref[...]
    g1 = jax.nn.sigmoid(g1_ref[...].astype(F32) + bg1_ref[...])
    g2 = jax.nn.sigmoid(g2_ref[...].astype(F32) + bg2_ref[...])
    out_ref[...] = (g1 * ya + g2 * yc).astype(out_ref.dtype)


def _merge(o, z, proj, wa, wc, bc, bg, *, gate_col, tm, tn):
    m, ka = o.shape
    kc = z.shape[1]
    d = wa.shape[1]
    g1_blk = gate_col // tn
    g2_blk = (gate_col + d) // tn
    nb = d // tn
    bg2 = bg.reshape(1, 2 * d).astype(F32)
    return pl.pallas_call(
        _merge_kernel,
        grid=(nb, m // tm),
        in_specs=[pl.BlockSpec((tm, ka), lambda j, i: (i, 0)),
                  pl.BlockSpec((tm, kc), lambda j, i: (i, 0)),
                  pl.BlockSpec((ka, tn), lambda j, i: (0, j)),
                  pl.BlockSpec((kc, tn), lambda j, i: (0, j)),
                  pl.BlockSpec((1, tn), lambda j, i: (0, j)),
                  pl.BlockSpec((tm, tn), lambda j, i: (i, g1_blk + j)),
                  pl.BlockSpec((tm, tn), lambda j, i: (i, g2_blk + j)),
                  pl.BlockSpec((1, tn), lambda j, i: (0, j)),
                  pl.BlockSpec((1, tn), lambda j, i: (0, nb + j))],
        out_specs=pl.BlockSpec((tm, tn), lambda j, i: (i, j)),
        out_shape=jax.ShapeDtypeStruct((m, d), BF16),
        scratch_shapes=[pltpu.VMEM((ka, tn), BF16), pltpu.VMEM((kc, tn), BF16)],
        compiler_params=_cparams(2),
        name="mixer_merge",
    )(o, z, wa, wc, bc.reshape(1, d).astype(F32), proj, proj, bg2, bg2)


def _router_kernel(h_ref, g_ref, wr_ref, br_ref, valid_ref,
                   up_ref, e_ref, gate_ref, rank_ref, cnt_ref, carry_sc, *, tm):
    i = pl.program_id(0)

    @pl.when(i == 0)
    def _():
        carry_sc[...] = jnp.zeros(carry_sc.shape, F32)

    h = h_ref[...]
    ms = jnp.mean(h * h, axis=-1, keepdims=True)
    u = h * lax.rsqrt(ms + EPS) * g_ref[...]

    half = u.shape[1] // 2
    bits = lax.bitcast_convert_type(u.astype(BF16).astype(F32), jnp.uint32)
    up_ref[...] = (bits[:, half:] & jnp.uint32(0xFFFF0000)) | (bits[:, :half] >> 16)

    logits = lax.dot_general(wr_ref[...], u, (((1,), (1,)), ((), ())),
                             precision=lax.Precision.HIGHEST,
                             preferred_element_type=F32) + br_ref[...]
    n_e = logits.shape[0]
    eiota = lax.broadcasted_iota(jnp.int32, logits.shape, 0).astype(F32)
    work = logits
    sel = jnp.zeros(logits.shape, jnp.bool_)
    top_l, top_e = [], []
    for _ in range(TOP_K):
        mx = jnp.max(work, axis=0, keepdims=True)
        idx = jnp.min(jnp.where(work == mx, eiota, float(n_e)), axis=0, keepdims=True)
        hit = eiota == idx
        top_l.append(mx)
        top_e.append(idx)
        sel = sel | hit
        work = jnp.where(hit, -jnp.inf, work)
    ex = [jnp.exp(t - top_l[0]) for t in top_l]
    den = ex[0] + ex[1] + ex[2] + ex[3]
    gate_ref[...] = jnp.concatenate([e / den for e in ex], axis=0)
    e_ref[...] = jnp.concatenate(top_e, axis=0).astype(jnp.int32)

    selv = jnp.where(sel & (valid_ref[...] > 0.0), 1.0, 0.0)
    before = (lax.broadcasted_iota(jnp.int32, (tm, tm), 0)
              < lax.broadcasted_iota(jnp.int32, (tm, tm), 1)).astype(BF16)
    rank_all = jnp.dot(selv.astype(BF16), before, preferred_element_type=F32) + carry_sc[...]
    ranks = [jnp.sum(jnp.where(eiota == idx, rank_all, 0.0), axis=0, keepdims=True) for idx in top_e]
    rank_ref[...] = jnp.concatenate(ranks, axis=0).astype(jnp.int32)
    carry = carry_sc[...] + jnp.sum(selv, axis=1, keepdims=True)
    carry_sc[...] = carry
    cnt_ref[...] = jnp.broadcast_to(carry, cnt_ref.shape).astype(jnp.int32)


def _router(h2, g, w_router, b_router, valid, *, tm):
    m, d = h2.shape
    n_e = w_router.shape[1]
    tok = lambda dt: jax.ShapeDtypeStruct((TOP_K, m), dt)
    tok_spec = pl.BlockSpec((TOP_K, tm), lambda i: (0, i))
    return pl.pallas_call(
        functools.partial(_router_kernel, tm=tm),
        grid=(m // tm,),
        in_specs=[pl.BlockSpec((tm, d), lambda i: (i, 0)),
                  pl.BlockSpec((1, d), lambda i: (0, 0)),
                  pl.BlockSpec((n_e, d), lambda i: (0, 0)),
                  pl.BlockSpec((n_e, 1), lambda i: (0, 0)),
                  pl.BlockSpec((1, tm), lambda i: (0, i))],
        out_specs=[pl.BlockSpec((tm, d // 2), lambda i: (i, 0)),
                   tok_spec, tok_spec, tok_spec,
                   pl.BlockSpec((n_e, 128), lambda i: (0, 0))],
        out_shape=[jax.ShapeDtypeStruct((m, d // 2), jnp.uint32),
                   tok(jnp.int32), tok(F32), tok(jnp.int32),
                   jax.ShapeDtypeStruct((n_e, 128), jnp.int32)],
        scratch_shapes=[pltpu.VMEM((n_e, 1), F32)],
        compiler_params=_cparams(1),
        name="router",
    )(h2, g.reshape(1, d).astype(F32), w_router.T.astype(F32), b_router.reshape(n_e, 1).astype(F32), valid)


def _dispatch_kernel(dest_ref, te_ref, nu_ref, u_ref, xs_ref, zero_sc, sem, zsem, *, tm, n_tiles):
    i = pl.program_id(0)

    @pl.when(i == 0)
    def _():
        zero_sc[...] = jnp.zeros(zero_sc.shape, zero_sc.dtype)
        nu = nu_ref[0]

        def partly_filled(t):
            nxt = te_ref[jnp.minimum(t + 1, n_tiles - 1)]
            return (t >= nu - 1) | (te_ref[t] != nxt)

        def tile_copy(t):
            return pltpu.make_async_copy(zero_sc, xs_ref.at[pl.ds(t * MOE_TILE, MOE_TILE), :], zsem)

        def start(t, c):
            @pl.when(partly_filled(t))
            def _():
                tile_copy(t).start()
            return c

        def wait(t, c):
            @pl.when(partly_filled(t))
            def _():
                tile_copy(t).wait()
            return c

        lax.fori_loop(0, n_tiles, start, 0)
        lax.fori_loop(0, n_tiles, wait, 0)

    def start_row(r, c):
        for k in range(TOP_K):
            d = dest_ref[(i * tm + r) * TOP_K + k]
            pltpu.make_async_copy(u_ref.at[pl.ds(r, 1), :], xs_ref.at[pl.ds(d, 1), :], sem).start(priority=k % 2)
        return c

    lax.fori_loop(0, tm, start_row, 0, unroll=8)
    for k in range(TOP_K):
        pltpu.make_async_copy(u_ref, xs_ref.at[pl.ds(0, tm), :], sem).wait()


def _dispatch(dest_flat, tile_e, n_used, u_packed, n_rows, *, tm, n_tiles):
    m, w = u_packed.shape
    grid_spec = pltpu.PrefetchScalarGridSpec(
        num_scalar_prefetch=3,
        grid=(m // tm,),
        in_specs=[pl.BlockSpec((tm, w), lambda i, dest, te, nu: (i, 0))],
        out_specs=pl.BlockSpec(memory_space=pl.ANY),
        scratch_shapes=[pltpu.VMEM((MOE_TILE, w), jnp.uint32),
                        pltpu.SemaphoreType.DMA(()), pltpu.SemaphoreType.DMA(())],
    )
    return pl.pallas_call(
        functools.partial(_dispatch_kernel, tm=tm, n_tiles=n_tiles),
        grid_spec=grid_spec,
        out_shape=jax.ShapeDtypeStruct((n_rows, w), jnp.uint32),
        compiler_params=_cparams(1),
        name="moe_dispatch",
    )(dest_flat, tile_e, n_used, u_packed)


def _stream_expert_tiles(ts_ref, tc_ref, g_ref, obuf, in_copy, out_copy, compute, *, nj, n_e, n_tiles):
    j = pl.program_id(0)
    e = pl.program_id(1)
    n_used = ts_ref[n_e - 1] + tc_ref[n_e - 1]
    t0 = ts_ref[e]

    @pl.when((j == 0) & (e == 0))
    def _():
        g_ref[0] = 0
        in_copy(0, 0).start()
        in_copy(1, 1).start()

    def body(i, c):
        g = g_ref[0]
        t = t0 + i
        t2 = t + 2
        wraps = t2 >= n_used
        t2 = jnp.where(wraps, t2 - n_used, t2)

        @pl.when(jnp.logical_not(wraps & (j == nj - 1)))
        def _():
            in_copy(t2, (g + 2) % 3).start()

        in_copy(t, g % 3).wait()

        @pl.when(g >= 2)
        def _():
            out_copy(j, t, g % 2).wait()

        compute(g % 3, g % 2)
        out_copy(j, t, g % 2).start()
        g_ref[0] = g + 1
        return c

    lax.fori_loop(0, tc_ref[e], body, 0)

    @pl.when((j == nj - 1) & (e == n_e - 1))
    def _():
        g = g_ref[0]

        @pl.when(g >= 2)
        def _():
            out_copy(j, 0, g % 2).wait()

        @pl.when(g >= 1)
        def _():
            out_copy(j, 0, (g - 1) % 2).wait()

        obuf[0] = jnp.zeros(obuf.shape[1:], obuf.dtype)

        def zero_tile(t, c):
            for jj in range(nj):
                cp = out_copy(jj, t, 0)
                cp.start()
                cp.wait()
            return c

        lax.fori_loop(n_used, n_tiles, zero_tile, 0)


def _gate_up_kernel(ts_ref, tc_ref, xs_ref, wg_ref, wu_ref, bg_ref, bu_ref, act_ref,
                    wgb_sc, wub_sc, xbuf, obuf, g_ref, xsem, osem, *, tf, nj, n_e, n_tiles):
    def in_copy(t, slot):
        return pltpu.make_async_copy(xs_ref.at[pl.ds(t * MOE_TILE, MOE_TILE), :], xbuf.at[slot], xsem.at[slot])

    def out_copy(jj, t, slot):
        return pltpu.make_async_copy(obuf.at[slot],
                                     act_ref.at[pl.ds(t * MOE_TILE, MOE_TILE), pl.ds(jj * tf, tf)], osem.at[slot])

    @pl.when(tc_ref[pl.program_id(1)] > 0)
    def _():
        wgb_sc[...] = wg_ref[...].astype(BF16)
        wub_sc[...] = wu_ref[...].astype(BF16)

    def compute(in_slot, out_slot):
        w = xbuf[in_slot]
        half = w.shape[1]
        lo = lax.bitcast_convert_type(w << 16, F32).astype(BF16)
        hi = lax.bitcast_convert_type(w & jnp.uint32(0xFFFF0000), F32).astype(BF16)
        g = (jnp.dot(lo, wgb_sc[:half, :], preferred_element_type=F32)
             + jnp.dot(hi, wgb_sc[half:, :], preferred_element_type=F32) + bg_ref[...])
        u = (jnp.dot(lo, wub_sc[:half, :], preferred_element_type=F32)
             + jnp.dot(hi, wub_sc[half:, :], preferred_element_type=F32) + bu_ref[...])
        g = jnp.minimum(g, SWIGLU_LIMIT)
        u = jnp.clip(u, -SWIGLU_LIMIT, SWIGLU_LIMIT)
        obuf[out_slot] = ((u + 1.0) * (g * jax.nn.sigmoid(SWIGLU_ALPHA * g))).astype(obuf.dtype)

    _stream_expert_tiles(ts_ref, tc_ref, g_ref, obuf, in_copy, out_copy, compute, nj=nj, n_e=n_e, n_tiles=n_tiles)


def _gate_up(tile_start, tile_count, xs, w_gu, b_gu, *, n_tiles, tf):
    n_e, d, f2 = w_gu.shape
    f = f2 // 2
    nj = f // tf
    grid_spec = pltpu.PrefetchScalarGridSpec(
        num_scalar_prefetch=2,
        grid=(nj, n_e),
        in_specs=[pl.BlockSpec(memory_space=pl.ANY),
                  pl.BlockSpec((None, d, tf), lambda j, e, ts, tc: (e, 0, j)),
                  pl.BlockSpec((None, d, tf), lambda j, e, ts, tc: (e, 0, nj + j)),
                  pl.BlockSpec((None, 1, tf), lambda j, e, ts, tc: (e, 0, j)),
                  pl.BlockSpec((None, 1, tf), lambda j, e, ts, tc: (e, 0, nj + j))],
        out_specs=pl.BlockSpec(memory_space=pl.ANY),
        scratch_shapes=[pltpu.VMEM((d, tf), BF16), pltpu.VMEM((d, tf), BF16),
                        pltpu.VMEM((3, MOE_TILE, d // 2), jnp.uint32), pltpu.VMEM((2, MOE_TILE, tf), BF16),
                        pltpu.SMEM((1,), jnp.int32),
                        pltpu.SemaphoreType.DMA((3,)), pltpu.SemaphoreType.DMA((2,))],
    )
    return pl.pallas_call(
        functools.partial(_gate_up_kernel, tf=tf, nj=nj, n_e=n_e, n_tiles=n_tiles),
        grid_spec=grid_spec,
        out_shape=jax.ShapeDtypeStruct((n_tiles * MOE_TILE, f), BF16),
        compiler_params=_cparams(2),
        name="moe_gate_up",
    )(tile_start, tile_count, xs, w_gu, w_gu, b_gu, b_gu)


def _down_kernel(ts_ref, tc_ref, act_ref, wd_ref, bd_ref, y_ref, wdb_sc, abuf, obuf, g_ref, asem, osem,
                 *, tn, nj, n_e, n_tiles):
    def in_copy(t, slot):
        return pltpu.make_async_copy(act_ref.at[pl.ds(t * MOE_TILE, MOE_TILE), :], abuf.at[slot], asem.at[slot])

    half = tn // 2

    def out_copy(jj, t, slot):
        return pltpu.make_async_copy(obuf.at[slot],
                                     y_ref.at[pl.ds(t * MOE_TILE, MOE_TILE), pl.ds(jj * half, half)], osem.at[slot])

    @pl.when(tc_ref[pl.program_id(1)] > 0)
    def _():
        wdb_sc[...] = wd_ref[...].astype(BF16)

    def compute(in_slot, out_slot):
        y = jnp.dot(abuf[in_slot], wdb_sc[...], preferred_element_type=F32) + bd_ref[...]
        bits = lax.bitcast_convert_type(y.astype(BF16).astype(F32), jnp.uint32)
        obuf[out_slot] = (bits[:, half:] & jnp.uint32(0xFFFF0000)) | (bits[:, :half] >> 16)

    _stream_expert_tiles(ts_ref, tc_ref, g_ref, obuf, in_copy, out_copy, compute, nj=nj, n_e=n_e, n_tiles=n_tiles)


def _down(tile_start, tile_count, act, w_d, b_d, *, n_tiles, tn):
    n_e, f, d = w_d.shape
    grid_spec = pltpu.PrefetchScalarGridSpec(
        num_scalar_prefetch=2,
        grid=(d // tn, n_e),
        in_specs=[pl.BlockSpec(memory_space=pl.ANY),
                  pl.BlockSpec((None, f, tn), lambda j, e, ts, tc: (e, 0, j)),
                  pl.BlockSpec((None, 1, tn), lambda j, e, ts, tc: (e, 0, j))],
        out_specs=pl.BlockSpec(memory_space=pl.ANY),
        scratch_shapes=[pltpu.VMEM((f, tn), BF16),
                        pltpu.VMEM((3, MOE_TILE, f), BF16), pltpu.VMEM((2, MOE_TILE, tn // 2), jnp.uint32),
                        pltpu.SMEM((1,), jnp.int32),
                        pltpu.SemaphoreType.DMA((3,)), pltpu.SemaphoreType.DMA((2,))],
    )
    return pl.pallas_call(
        functools.partial(_down_kernel, tn=tn, nj=d // tn, n_e=n_e, n_tiles=n_tiles),
        grid_spec=grid_spec,
        out_shape=jax.ShapeDtypeStruct((n_tiles * MOE_TILE, d // 2), jnp.uint32),
        compiler_params=_cparams(2),
        name="moe_down",
    )(tile_start, tile_count, act, w_d, b_d)


def _combine_kernel(dest_ref, h_ref, gate_ref, g_ref, ys_ref, out_ref, buf, sem, *, tm, lp, row0, nst, pack_w):
    s = pl.program_id(0)
    n_steps = pl.num_programs(0)

    def start_gather(step, slot):
        t0 = (step // nst) * lp + row0 + (step % nst) * tm

        def start_row(r, c):
            for k in range(TOP_K):
                d = dest_ref[(t0 + r) * TOP_K + k]
                pltpu.make_async_copy(ys_ref.at[pl.ds(d, 1), :], buf.at[slot, k, pl.ds(r, 1), :],
                                      sem.at[slot]).start(priority=k % 2)
            return c

        lax.fori_loop(0, tm, start_row, 0, unroll=8)

    @pl.when(s == 0)
    def _():
        start_gather(0, 0)

    @pl.when(s + 1 < n_steps)
    def _():
        start_gather(s + 1, (s + 1) % 2)

    slot = s % 2
    for k in range(TOP_K):
        pltpu.make_async_copy(ys_ref.at[pl.ds(0, tm), :], buf.at[slot, k], sem.at[slot]).wait()

    gate = gate_ref[...]
    half = pack_w // 2
    acc = jnp.zeros(h_ref.shape, F32)
    for k in range(TOP_K):
        w = buf[slot, k]
        lo = lax.bitcast_convert_type(w << 16, F32)
        hi = lax.bitcast_convert_type(w & jnp.uint32(0xFFFF0000), F32)
        cols = []
        for c in range(w.shape[1] // half):
            cols += [lo[:, c * half:(c + 1) * half], hi[:, c * half:(c + 1) * half]]
        acc = acc + gate[:, k:k + 1] * jnp.concatenate(cols, axis=1)
    h = h_ref[...] + acc
    ms = jnp.mean(h * h, axis=-1, keepdims=True)
    out_ref[...] = (h * lax.rsqrt(ms + EPS) * g_ref[...]).astype(out_ref.dtype)


def _combine(dest_flat, h2, gate, g, ys, *, batch, seq, lp, row0, tm, pack_w):
    d = h2.shape[1]
    nb_b = lp // tm
    nb0 = row0 // tm
    nst = seq // tm
    blk = lambda s: (s // nst) * nb_b + nb0 + s % nst
    grid_spec = pltpu.PrefetchScalarGridSpec(
        num_scalar_prefetch=1,
        grid=(batch * nst,),
        in_specs=[pl.BlockSpec((tm, d), lambda s, dest: (blk(s), 0)),
                  pl.BlockSpec((tm, TOP_K), lambda s, dest: (blk(s), 0)),
                  pl.BlockSpec((1, d), lambda s, dest: (0, 0)),
                  pl.BlockSpec(memory_space=pl.ANY)],
        out_specs=pl.BlockSpec((None, tm, d), lambda s, dest: (s // nst, s % nst, 0)),
        scratch_shapes=[pltpu.VMEM((2, TOP_K, tm, d // 2), jnp.uint32), pltpu.SemaphoreType.DMA((2,))],
    )
    return pl.pallas_call(
        functools.partial(_combine_kernel, tm=tm, lp=lp, row0=row0, nst=nst, pack_w=pack_w),
        grid_spec=grid_spec,
        out_shape=jax.ShapeDtypeStruct((batch, seq, d), F32),
        compiler_params=_cparams(1),
        name="moe_combine",
    )(dest_flat, h2, gate, g.reshape(1, d).astype(F32), ys)


def _pick(pref, n):
    t = pref
    while n % t:
        t //= 2
    return t


def _row_tile(n, pref):
    t = pref // ROW_ALIGN * ROW_ALIGN
    while n % t:
        t -= ROW_ALIGN
    return t


def kernel(x, meta_tokens, norm_mix_g, w_in, b_gate, lambda_q1, lambda_k1, lambda_q2, lambda_k2, head_norm_g, w_attn_out, conv_w, conv_b, conv_ln_g, conv_ln_b, w_conv_out, b_conv_out, w_out, norm_ffn_g, w_router, b_router, w_gate_up, b_gate_up, w_down, b_down, final_norm_g):
    batch, seq, d = x.shape
    depth = w_in.shape[0]
    assert depth == 1 and seq % ROW_ALIGN == 0 and N_META <= ROW_ALIGN
    n_heads = d // 256
    hw = 2 * HEAD_DIM
    qk_w = n_heads * hw
    conv_ch = conv_w.shape[2]
    n_pad = ROW_ALIGN - N_META
    lp = n_pad + N_META + seq
    tp = batch * lp
    f = w_down.shape[2]
    layer = 0
    lam_init = 0.8 - 0.6 * math.exp(-0.3 * layer)

    h0, u = _embed_norm(x, meta_tokens, norm_mix_g[layer], n_pad=n_pad)

    proj = _matmul(u, w_in[layer], BF16, _row_tile(tp, IN_PROJ_ROWS), _pick(IN_PROJ_COLS, w_in.shape[2]),
                   name="in_proj")

    tq = ATTN_Q_TILE if lp % ATTN_Q_TILE == 0 else ROW_ALIGN
    vt = proj[:, 2 * qk_w:3 * qk_w].reshape(batch, lp, n_heads, hw).transpose(0, 2, 3, 1)
    o = _attention(proj, vt, lambda_q1[layer], lambda_k1[layer], lambda_q2[layer], lambda_k2[layer],
                   head_norm_g[layer], batch=batch, lp=lp, n_heads=n_heads, tq=tq, lam_init=lam_init,
                   n_pad=n_pad)
    ca_col = 3 * qk_w
    z = _conv_branch(proj, conv_w[layer], conv_b[layer], conv_ln_g[layer], conv_ln_b[layer],
                     ca_blk=ca_col // conv_ch, cg_blk=ca_col // conv_ch + 1, tm=_pick(ROW_TILE, tp))
    merged = _merge(o, z, proj, w_attn_out[layer], w_conv_out[layer],
                    b_conv_out[layer], b_gate[layer], gate_col=ca_col + 2 * conv_ch,
                    tm=_row_tile(tp, MIX_OUT_ROWS), tn=_pick(MIX_OUT_COLS, d))
    h2 = _matmul(merged, w_out[layer], F32, _row_tile(tp, MIX_OUT_ROWS), _pick(MIX_OUT_COLS, d), res=h0,
                 name="out_proj")

    pos = np.arange(tp) % lp
    valid_np = pos >= n_pad
    valid = jnp.asarray(valid_np.astype(np.float32).reshape(1, tp))
    u_packed, top_e, gate_t, rank_t, cnt = _router(h2, norm_ffn_g[layer], w_router[layer], b_router[layer],
                                                   valid, tm=_pick(ROW_TILE, tp))
    counts = cnt[:, 0]
    padded = (counts + MOE_TILE - 1) // MOE_TILE * MOE_TILE
    e_ids = np.arange(N_EXPERTS)
    pad_end = jnp.sum(jnp.where(jnp.asarray(e_ids[None, :] <= e_ids[:, None]), padded[None, :], 0), axis=1)
    pad_start = pad_end - padded
    start_tok = jnp.sum(jnp.where(top_e[:, :, None] == jnp.asarray(e_ids, jnp.int32), pad_start, 0), axis=-1)
    n_real = int(valid_np.sum()) * TOP_K
    assert n_real >= 2 * MOE_TILE
    n_tiles = -(-(n_real + N_EXPERTS * (MOE_TILE - 1)) // MOE_TILE)
    n_slots = n_tiles * MOE_TILE
    dump = n_slots + (np.cumsum(~valid_np) - 1)[None, :] * TOP_K + np.arange(TOP_K)[:, None]
    dest_t = jnp.where(jnp.asarray(valid_np)[None, :], start_tok + rank_t, jnp.asarray(dump, jnp.int32))
    dest_flat = dest_t.T.reshape(-1).astype(jnp.int32)
    n_dump = int((~valid_np).sum()) * TOP_K
    n_used = (pad_end[-1] // MOE_TILE).astype(jnp.int32).reshape(1)
    tile_start = jnp.asarray(np.arange(n_tiles, dtype=np.int32) * MOE_TILE)
    tile_e = jnp.minimum(jnp.sum((pad_end[None, :] <= tile_start[:, None]).astype(jnp.int32), axis=1),
                         N_EXPERTS - 1).astype(jnp.int32)

    xs = _dispatch(dest_flat, tile_e, n_used, u_packed, n_slots + n_dump, tm=ROW_ALIGN, n_tiles=n_tiles)
    tile_first = (pad_start // MOE_TILE).astype(jnp.int32)
    tile_count = (padded // MOE_TILE).astype(jnp.int32)
    act = _gate_up(tile_first, tile_count, xs, w_gate_up[layer],
                   b_gate_up[layer].reshape(N_EXPERTS, 1, 2 * f).astype(F32), n_tiles=n_tiles,
                   tf=_pick(GATE_UP_COLS, f))
    down_tn = _pick(DOWN_COLS, d)
    ys = _down(tile_first, tile_count, act, w_down[layer],
               b_down[layer].reshape(N_EXPERTS, 1, d).astype(F32), n_tiles=n_tiles, tn=down_tn)
    return _combine(dest_flat, h2, gate_t.T, final_norm_g, ys, batch=batch, seq=seq, lp=lp,
                    row0=ROW_ALIGN, tm=ROW_ALIGN, pack_w=down_tn)
```

```python
import functools
import math

import numpy as np
import jax
import jax.numpy as jnp
from jax import lax
from jax.experimental import pallas as pl
from jax.experimental.pallas import tpu as pltpu

N_META = 16
HEAD_DIM = 64
N_EXPERTS = 32
TOP_K = 4
CONV_K = 31
EPS = 1e-5
SWIGLU_LIMIT = 7.0
SWIGLU_ALPHA = 1.702
SUBLANES = 8
LANES = 128
ROW_ALIGN = 128
CONV_HALO = 32
MOE_TILE = 256
ATTN_HEADS_PER_STEP = 4
VMEM_LIMIT = 56 * 1024 * 1024

IN_PROJ_ROWS, IN_PROJ_COLS = 1536, 512
MIX_OUT_ROWS, MIX_OUT_COLS = 768, 512
ROW_TILE = 256
ATTN_Q_TILE = 384
GATE_UP_COLS = 512
DOWN_COLS = 2048

F32 = jnp.float32
BF16 = jnp.bfloat16


def _cparams(n_axes):
    return pltpu.CompilerParams(dimension_semantics=("arbitrary",) * n_axes,
                                vmem_limit_bytes=VMEM_LIMIT)


def _embed_norm_kernel(x_ref, meta_ref, g_ref, h_ref, u_ref, *, n_pad):
    i = pl.program_id(1)

    @pl.when(i == 0)
    def _():
        h_ref[0:n_pad, :] = jnp.zeros((n_pad, h_ref.shape[1]), F32)
        h_ref[n_pad:, :] = meta_ref[...]

    @pl.when(i > 0)
    def _():
        h_ref[...] = x_ref[...]

    h = h_ref[...]
    ms = jnp.mean(h * h, axis=-1, keepdims=True)
    u_ref[...] = (h * lax.rsqrt(ms + EPS) * g_ref[...]).astype(u_ref.dtype)


def _embed_norm(x, meta, g, *, n_pad):
    batch, seq, d = x.shape
    tm = ROW_ALIGN
    nb = (n_pad + N_META + seq) // tm
    out_spec = pl.BlockSpec((tm, d), lambda b, i: (b * nb + i, 0))
    return pl.pallas_call(
        functools.partial(_embed_norm_kernel, n_pad=n_pad),
        grid=(batch, nb),
        in_specs=[pl.BlockSpec((None, tm, d), lambda b, i: (b, jnp.maximum(i - 1, 0), 0)),
                  pl.BlockSpec((N_META, d), lambda b, i: (0, 0)),
                  pl.BlockSpec((1, d), lambda b, i: (0, 0))],
        out_specs=[out_spec, out_spec],
        out_shape=[jax.ShapeDtypeStruct((batch * nb * tm, d), F32),
                   jax.ShapeDtypeStruct((batch * nb * tm, d), BF16)],
        compiler_params=_cparams(2),
        name="embed_norm",
    )(x, meta.astype(F32), g.reshape(1, d).astype(F32))


def _cast_weight_once(w_ref, wb_sc):
    @pl.when(pl.program_id(1) == 0)
    def _():
        wb_sc[...] = w_ref[...].astype(BF16)


def _matmul_kernel(a_ref, w_ref, o_ref, wb_sc):
    _cast_weight_once(w_ref, wb_sc)
    o_ref[...] = jnp.dot(a_ref[...], wb_sc[...], preferred_element_type=F32).astype(o_ref.dtype)


def _matmul_res_kernel(a_ref, w_ref, r_ref, o_ref, wb_sc):
    _cast_weight_once(w_ref, wb_sc)
    acc = jnp.dot(a_ref[...], wb_sc[...], preferred_element_type=F32)
    o_ref[...] = (acc + r_ref[...]).astype(o_ref.dtype)


def _matmul(a, w, out_dtype, tm, tn, res=None, name="matmul"):
    m, k = a.shape
    n = w.shape[1]
    in_specs = [pl.BlockSpec((tm, k), lambda j, i: (i, 0)),
                pl.BlockSpec((k, tn), lambda j, i: (0, j))]
    args = [a, w]
    kern = _matmul_kernel
    if res is not None:
        in_specs.append(pl.BlockSpec((tm, tn), lambda j, i: (i, j)))
        args.append(res)
        kern = _matmul_res_kernel
    return pl.pallas_call(
        kern,
        grid=(n // tn, m // tm),
        in_specs=in_specs,
        out_specs=pl.BlockSpec((tm, tn), lambda j, i: (i, j)),
        out_shape=jax.ShapeDtypeStruct((m, n), out_dtype),
        scratch_shapes=[pltpu.VMEM((k, tn), BF16)],
        compiler_params=_cparams(2),
        name=name,
    )(*args)


def _attn_kernel(lq1_ref, lk1_ref, lq2_ref, lk2_ref, hg_ref, bias_ref, q_ref, k_ref, v_ref, o_ref,
                 vt_ref, q12_sc, s_sc, m_sc, l_sc, acc_sc, *, tq, lam_init, n_hd):
    hw = 2 * HEAD_DIM
    nq = q_ref.shape[0] // tq
    lam = (jnp.exp(jnp.sum(lq1_ref[...] * lk1_ref[...], axis=-1, keepdims=True))
           - jnp.exp(jnp.sum(lq2_ref[...] * lk2_ref[...], axis=-1, keepdims=True)) + lam_init)

    for hd in range(n_hd):
        for r0 in range(0, v_ref.shape[0], tq):
            vt_ref[hd, :, r0:r0 + tq] = v_ref[r0:r0 + tq, hd * hw:(hd + 1) * hw].T

    def q_tile(qi, carry):
        q0 = pl.multiple_of(qi * tq, tq)
        for hd in range(n_hd):
            q = (q_ref[pl.ds(q0, tq), hd * hw:(hd + 1) * hw].astype(F32)
                 * (HEAD_DIM ** -0.5 * math.log2(math.e))).astype(BF16)
            lane = lax.broadcasted_iota(jnp.int32, q.shape, 1)
            zero = jnp.zeros_like(q)
            q12_sc[2 * hd] = jnp.where(lane < HEAD_DIM, q, zero)
            q12_sc[2 * hd + 1] = jnp.where(lane >= HEAD_DIM, q, zero)
        m_sc[...] = jnp.full(m_sc.shape, -jnp.inf, F32)
        l_sc[...] = jnp.zeros(l_sc.shape, F32)
        acc_sc[...] = jnp.zeros(acc_sc.shape, F32)

        def scores(kj, c):
            hd = c // 2
            k0 = pl.multiple_of(kj * tq, tq)
            s = lax.dot_general(k_ref[pl.ds(k0, tq), hd * hw:(hd + 1) * hw], q12_sc[c],
                                (((1,), (1,)), ((), ())), preferred_element_type=F32)
            kind = jnp.where(kj == 0, 1, 0) + jnp.where(kj == qi, 2, 0)
            s_sc[c] = s + bias_ref[kind]

        def update(kj, c):
            hd = c // 2
            k0 = pl.multiple_of(kj * tq, tq)
            s = s_sc[c]
            m_prev = m_sc[c]
            m_new = jnp.maximum(m_prev, jnp.max(s, axis=0, keepdims=True))
            alpha = jnp.exp2(m_prev - m_new)
            p = jnp.exp2(s - m_new)
            l_sc[c] = alpha * l_sc[c] + jnp.sum(p.reshape(tq // 8, 8, tq), axis=0)
            acc_sc[c] = alpha * acc_sc[c] + jnp.dot(vt_ref[hd, :, pl.ds(k0, tq)], p.astype(BF16),
                                                    preferred_element_type=F32)
            m_sc[c] = m_new

        for hd in range(n_hd):
            scores(0, 2 * hd)

        def body(kj, c):
            for hd in range(n_hd):
                scores(kj, 2 * hd + 1)
                update(kj, 2 * hd)
                scores(kj + 1, 2 * hd)
            for hd in range(n_hd):
                update(kj, 2 * hd + 1)
            return c

        lax.fori_loop(0, qi, body, 0)
        for hd in range(n_hd):
            scores(qi, 2 * hd + 1)
            update(qi, 2 * hd)
        for hd in range(n_hd):
            update(qi, 2 * hd + 1)

        for hd in range(n_hd):
            o1 = acc_sc[2 * hd] / jnp.sum(l_sc[2 * hd], axis=0, keepdims=True)
            o2 = acc_sc[2 * hd + 1] / jnp.sum(l_sc[2 * hd + 1], axis=0, keepdims=True)
            o = o1 - lam * o2
            ms = jnp.mean(o * o, axis=0, keepdims=True)
            o = o * lax.rsqrt(ms + EPS) * hg_ref[...] * (1.0 - lam_init)
            o_ref[pl.ds(q0, tq), hd * hw:(hd + 1) * hw] = o.T.astype(o_ref.dtype)
        return carry

    lax.fori_loop(0, nq, q_tile, 0)


def _attn_bias(tq, n_pad):
    neg = np.float32(np.finfo(np.float32).min)
    r = np.arange(tq)[:, None]
    c = np.arange(tq)[None, :]
    pad = np.broadcast_to(r < n_pad, (tq, tq))
    future = r > c
    tiles = [np.zeros((tq, tq), bool), pad, future, pad | future]
    return jnp.asarray(np.stack([np.where(t, neg, np.float32(0)) for t in tiles]).astype(np.float32))


def _attention(proj, lq1, lk1, lq2, lk2, head_g, *, batch, lp, n_heads, tq, lam_init, n_pad):
    hw = 2 * HEAD_DIM
    n_hd = ATTN_HEADS_PER_STEP if n_heads % ATTN_HEADS_PER_STEP == 0 else 1
    assert n_pad <= tq
    koff = n_heads // n_hd
    vec = lambda a: a.reshape(1, -1).astype(F32)
    small = lambda n: pl.BlockSpec((1, n), lambda b, h: (0, 0))
    return pl.pallas_call(
        functools.partial(_attn_kernel, tq=tq, lam_init=lam_init, n_hd=n_hd),
        grid=(batch, n_heads // n_hd),
        in_specs=[small(HEAD_DIM), small(HEAD_DIM), small(HEAD_DIM), small(HEAD_DIM),
                  pl.BlockSpec((hw, 1), lambda b, h: (0, 0)),
                  pl.BlockSpec((4, tq, tq), lambda b, h: (0, 0, 0)),
                  pl.BlockSpec((lp, n_hd * hw), lambda b, h: (b, h)),
                  pl.BlockSpec((lp, n_hd * hw), lambda b, h: (b, koff + h)),
                  pl.BlockSpec((lp, n_hd * hw), lambda b, h: (b, 2 * koff + h))],
        out_specs=pl.BlockSpec((lp, n_hd * hw), lambda b, h: (b, h)),
        out_shape=jax.ShapeDtypeStruct((batch * lp, n_heads * hw), BF16),
        scratch_shapes=[pltpu.VMEM((n_hd, hw, lp), BF16),
                        pltpu.VMEM((2 * n_hd, tq, hw), BF16),
                        pltpu.VMEM((2 * n_hd, tq, tq), F32),
                        pltpu.VMEM((2 * n_hd, 1, tq), F32), pltpu.VMEM((2 * n_hd, 8, tq), F32),
                        pltpu.VMEM((2 * n_hd, hw, tq), F32)],
        compiler_params=_cparams(2),
        name="diff_attention",
    )(vec(lq1), vec(lk1), vec(lq2), vec(lk2), head_g.reshape(hw, 1).astype(F32), _attn_bias(tq, n_pad),
      proj, proj, proj)


def _conv_kernel(ca_ref, cg_ref, ca_h_ref, cg_h_ref, w_ref, b_ref, lg_ref, lb_ref, z_ref,
                 ext_sc, sh_sc, y_sc, *, tm):
    i = pl.program_id(0)
    n_ch = w_ref.shape[1]
    n_ext = CONV_HALO + tm
    glu = lambda a, g: a.astype(F32) * jax.nn.sigmoid(g.astype(F32))
    halo = glu(ca_h_ref[...], cg_h_ref[...])
    ext_sc[0:CONV_HALO, :] = jnp.where(i > 0, halo, jnp.zeros_like(halo))
    ext_sc[CONV_HALO:n_ext, :] = glu(ca_ref[...], cg_ref[...])
    ext_sc[n_ext:n_ext + SUBLANES, :] = jnp.zeros((SUBLANES, n_ch), F32)
    base = CONV_HALO - (CONV_K - 1)

    def slab(lc, carry):
        l0 = pl.multiple_of(lc * LANES, LANES)
        for rho in range(SUBLANES):
            sh_sc[rho] = ext_sc[rho:rho + n_ext, pl.ds(l0, LANES)]
        acc = jnp.zeros((tm, LANES), F32) + b_ref[:, pl.ds(l0, LANES)]
        for j in range(CONV_K):
            rho = (base + j) % SUBLANES
            a = base + j - rho
            acc = acc + w_ref[j:j + 1, pl.ds(l0, LANES)] * sh_sc[rho, a:a + tm, :]
        y_sc[:, pl.ds(l0, LANES)] = acc
        return carry

    lax.fori_loop(0, n_ch // LANES, slab, 0)
    acc = y_sc[...]
    mu = jnp.mean(acc, axis=-1, keepdims=True)
    d = acc - mu
    var = jnp.mean(d * d, axis=-1, keepdims=True)
    y = d * lax.rsqrt(var + EPS) * lg_ref[...] + lb_ref[...]
    z_ref[...] = (y * jax.nn.sigmoid(y)).astype(z_ref.dtype)


def _conv_branch(proj, conv_w, conv_b, ln_g, ln_b, *, ca_blk, cg_blk, tm):
    m = proj.shape[0]
    c = conv_w.shape[1]
    hb = tm // CONV_HALO
    row = lambda a: a.reshape(1, c).astype(F32)
    vec = pl.BlockSpec((1, c), lambda i: (0, 0))
    return pl.pallas_call(
        functools.partial(_conv_kernel, tm=tm),
        grid=(m // tm,),
        in_specs=[pl.BlockSpec((tm, c), lambda i: (i, ca_blk)),
                  pl.BlockSpec((tm, c), lambda i: (i, cg_blk)),
                  pl.BlockSpec((CONV_HALO, c), lambda i: (jnp.maximum(i * hb - 1, 0), ca_blk)),
                  pl.BlockSpec((CONV_HALO, c), lambda i: (jnp.maximum(i * hb - 1, 0), cg_blk)),
                  pl.BlockSpec((CONV_K, c), lambda i: (0, 0)),
                  vec, vec, vec],
        out_specs=pl.BlockSpec((tm, c), lambda i: (i, 0)),
        out_shape=jax.ShapeDtypeStruct((m, c), BF16),
        scratch_shapes=[pltpu.VMEM((CONV_HALO + tm + SUBLANES, c), F32),
                        pltpu.VMEM((SUBLANES, CONV_HALO + tm, LANES), F32),
                        pltpu.VMEM((tm, c), F32)],
        compiler_params=_cparams(1),
        name="conformer_conv",
    )(proj, proj, proj, proj, conv_w.astype(F32), row(conv_b), row(ln_g), row(ln_b))


def _merge_kernel(o_ref, z_ref, wa_ref, wc_ref, bc_ref, g1_ref, g2_ref, bg1_ref, bg2_ref, out_ref,
                  wab_sc, wcb_sc):
    _cast_weight_once(wa_ref, wab_sc)
    _cast_weight_once(wc_ref, wcb_sc)
    ya = jnp.dot(o_ref[...], wab_sc[...], preferred_element_type=F32)
    yc = jnp.dot(z_ref[...], wcb_sc[...], preferred_element_type=F32) + bc_ref[...]
    g1 = jax.nn.sigmoid(g1_ref[...].astype(F32) + bg1_ref[...])
    g2 = jax.nn.sigmoid(g2_ref[...].astype(F32) + bg2_ref[...])
    out_ref[...] = (g1 * ya + g2 * yc).astype(out_ref.dtype)


def _merge(o, z, proj, wa, wc, bc, bg, *, gate_col, tm, tn):
    m, ka = o.shape
    kc = z.shape[1]
    d = wa.shape[1]
    g1_blk = gate_col // tn
    g2_blk = (gate_col + d) // tn
    nb = d // tn
    bg2 = bg.reshape(1, 2 * d).astype(F32)
    return pl.pallas_call(
        _merge_kernel,
        grid=(nb, m // tm),
        in_specs=[pl.BlockSpec((tm, ka), lambda j, i: (i, 0)),
                  pl.BlockSpec((tm, kc), lambda j, i: (i, 0)),
                  pl.BlockSpec((ka, tn), lambda j, i: (0, j)),
                  pl.BlockSpec((kc, tn), lambda j, i: (0, j)),
                  pl.BlockSpec((1, tn), lambda j, i: (0, j)),
                  pl.BlockSpec((tm, tn), lambda j, i: (i, g1_blk + j)),
                  pl.BlockSpec((tm, tn), lambda j, i: (i, g2_blk + j)),
                  pl.BlockSpec((1, tn), lambda j, i: (0, j)),
                  pl.BlockSpec((1, tn), lambda j, i: (0, nb + j))],
        out_specs=pl.BlockSpec((tm, tn), lambda j, i: (i, j)),
        out_shape=jax.ShapeDtypeStruct((m, d), BF16),
        scratch_shapes=[pltpu.VMEM((ka, tn), BF16), pltpu.VMEM((kc, tn), BF16)],
        compiler_params=_cparams(2),
        name="mixer_merge",
    )(o, z, wa, wc, bc.reshape(1, d).astype(F32), proj, proj, bg2, bg2)


def _router_kernel(h_ref, g_ref, wr_ref, br_ref, valid_ref,
                   up_ref, e_ref, gate_ref, rank_ref, cnt_ref, carry_sc, *, tm):
    i = pl.program_id(0)

    @pl.when(i == 0)
    def _():
        carry_sc[...] = jnp.zeros(carry_sc.shape, F32)

    h = h_ref[...]
    ms = jnp.mean(h * h, axis=-1, keepdims=True)
    u = h * lax.rsqrt(ms + EPS) * g_ref[...]

    half = u.shape[1] // 2
    u_hi = u.astype(BF16)
    u_hi32 = u_hi.astype(F32)
    bits = lax.bitcast_convert_type(u_hi32, jnp.uint32)
    up_ref[...] = (bits[:, half:] & jnp.uint32(0xFFFF0000)) | (bits[:, :half] >> 16)

    u_lo = (u - u_hi32).astype(BF16)
    w = wr_ref[...]
    w_hi = w.astype(BF16)
    w_lo = (w - w_hi.astype(F32)).astype(BF16)
    nt = (((1,), (1,)), ((), ()))
    logits = (lax.dot_general(w_hi, u_hi, nt, preferred_element_type=F32)
              + lax.dot_general(w_hi, u_lo, nt, preferred_element_type=F32)
              + lax.dot_general(w_lo, u_hi, nt, preferred_element_type=F32)) + br_ref[...]
    n_e = logits.shape[0]
    eiota = lax.broadcasted_iota(jnp.int32, logits.shape, 0).astype(F32)
    work = logits
    sel = jnp.zeros(logits.shape, jnp.bool_)
    top_l, top_e = [], []
    for _ in range(TOP_K):
        mx = jnp.max(work, axis=0, keepdims=True)
        idx = jnp.min(jnp.where(work == mx, eiota, float(n_e)), axis=0, keepdims=True)
        hit = eiota == idx
        top_l.append(mx)
        top_e.append(idx)
        sel = sel | hit
        work = jnp.where(hit, -jnp.inf, work)
    ex = [jnp.exp(t - top_l[0]) for t in top_l]
    den = ex[0] + ex[1] + ex[2] + ex[3]
    gate_ref[...] = jnp.concatenate([e / den for e in ex], axis=0)
    e_ref[...] = jnp.concatenate(top_e, axis=0).astype(jnp.int32)

    selv = jnp.where(sel & (valid_ref[...] > 0.0), 1.0, 0.0)
    before = (lax.broadcasted_iota(jnp.int32, (tm, tm), 0)
              < lax.broadcasted_iota(jnp.int32, (tm, tm), 1)).astype(BF16)
    rank_all = jnp.dot(selv.astype(BF16), before, preferred_element_type=F32) + carry_sc[...]
    ranks = [jnp.sum(jnp.where(eiota == idx, rank_all, 0.0), axis=0, keepdims=True) for idx in top_e]
    rank_ref[...] = jnp.concatenate(ranks, axis=0).astype(jnp.int32)
    carry = carry_sc[...] + jnp.sum(selv, axis=1, keepdims=True)
    carry_sc[...] = carry
    cnt_ref[...] = jnp.broadcast_to(carry, cnt_ref.shape).astype(jnp.int32)


def _router(h2, g, w_router, b_router, valid, *, tm):
    m, d = h2.shape
    n_e = w_router.shape[1]
    tok = lambda dt: jax.ShapeDtypeStruct((TOP_K, m), dt)
    tok_spec = pl.BlockSpec((TOP_K, tm), lambda i: (0, i))
    return pl.pallas_call(
        functools.partial(_router_kernel, tm=tm),
        grid=(m // tm,),
        in_specs=[pl.BlockSpec((tm, d), lambda i: (i, 0)),
                  pl.BlockSpec((1, d), lambda i: (0, 0)),
                  pl.BlockSpec((n_e, d), lambda i: (0, 0)),
                  pl.BlockSpec((n_e, 1), lambda i: (0, 0)),
                  pl.BlockSpec((1, tm), lambda i: (0, i))],
        out_specs=[pl.BlockSpec((tm, d // 2), lambda i: (i, 0)),
                   tok_spec, tok_spec, tok_spec,
                   pl.BlockSpec((n_e, 128), lambda i: (0, 0))],
        out_shape=[jax.ShapeDtypeStruct((m, d // 2), jnp.uint32),
                   tok(jnp.int32), tok(F32), tok(jnp.int32),
                   jax.ShapeDtypeStruct((n_e, 128), jnp.int32)],
        scratch_shapes=[pltpu.VMEM((n_e, 1), F32)],
        compiler_params=_cparams(1),
        name="router",
    )(h2, g.reshape(1, d).astype(F32), w_router.T.astype(F32), b_router.reshape(n_e, 1).astype(F32), valid)


def _dispatch_kernel(dest_ref, te_ref, nu_ref, u_ref, xs_ref, zero_sc, sem, zsem, *, tm, n_tiles):
    i = pl.program_id(0)

    @pl.when(i == 0)
    def _():
        zero_sc[...] = jnp.zeros(zero_sc.shape, zero_sc.dtype)
        nu = nu_ref[0]

        def partly_filled(t):
            nxt = te_ref[jnp.minimum(t + 1, n_tiles - 1)]
            return (t >= nu - 1) | (te_ref[t] != nxt)

        def tile_copy(t):
            return pltpu.make_async_copy(zero_sc, xs_ref.at[pl.ds(t * MOE_TILE, MOE_TILE), :], zsem)

        def start(t, c):
            @pl.when(partly_filled(t))
            def _():
                tile_copy(t).start()
            return c

        def wait(t, c):
            @pl.when(partly_filled(t))
            def _():
                tile_copy(t).wait()
            return c

        lax.fori_loop(0, n_tiles, start, 0)
        lax.fori_loop(0, n_tiles, wait, 0)

    def start_row(r, c):
        for k in range(TOP_K):
            d = dest_ref[(i * tm + r) * TOP_K + k]
            pltpu.make_async_copy(u_ref.at[pl.ds(r, 1), :], xs_ref.at[pl.ds(d, 1), :], sem).start(priority=k % 2)
        return c

    lax.fori_loop(0, tm, start_row, 0, unroll=8)
    for k in range(TOP_K):
        pltpu.make_async_copy(u_ref, xs_ref.at[pl.ds(0, tm), :], sem).wait()


def _dispatch(dest_flat, tile_e, n_used, u_packed, n_rows, *, tm, n_tiles):
    m, w = u_packed.shape
    grid_spec = pltpu.PrefetchScalarGridSpec(
        num_scalar_prefetch=3,
        grid=(m // tm,),
        in_specs=[pl.BlockSpec((tm, w), lambda i, dest, te, nu: (i, 0))],
        out_specs=pl.BlockSpec(memory_space=pl.ANY),
        scratch_shapes=[pltpu.VMEM((MOE_TILE, w), jnp.uint32),
                        pltpu.SemaphoreType.DMA(()), pltpu.SemaphoreType.DMA(())],
    )
    return pl.pallas_call(
        functools.partial(_dispatch_kernel, tm=tm, n_tiles=n_tiles),
        grid_spec=grid_spec,
        out_shape=jax.ShapeDtypeStruct((n_rows, w), jnp.uint32),
        compiler_params=_cparams(1),
        name="moe_dispatch",
    )(dest_flat, tile_e, n_used, u_packed)


def _stream_expert_tiles(ts_ref, tc_ref, g_ref, obuf, in_copy, out_copy, compute, *, nj, n_e, n_tiles):
    j = pl.program_id(0)
    e = pl.program_id(1)
    n_used = ts_ref[n_e - 1] + tc_ref[n_e - 1]
    t0 = ts_ref[e]

    @pl.when((j == 0) & (e == 0))
    def _():
        g_ref[0] = 0
        in_copy(0, 0).start()
        in_copy(1, 1).start()

    def body(i, c):
        g = g_ref[0]
        t = t0 + i
        t2 = t + 2
        wraps = t2 >= n_used
        t2 = jnp.where(wraps, t2 - n_used, t2)

        @pl.when(jnp.logical_not(wraps & (j == nj - 1)))
        def _():
            in_copy(t2, (g + 2) % 3).start()

        in_copy(t, g % 3).wait()

        @pl.when(g >= 2)
        def _():
            out_copy(j, t, g % 2).wait()

        compute(g % 3, g % 2)
        out_copy(j, t, g % 2).start()
        g_ref[0] = g + 1
        return c

    lax.fori_loop(0, tc_ref[e], body, 0)

    @pl.when((j == nj - 1) & (e == n_e - 1))
    def _():
        g = g_ref[0]

        @pl.when(g >= 2)
        def _():
            out_copy(j, 0, g % 2).wait()

        @pl.when(g >= 1)
        def _():
            out_copy(j, 0, (g - 1) % 2).wait()

        obuf[0] = jnp.zeros(obuf.shape[1:], obuf.dtype)

        def zero_tile(t, c):
            for jj in range(nj):
                cp = out_copy(jj, t, 0)
                cp.start()
                cp.wait()
            return c

        lax.fori_loop(n_used, n_tiles, zero_tile, 0)


def _gate_up_kernel(ts_ref, tc_ref, xs_ref, wg_ref, wu_ref, bg_ref, bu_ref, act_ref,
                    wgb_sc, wub_sc, xbuf, obuf, g_ref, xsem, osem, *, tf, nj, n_e, n_tiles):
    def in_copy(t, slot):
        return pltpu.make_async_copy(xs_ref.at[pl.ds(t * MOE_TILE, MOE_TILE), :], xbuf.at[slot], xsem.at[slot])

    def out_copy(jj, t, slot):
        return pltpu.make_async_copy(obuf.at[slot],
                                     act_ref.at[pl.ds(t * MOE_TILE, MOE_TILE), pl.ds(jj * tf, tf)], osem.at[slot])

    @pl.when(tc_ref[pl.program_id(1)] > 0)
    def _():
        wgb_sc[...] = wg_ref[...].astype(BF16)
        wub_sc[...] = wu_ref[...].astype(BF16)

    def compute(in_slot, out_slot):
        w = xbuf[in_slot]
        half = w.shape[1]
        lo = lax.bitcast_convert_type(w << 16, F32).astype(BF16)
        hi = lax.bitcast_convert_type(w & jnp.uint32(0xFFFF0000), F32).astype(BF16)
        g = (jnp.dot(lo, wgb_sc[:half, :], preferred_element_type=F32)
             + jnp.dot(hi, wgb_sc[half:, :], preferred_element_type=F32) + bg_ref[...])
        u = (jnp.dot(lo, wub_sc[:half, :], preferred_element_type=F32)
             + jnp.dot(hi, wub_sc[half:, :], preferred_element_type=F32) + bu_ref[...])
        g = jnp.minimum(g, SWIGLU_LIMIT)
        u = jnp.clip(u, -SWIGLU_LIMIT, SWIGLU_LIMIT)
        obuf[out_slot] = ((u + 1.0) * (g * jax.nn.sigmoid(SWIGLU_ALPHA * g))).astype(obuf.dtype)

    _stream_expert_tiles(ts_ref, tc_ref, g_ref, obuf, in_copy, out_copy, compute, nj=nj, n_e=n_e, n_tiles=n_tiles)


def _gate_up(tile_start, tile_count, xs, w_gu, b_gu, *, n_tiles, tf):
    n_e, d, f2 = w_gu.shape
    f = f2 // 2
    nj = f // tf
    grid_spec = pltpu.PrefetchScalarGridSpec(
        num_scalar_prefetch=2,
        grid=(nj, n_e),
        in_specs=[pl.BlockSpec(memory_space=pl.ANY),
                  pl.BlockSpec((None, d, tf), lambda j, e, ts, tc: (e, 0, j)),
                  pl.BlockSpec((None, d, tf), lambda j, e, ts, tc: (e, 0, nj + j)),
                  pl.BlockSpec((None, 1, tf), lambda j, e, ts, tc: (e, 0, j)),
                  pl.BlockSpec((None, 1, tf), lambda j, e, ts, tc: (e, 0, nj + j))],
        out_specs=pl.BlockSpec(memory_space=pl.ANY),
        scratch_shapes=[pltpu.VMEM((d, tf), BF16), pltpu.VMEM((d, tf), BF16),
                        pltpu.VMEM((3, MOE_TILE, d // 2), jnp.uint32), pltpu.VMEM((2, MOE_TILE, tf), BF16),
                        pltpu.SMEM((1,), jnp.int32),
                        pltpu.SemaphoreType.DMA((3,)), pltpu.SemaphoreType.DMA((2,))],
    )
    return pl.pallas_call(
        functools.partial(_gate_up_kernel, tf=tf, nj=nj, n_e=n_e, n_tiles=n_tiles),
        grid_spec=grid_spec,
        out_shape=jax.ShapeDtypeStruct((n_tiles * MOE_TILE, f), BF16),
        compiler_params=_cparams(2),
        name="moe_gate_up",
    )(tile_start, tile_count, xs, w_gu, w_gu, b_gu, b_gu)


def _down_kernel(ts_ref, tc_ref, act_ref, wd_ref, bd_ref, y_ref, wdb_sc, abuf, obuf, g_ref, asem, osem,
                 *, tn, nj, n_e, n_tiles):
    def in_copy(t, slot):
        return pltpu.make_async_copy(act_ref.at[pl.ds(t * MOE_TILE, MOE_TILE), :], abuf.at[slot], asem.at[slot])

    half = tn // 2

    def out_copy(jj, t, slot):
        return pltpu.make_async_copy(obuf.at[slot],
                                     y_ref.at[pl.ds(t * MOE_TILE, MOE_TILE), pl.ds(jj * half, half)], osem.at[slot])

    @pl.when(tc_ref[pl.program_id(1)] > 0)
    def _():
        wdb_sc[...] = wd_ref[...].astype(BF16)

    def compute(in_slot, out_slot):
        y = jnp.dot(abuf[in_slot], wdb_sc[...], preferred_element_type=F32) + bd_ref[...]
        bits = lax.bitcast_convert_type(y.astype(BF16).astype(F32), jnp.uint32)
        obuf[out_slot] = (bits[:, half:] & jnp.uint32(0xFFFF0000)) | (bits[:, :half] >> 16)

    _stream_expert_tiles(ts_ref, tc_ref, g_ref, obuf, in_copy, out_copy, compute, nj=nj, n_e=n_e, n_tiles=n_tiles)


def _down(tile_start, tile_count, act, w_d, b_d, *, n_tiles, tn):
    n_e, f, d = w_d.shape
    grid_spec = pltpu.PrefetchScalarGridSpec(
        num_scalar_prefetch=2,
        grid=(d // tn, n_e),
        in_specs=[pl.BlockSpec(memory_space=pl.ANY),
                  pl.BlockSpec((None, f, tn), lambda j, e, ts, tc: (e, 0, j)),
                  pl.BlockSpec((None, 1, tn), lambda j, e, ts, tc: (e, 0, j))],
        out_specs=pl.BlockSpec(memory_space=pl.ANY),
        scratch_shapes=[pltpu.VMEM((f, tn), BF16),
                        pltpu.VMEM((3, MOE_TILE, f), BF16), pltpu.VMEM((2, MOE_TILE, tn // 2), jnp.uint32),
                        pltpu.SMEM((1,), jnp.int32),
                        pltpu.SemaphoreType.DMA((3,)), pltpu.SemaphoreType.DMA((2,))],
    )
    return pl.pallas_call(
        functools.partial(_down_kernel, tn=tn, nj=d // tn, n_e=n_e, n_tiles=n_tiles),
        grid_spec=grid_spec,
        out_shape=jax.ShapeDtypeStruct((n_tiles * MOE_TILE, d // 2), jnp.uint32),
        compiler_params=_cparams(2),
        name="moe_down",
    )(tile_start, tile_count, act, w_d, b_d)


def _combine_kernel(dest_ref, h_ref, gate_ref, g_ref, ys_ref, out_ref, buf, sem, *, tm, lp, row0, nst, pack_w):
    s = pl.program_id(0)
    n_steps = pl.num_programs(0)

    def start_gather(step, slot):
        t0 = (step // nst) * lp + row0 + (step % nst) * tm

        def start_row(r, c):
            for k in range(TOP_K):
                d = dest_ref[(t0 + r) * TOP_K + k]
                pltpu.make_async_copy(ys_ref.at[pl.ds(d, 1), :], buf.at[slot, k, pl.ds(r, 1), :],
                                      sem.at[slot]).start(priority=k % 2)
            return c

        lax.fori_loop(0, tm, start_row, 0, unroll=8)

    @pl.when(s == 0)
    def _():
        start_gather(0, 0)

    @pl.when(s + 1 < n_steps)
    def _():
        start_gather(s + 1, (s + 1) % 2)

    slot = s % 2
    for k in range(TOP_K):
        pltpu.make_async_copy(ys_ref.at[pl.ds(0, tm), :], buf.at[slot, k], sem.at[slot]).wait()

    gate = gate_ref[...]
    half = pack_w // 2
    acc = jnp.zeros(h_ref.shape, F32)
    for k in range(TOP_K):
        w = buf[slot, k]
        lo = lax.bitcast_convert_type(w << 16, F32)
        hi = lax.bitcast_convert_type(w & jnp.uint32(0xFFFF0000), F32)
        cols = []
        for c in range(w.shape[1] // half):
            cols += [lo[:, c * half:(c + 1) * half], hi[:, c * half:(c + 1) * half]]
        acc = acc + gate[:, k:k + 1] * jnp.concatenate(cols, axis=1)
    h = h_ref[...] + acc
    ms = jnp.mean(h * h, axis=-1, keepdims=True)
    out_ref[...] = (h * lax.rsqrt(ms + EPS) * g_ref[...]).astype(out_ref.dtype)


def _combine(dest_flat, h2, gate, g, ys, *, batch, seq, lp, row0, tm, pack_w):
    d = h2.shape[1]
    nb_b = lp // tm
    nb0 = row0 // tm
    nst = seq // tm
    blk = lambda s: (s // nst) * nb_b + nb0 + s % nst
    grid_spec = pltpu.PrefetchScalarGridSpec(
        num_scalar_prefetch=1,
        grid=(batch * nst,),
        in_specs=[pl.BlockSpec((tm, d), lambda s, dest: (blk(s), 0)),
                  pl.BlockSpec((tm, TOP_K), lambda s, dest: (blk(s), 0)),
                  pl.BlockSpec((1, d), lambda s, dest: (0, 0)),
                  pl.BlockSpec(memory_space=pl.ANY)],
        out_specs=pl.BlockSpec((None, tm, d), lambda s, dest: (s // nst, s % nst, 0)),
        scratch_shapes=[pltpu.VMEM((2, TOP_K, tm, d // 2), jnp.uint32), pltpu.SemaphoreType.DMA((2,))],
    )
    return pl.pallas_call(
        functools.partial(_combine_kernel, tm=tm, lp=lp, row0=row0, nst=nst, pack_w=pack_w),
        grid_spec=grid_spec,
        out_shape=jax.ShapeDtypeStruct((batch, seq, d), F32),
        compiler_params=_cparams(1),
        name="moe_combine",
    )(dest_flat, h2, gate, g.reshape(1, d).astype(F32), ys)


def _pick(pref, n):
    t = pref
    while n % t:
        t //= 2
    return t


def _row_tile(n, pref):
    t = pref // ROW_ALIGN * ROW_ALIGN
    while n % t:
        t -= ROW_ALIGN
    return t


def kernel(x, meta_tokens, norm_mix_g, w_in, b_gate, lambda_q1, lambda_k1, lambda_q2, lambda_k2, head_norm_g, w_attn_out, conv_w, conv_b, conv_ln_g, conv_ln_b, w_conv_out, b_conv_out, w_out, norm_ffn_g, w_router, b_router, w_gate_up, b_gate_up, w_down, b_down, final_norm_g):
    batch, seq, d = x.shape
    depth = w_in.shape[0]
    assert depth == 1 and seq % ROW_ALIGN == 0 and N_META <= ROW_ALIGN
    n_heads = d // 256
    hw = 2 * HEAD_DIM
    qk_w = n_heads * hw
    conv_ch = conv_w.shape[2]
    n_pad = ROW_ALIGN - N_META
    lp = n_pad + N_META + seq
    tp = batch * lp
    f = w_down.shape[2]
    layer = 0
    lam_init = 0.8 - 0.6 * math.exp(-0.3 * layer)

    h0, u = _embed_norm(x, meta_tokens, norm_mix_g[layer], n_pad=n_pad)

    proj = _matmul(u, w_in[layer], BF16, _row_tile(tp, IN_PROJ_ROWS), _pick(IN_PROJ_COLS, w_in.shape[2]),
                   name="in_proj")

    tq = ATTN_Q_TILE if lp % ATTN_Q_TILE == 0 else ROW_ALIGN
    o = _attention(proj, lambda_q1[layer], lambda_k1[layer], lambda_q2[layer], lambda_k2[layer],
                   head_norm_g[layer], batch=batch, lp=lp, n_heads=n_heads, tq=tq, lam_init=lam_init,
                   n_pad=n_pad)
    ca_col = 3 * qk_w
    z = _conv_branch(proj, conv_w[layer], conv_b[layer], conv_ln_g[layer], conv_ln_b[layer],
                     ca_blk=ca_col // conv_ch, cg_blk=ca_col // conv_ch + 1, tm=_pick(ROW_TILE, tp))
    merged = _merge(o, z, proj, w_attn_out[layer], w_conv_out[layer],
                    b_conv_out[layer], b_gate[layer], gate_col=ca_col + 2 * conv_ch,
                    tm=_row_tile(tp, MIX_OUT_ROWS), tn=_pick(MIX_OUT_COLS, d))
    h2 = _matmul(merged, w_out[layer], F32, _row_tile(tp, MIX_OUT_ROWS), _pick(MIX_OUT_COLS, d), res=h0,
                 name="out_proj")

    pos = np.arange(tp) % lp
    valid_np = pos >= n_pad
    valid = jnp.asarray(valid_np.astype(np.float32).reshape(1, tp))
    u_packed, top_e, gate_t, rank_t, cnt = _router(h2, norm_ffn_g[layer], w_router[layer], b_router[layer],
                                                   valid, tm=_pick(ROW_TILE, tp))
    counts = cnt[:, 0]
    padded = (counts + MOE_TILE - 1) // MOE_TILE * MOE_TILE
    e_ids = np.arange(N_EXPERTS)
    pad_end = jnp.sum(jnp.where(jnp.asarray(e_ids[None, :] <= e_ids[:, None]), padded[None, :], 0), axis=1)
    pad_start = pad_end - padded
    start_tok = jnp.sum(jnp.where(top_e[:, :, None] == jnp.asarray(e_ids, jnp.int32), pad_start, 0), axis=-1)
    n_real = int(valid_np.sum()) * TOP_K
    assert n_real >= 2 * MOE_TILE
    n_tiles = -(-(n_real + N_EXPERTS * (MOE_TILE - 1)) // MOE_TILE)
    n_slots = n_tiles * MOE_TILE
    dump = n_slots + (np.cumsum(~valid_np) - 1)[None, :] * TOP_K + np.arange(TOP_K)[:, None]
    dest_t = jnp.where(jnp.asarray(valid_np)[None, :], start_tok + rank_t, jnp.asarray(dump, jnp.int32))
    dest_flat = dest_t.T.reshape(-1).astype(jnp.int32)
    n_dump = int((~valid_np).sum()) * TOP_K
    n_used = (pad_end[-1] // MOE_TILE).astype(jnp.int32).reshape(1)
    tile_start = jnp.asarray(np.arange(n_tiles, dtype=np.int32) * MOE_TILE)
    tile_e = jnp.minimum(jnp.sum((pad_end[None, :] <= tile_start[:, None]).astype(jnp.int32), axis=1),
                         N_EXPERTS - 1).astype(jnp.int32)

    xs = _dispatch(dest_flat, tile_e, n_used, u_packed, n_slots + n_dump, tm=ROW_ALIGN, n_tiles=n_tiles)
    tile_first = (pad_start // MOE_TILE).astype(jnp.int32)
    tile_count = (padded // MOE_TILE).astype(jnp.int32)
    act = _gate_up(tile_first, tile_count, xs, w_gate_up[layer],
                   b_gate_up[layer].reshape(N_EXPERTS, 1, 2 * f).astype(F32), n_tiles=n_tiles,
                   tf=_pick(GATE_UP_COLS, f))
    down_tn = _pick(DOWN_COLS, d)
    ys = _down(tile_first, tile_count, act, w_down[layer],
               b_down[layer].reshape(N_EXPERTS, 1, d).astype(F32), n_tiles=n_tiles, tn=down_tn)
    return _combine(dest_flat, h2, gate_t.T, final_norm_g, ys, batch=batch, seq=seq, lp=lp,
                    row0=ROW_ALIGN, tm=ROW_ALIGN, pack_w=down_tn)
```

```python
import functools
import math

import numpy as np
import jax
import jax.numpy as jnp
from jax import lax
from jax.experimental import pallas as pl
from jax.experimental.pallas import tpu as pltpu

N_META = 16
HEAD_DIM = 64
N_EXPERTS = 32
TOP_K = 4
CONV_K = 31
EPS = 1e-5
SWIGLU_LIMIT = 7.0
SWIGLU_ALPHA = 1.702
SUBLANES = 8
LANES = 128
ROW_ALIGN = 128
CONV_HALO = 32
MOE_TILE = 256
ATTN_HEADS_PER_STEP = 4
VMEM_LIMIT = 56 * 1024 * 1024

IN_PROJ_ROWS, IN_PROJ_COLS = 1536, 512
MIX_OUT_ROWS, MIX_OUT_COLS = 768, 512
ROW_TILE = 256
ATTN_Q_TILE = 384
GATE_UP_COLS = 512
DOWN_COLS = 2048

F32 = jnp.float32
BF16 = jnp.bfloat16


def _cparams(n_axes):
    return pltpu.CompilerParams(dimension_semantics=("arbitrary",) * n_axes,
                                vmem_limit_bytes=VMEM_LIMIT)


def _embed_norm_kernel(x_ref, meta_ref, g_ref, h_ref, u_ref, *, n_pad):
    i = pl.program_id(1)

    @pl.when(i == 0)
    def _():
        h_ref[0:n_pad, :] = jnp.zeros((n_pad, h_ref.shape[1]), F32)
        h_ref[n_pad:, :] = meta_ref[...]

    @pl.when(i > 0)
    def _():
        h_ref[...] = x_ref[...]

    h = h_ref[...]
    ms = jnp.mean(h * h, axis=-1, keepdims=True)
    u_ref[...] = (h * lax.rsqrt(ms + EPS) * g_ref[...]).astype(u_ref.dtype)


def _embed_norm(x, meta, g, *, n_pad):
    batch, seq, d = x.shape
    tm = ROW_ALIGN
    nb = (n_pad + N_META + seq) // tm
    out_spec = pl.BlockSpec((tm, d), lambda b, i: (b * nb + i, 0))
    return pl.pallas_call(
        functools.partial(_embed_norm_kernel, n_pad=n_pad),
        grid=(batch, nb),
        in_specs=[pl.BlockSpec((None, tm, d), lambda b, i: (b, jnp.maximum(i - 1, 0), 0)),
                  pl.BlockSpec((N_META, d), lambda b, i: (0, 0)),
                  pl.BlockSpec((1, d), lambda b, i: (0, 0))],
        out_specs=[out_spec, out_spec],
        out_shape=[jax.ShapeDtypeStruct((batch * nb * tm, d), F32),
                   jax.ShapeDtypeStruct((batch * nb * tm, d), BF16)],
        compiler_params=_cparams(2),
        name="embed_norm",
    )(x, meta.astype(F32), g.reshape(1, d).astype(F32))


def _cast_weight_once(w_ref, wb_sc):
    @pl.when(pl.program_id(1) == 0)
    def _():
        wb_sc[...] = w_ref[...].astype(BF16)


def _matmul_kernel(a_ref, w_ref, o_ref, wb_sc):
    _cast_weight_once(w_ref, wb_sc)
    o_ref[...] = jnp.dot(a_ref[...], wb_sc[...], preferred_element_type=F32).astype(o_ref.dtype)


def _ring_rows(srcs, bufs, sem, *, tm, n_row_tiles, n_steps):
    g = pl.program_id(0) * n_row_tiles + pl.program_id(1)

    def copies(step, slot):
        r0 = pl.multiple_of((step % n_row_tiles) * tm, tm)
        return [pltpu.make_async_copy(src.at[pl.ds(r0, tm), :], buf.at[slot], sem.at[slot, n])
                for n, (src, buf) in enumerate(zip(srcs, bufs))]

    @pl.when(g == 0)
    def _():
        for step in range(2):
            for cp in copies(step, step):
                cp.start()

    @pl.when(g + 2 < n_steps)
    def _():
        for cp in copies(g + 2, (g + 2) % 3):
            cp.start()

    slot = g % 3
    for cp in copies(g, slot):
        cp.wait()
    return slot


def _matmul_res_kernel(a_ref, w_ref, r_ref, o_ref, wb_sc, abuf, sem, *, tm, n_row_tiles, n_steps):
    _cast_weight_once(w_ref, wb_sc)
    slot = _ring_rows([a_ref], [abuf], sem, tm=tm, n_row_tiles=n_row_tiles, n_steps=n_steps)
    acc = jnp.dot(abuf[slot], wb_sc[...], preferred_element_type=F32)
    o_ref[...] = (acc + r_ref[...]).astype(o_ref.dtype)


def _matmul(a, w, out_dtype, tm, tn, res=None, name="matmul"):
    m, k = a.shape
    n = w.shape[1]
    grid = (n // tn, m // tm)
    w_spec = pl.BlockSpec((k, tn), lambda j, i: (0, j))
    scratch = [pltpu.VMEM((k, tn), BF16)]
    if res is None:
        kern = _matmul_kernel
        in_specs = [pl.BlockSpec((tm, k), lambda j, i: (i, 0)), w_spec]
        args = [a, w]
    else:
        assert grid[0] * grid[1] >= 2
        kern = functools.partial(_matmul_res_kernel, tm=tm, n_row_tiles=grid[1], n_steps=grid[0] * grid[1])
        in_specs = [pl.BlockSpec(memory_space=pl.ANY), w_spec, pl.BlockSpec((tm, tn), lambda j, i: (i, j))]
        args = [a, w, res]
        scratch += [pltpu.VMEM((3, tm, k), a.dtype), pltpu.SemaphoreType.DMA((3, 1))]
    return pl.pallas_call(
        kern,
        grid=grid,
        in_specs=in_specs,
        out_specs=pl.BlockSpec((tm, tn), lambda j, i: (i, j)),
        out_shape=jax.ShapeDtypeStruct((m, n), out_dtype),
        scratch_shapes=scratch,
        compiler_params=_cparams(2),
        name=name,
    )(*args)


def _attn_kernel(lq1_ref, lk1_ref, lq2_ref, lk2_ref, hg_ref, bias_ref, q_ref, k_ref, v_ref, o_ref,
                 vt_ref, q12_sc, s_sc, m_sc, l_sc, acc_sc, *, tq, lam_init, n_hd):
    hw = 2 * HEAD_DIM
    nq = q_ref.shape[0] // tq
    lam = (jnp.exp(jnp.sum(lq1_ref[...] * lk1_ref[...], axis=-1, keepdims=True))
           - jnp.exp(jnp.sum(lq2_ref[...] * lk2_ref[...], axis=-1, keepdims=True)) + lam_init)

    for hd in range(n_hd):
        for r0 in range(0, v_ref.shape[0], tq):
            vt_ref[hd, :, r0:r0 + tq] = v_ref[r0:r0 + tq, hd * hw:(hd + 1) * hw].T

    def q_tile(qi, carry):
        q0 = pl.multiple_of(qi * tq, tq)
        for hd in range(n_hd):
            q = (q_ref[pl.ds(q0, tq), hd * hw:(hd + 1) * hw].astype(F32)
                 * (HEAD_DIM ** -0.5 * math.log2(math.e))).astype(BF16)
            lane = lax.broadcasted_iota(jnp.int32, q.shape, 1)
            zero = jnp.zeros_like(q)
            q12_sc[2 * hd] = jnp.where(lane < HEAD_DIM, q, zero)
            q12_sc[2 * hd + 1] = jnp.where(lane >= HEAD_DIM, q, zero)
        m_sc[...] = jnp.full(m_sc.shape, -jnp.inf, F32)
        l_sc[...] = jnp.zeros(l_sc.shape, F32)
        acc_sc[...] = jnp.zeros(acc_sc.shape, F32)

        def scores(kj, c):
            hd = c // 2
            k0 = pl.multiple_of(kj * tq, tq)
            s = lax.dot_general(k_ref[pl.ds(k0, tq), hd * hw:(hd + 1) * hw], q12_sc[c],
                                (((1,), (1,)), ((), ())), preferred_element_type=F32)
            kind = jnp.where(kj == 0, 1, 0) + jnp.where(kj == qi, 2, 0)
            s_sc[c] = s + bias_ref[kind]

        def update(kj, c):
            hd = c // 2
            k0 = pl.multiple_of(kj * tq, tq)
            s = s_sc[c]
            m_prev = m_sc[c]
            m_new = jnp.maximum(m_prev, jnp.max(s, axis=0, keepdims=True))
            alpha = jnp.exp2(m_prev - m_new)
            p = jnp.exp2(s - m_new)
            l_sc[c] = alpha * l_sc[c] + jnp.sum(p.reshape(tq // 8, 8, tq), axis=0)
            acc_sc[c] = alpha * acc_sc[c] + jnp.dot(vt_ref[hd, :, pl.ds(k0, tq)], p.astype(BF16),
                                                    preferred_element_type=F32)
            m_sc[c] = m_new

        for hd in range(n_hd):
            scores(0, 2 * hd)

        def body(kj, c):
            for hd in range(n_hd):
                scores(kj, 2 * hd + 1)
                update(kj, 2 * hd)
                scores(kj + 1, 2 * hd)
            for hd in range(n_hd):
                update(kj, 2 * hd + 1)
            return c

        lax.fori_loop(0, qi, body, 0)
        for hd in range(n_hd):
            scores(qi, 2 * hd + 1)
            update(qi, 2 * hd)
        for hd in range(n_hd):
            update(qi, 2 * hd + 1)

        for hd in range(n_hd):
            o1 = acc_sc[2 * hd] / jnp.sum(l_sc[2 * hd], axis=0, keepdims=True)
            o2 = acc_sc[2 * hd + 1] / jnp.sum(l_sc[2 * hd + 1], axis=0, keepdims=True)
            o = o1 - lam * o2
            ms = jnp.mean(o * o, axis=0, keepdims=True)
            o = o * lax.rsqrt(ms + EPS) * hg_ref[...] * (1.0 - lam_init)
            o_ref[pl.ds(q0, tq), hd * hw:(hd + 1) * hw] = o.T.astype(o_ref.dtype)
        return carry

    lax.fori_loop(0, nq, q_tile, 0)


def _attn_bias(tq, n_pad):
    neg = np.float32(np.finfo(np.float32).min)
    r = np.arange(tq)[:, None]
    c = np.arange(tq)[None, :]
    pad = np.broadcast_to(r < n_pad, (tq, tq))
    future = r > c
    tiles = [np.zeros((tq, tq), bool), pad, future, pad | future]
    return jnp.asarray(np.stack([np.where(t, neg, np.float32(0)) for t in tiles]).astype(np.float32))


def _attention(proj, lq1, lk1, lq2, lk2, head_g, *, batch, lp, n_heads, tq, lam_init, n_pad):
    hw = 2 * HEAD_DIM
    n_hd = ATTN_HEADS_PER_STEP if n_heads % ATTN_HEADS_PER_STEP == 0 else 1
    assert n_pad <= tq
    koff = n_heads // n_hd
    vec = lambda a: a.reshape(1, -1).astype(F32)
    small = lambda n: pl.BlockSpec((1, n), lambda b, h: (0, 0))
    return pl.pallas_call(
        functools.partial(_attn_kernel, tq=tq, lam_init=lam_init, n_hd=n_hd),
        grid=(batch, n_heads // n_hd),
        in_specs=[small(HEAD_DIM), small(HEAD_DIM), small(HEAD_DIM), small(HEAD_DIM),
                  pl.BlockSpec((hw, 1), lambda b, h: (0, 0)),
                  pl.BlockSpec((4, tq, tq), lambda b, h: (0, 0, 0)),
                  pl.BlockSpec((lp, n_hd * hw), lambda b, h: (b, h)),
                  pl.BlockSpec((lp, n_hd * hw), lambda b, h: (b, koff + h)),
                  pl.BlockSpec((lp, n_hd * hw), lambda b, h: (b, 2 * koff + h))],
        out_specs=pl.BlockSpec((lp, n_hd * hw), lambda b, h: (b, h)),
        out_shape=jax.ShapeDtypeStruct((batch * lp, n_heads * hw), BF16),
        scratch_shapes=[pltpu.VMEM((n_hd, hw, lp), BF16),
                        pltpu.VMEM((2 * n_hd, tq, hw), BF16),
                        pltpu.VMEM((2 * n_hd, tq, tq), F32),
                        pltpu.VMEM((2 * n_hd, 1, tq), F32), pltpu.VMEM((2 * n_hd, 8, tq), F32),
                        pltpu.VMEM((2 * n_hd, hw, tq), F32)],
        compiler_params=_cparams(2),
        name="diff_attention",
    )(vec(lq1), vec(lk1), vec(lq2), vec(lk2), head_g.reshape(hw, 1).astype(F32), _attn_bias(tq, n_pad),
      proj, proj, proj)


def _conv_kernel(ca_ref, cg_ref, ca_h_ref, cg_h_ref, w_ref, b_ref, lg_ref, lb_ref, z_ref,
                 ext_sc, sh_sc, y_sc, *, tm):
    i = pl.program_id(0)
    n_ch = w_ref.shape[1]
    n_ext = CONV_HALO + tm
    glu = lambda a, g: a.astype(F32) * jax.nn.sigmoid(g.astype(F32))
    halo = glu(ca_h_ref[...], cg_h_ref[...])
    ext_sc[0:CONV_HALO, :] = jnp.where(i > 0, halo, jnp.zeros_like(halo))
    ext_sc[CONV_HALO:n_ext, :] = glu(ca_ref[...], cg_ref[...])
    ext_sc[n_ext:n_ext + SUBLANES, :] = jnp.zeros((SUBLANES, n_ch), F32)
    base = CONV_HALO - (CONV_K - 1)

    def slab(lc, carry):
        l0 = pl.multiple_of(lc * LANES, LANES)
        for rho in range(SUBLANES):
            sh_sc[rho] = ext_sc[rho:rho + n_ext, pl.ds(l0, LANES)]
        acc = jnp.zeros((tm, LANES), F32) + b_ref[:, pl.ds(l0, LANES)]
        for j in range(CONV_K):
            rho = (base + j) % SUBLANES
            a = base + j - rho
            acc = acc + w_ref[j:j + 1, pl.ds(l0, LANES)] * sh_sc[rho, a:a + tm, :]
        y_sc[:, pl.ds(l0, LANES)] = acc
        return carry

    lax.fori_loop(0, n_ch // LANES, slab, 0)
    acc = y_sc[...]
    mu = jnp.mean(acc, axis=-1, keepdims=True)
    d = acc - mu
    var = jnp.mean(d * d, axis=-1, keepdims=True)
    y = d * lax.rsqrt(var + EPS) * lg_ref[...] + lb_ref[...]
    z_ref[...] = (y * jax.nn.sigmoid(y)).astype(z_ref.dtype)


def _conv_branch(proj, conv_w, conv_b, ln_g, ln_b, *, ca_blk, cg_blk, tm):
    m = proj.shape[0]
    c = conv_w.shape[1]
    hb = tm // CONV_HALO
    row = lambda a: a.reshape(1, c).astype(F32)
    vec = pl.BlockSpec((1, c), lambda i: (0, 0))
    return pl.pallas_call(
        functools.partial(_conv_kernel, tm=tm),
        grid=(m // tm,),
        in_specs=[pl.BlockSpec((tm, c), lambda i: (i, ca_blk)),
                  pl.BlockSpec((tm, c), lambda i: (i, cg_blk)),
                  pl.BlockSpec((CONV_HALO, c), lambda i: (jnp.maximum(i * hb - 1, 0), ca_blk)),
                  pl.BlockSpec((CONV_HALO, c), lambda i: (jnp.maximum(i * hb - 1, 0), cg_blk)),
                  pl.BlockSpec((CONV_K, c), lambda i: (0, 0)),
                  vec, vec, vec],
        out_specs=pl.BlockSpec((tm, c), lambda i: (i, 0)),
        out_shape=jax.ShapeDtypeStruct((m, c), BF16),
        scratch_shapes=[pltpu.VMEM((CONV_HALO + tm + SUBLANES, c), F32),
                        pltpu.VMEM((SUBLANES, CONV_HALO + tm, LANES), F32),
                        pltpu.VMEM((tm, c), F32)],
        compiler_params=_cparams(1),
        name="conformer_conv",
    )(proj, proj, proj, proj, conv_w.astype(F32), row(conv_b), row(ln_g), row(ln_b))


def _merge_kernel(o_ref, z_ref, wa_ref, wc_ref, bc_ref, g1_ref, g2_ref, bg1_ref, bg2_ref, out_ref,
                  wab_sc, wcb_sc, obuf, zbuf, sem, *, tm, n_row_tiles, n_steps):
    _cast_weight_once(wa_ref, wab_sc)
    _cast_weight_once(wc_ref, wcb_sc)
    slot = _ring_rows([o_ref, z_ref], [obuf, zbuf], sem, tm=tm, n_row_tiles=n_row_tiles, n_steps=n_steps)
    ya = jnp.dot(obuf[slot], wab_sc[...], preferred_element_type=F32)
    yc = jnp.dot(zbuf[slot], wcb_sc[...], preferred_element_type=F32) + bc_ref[...]
    g1 = jax.nn.sigmoid(g1_ref[...].astype(F32) + bg1_ref[...])
    g2 = jax.nn.sigmoid(g2_ref[...].astype(F32) + bg2_ref[...])
    out_ref[...] = (g1 * ya + g2 * yc).astype(out_ref.dtype)


def _merge(o, z, proj, wa, wc, bc, bg, *, gate_col, tm, tn):
    m, ka = o.shape
    kc = z.shape[1]
    d = wa.shape[1]
    g1_blk = gate_col // tn
    g2_blk = (gate_col + d) // tn
    nb = d // tn
    n_row_tiles = m // tm
    assert nb * n_row_tiles >= 2
    bg2 = bg.reshape(1, 2 * d).astype(F32)
    return pl.pallas_call(
        functools.partial(_merge_kernel, tm=tm, n_row_tiles=n_row_tiles, n_steps=nb * n_row_tiles),
        grid=(nb, n_row_tiles),
        in_specs=[pl.BlockSpec(memory_space=pl.ANY),
                  pl.BlockSpec(memory_space=pl.ANY),
                  pl.BlockSpec((ka, tn), lambda j, i: (0, j)),
                  pl.BlockSpec((kc, tn), lambda j, i: (0, j)),
                  pl.BlockSpec((1, tn), lambda j, i: (0, j)),
                  pl.BlockSpec((tm, tn), lambda j, i: (i, g1_blk + j)),
                  pl.BlockSpec((tm, tn), lambda j, i: (i, g2_blk + j)),
                  pl.BlockSpec((1, tn), lambda j, i: (0, j)),
                  pl.BlockSpec((1, tn), lambda j, i: (0, nb + j))],
        out_specs=pl.BlockSpec((tm, tn), lambda j, i: (i, j)),
        out_shape=jax.ShapeDtypeStruct((m, d), BF16),
        scratch_shapes=[pltpu.VMEM((ka, tn), BF16), pltpu.VMEM((kc, tn), BF16),
                        pltpu.VMEM((3, tm, ka), o.dtype), pltpu.VMEM((3, tm, kc), z.dtype),
                        pltpu.SemaphoreType.DMA((3, 2))],
        compiler_params=_cparams(2),
        name="mixer_merge",
    )(o, z, wa, wc, bc.reshape(1, d).astype(F32), proj, proj, bg2, bg2)


def _router_kernel(h_ref, g_ref, wr_ref, br_ref, valid_ref,
                   up_ref, e_ref, gate_ref, rank_ref, cnt_ref, carry_sc, *, tm):
    i = pl.program_id(0)

    @pl.when(i == 0)
    def _():
        carry_sc[...] = jnp.zeros(carry_sc.shape, F32)

    h = h_ref[...]
    ms = jnp.mean(h * h, axis=-1, keepdims=True)
    u = h * lax.rsqrt(ms + EPS) * g_ref[...]

    half = u.shape[1] // 2
    u_hi = u.astype(BF16)
    u_hi32 = u_hi.astype(F32)
    bits = lax.bitcast_convert_type(u_hi32, jnp.uint32)
    up_ref[...] = (bits[:, half:] & jnp.uint32(0xFFFF0000)) | (bits[:, :half] >> 16)

    u_lo = (u - u_hi32).astype(BF16)
    w = wr_ref[...]
    w_hi = w.astype(BF16)
    w_lo = (w - w_hi.astype(F32)).astype(BF16)
    nt = (((1,), (1,)), ((), ()))
    logits = (lax.dot_general(w_hi, u_hi, nt, preferred_element_type=F32)
              + lax.dot_general(w_hi, u_lo, nt, preferred_element_type=F32)
              + lax.dot_general(w_lo, u_hi, nt, preferred_element_type=F32)) + br_ref[...]
    n_e = logits.shape[0]
    eiota = lax.broadcasted_iota(jnp.int32, logits.shape, 0).astype(F32)
    work = logits
    sel = jnp.zeros(logits.shape, jnp.bool_)
    top_l, top_e = [], []
    for _ in range(TOP_K):
        mx = jnp.max(work, axis=0, keepdims=True)
        idx = jnp.min(jnp.where(work == mx, eiota, float(n_e)), axis=0, keepdims=True)
        hit = eiota == idx
        top_l.append(mx)
        top_e.append(idx)
        sel = sel | hit
        work = jnp.where(hit, -jnp.inf, work)
    ex = [jnp.exp(t - top_l[0]) for t in top_l]
    den = ex[0] + ex[1] + ex[2] + ex[3]
    gate_ref[...] = jnp.concatenate([e / den for e in ex], axis=0)
    e_ref[...] = jnp.concatenate(top_e, axis=0).astype(jnp.int32)

    selv = jnp.where(sel & (valid_ref[...] > 0.0), 1.0, 0.0)
    before = (lax.broadcasted_iota(jnp.int32, (tm, tm), 0)
              < lax.broadcasted_iota(jnp.int32, (tm, tm), 1)).astype(BF16)
    rank_all = jnp.dot(selv.astype(BF16), before, preferred_element_type=F32) + carry_sc[...]
    ranks = [jnp.sum(jnp.where(eiota == idx, rank_all, 0.0), axis=0, keepdims=True) for idx in top_e]
    rank_ref[...] = jnp.concatenate(ranks, axis=0).astype(jnp.int32)
    carry = carry_sc[...] + jnp.sum(selv, axis=1, keepdims=True)
    carry_sc[...] = carry
    cnt_ref[...] = jnp.broadcast_to(carry, cnt_ref.shape).astype(jnp.int32)


def _router(h2, g, w_router, b_router, valid, *, tm):
    m, d = h2.shape
    n_e = w_router.shape[1]
    tok = lambda dt: jax.ShapeDtypeStruct((TOP_K, m), dt)
    tok_spec = pl.BlockSpec((TOP_K, tm), lambda i: (0, i))
    return pl.pallas_call(
        functools.partial(_router_kernel, tm=tm),
        grid=(m // tm,),
        in_specs=[pl.BlockSpec((tm, d), lambda i: (i, 0)),
                  pl.BlockSpec((1, d), lambda i: (0, 0)),
                  pl.BlockSpec((n_e, d), lambda i: (0, 0)),
                  pl.BlockSpec((n_e, 1), lambda i: (0, 0)),
                  pl.BlockSpec((1, tm), lambda i: (0, i))],
        out_specs=[pl.BlockSpec((tm, d // 2), lambda i: (i, 0)),
                   tok_spec, tok_spec, tok_spec,
                   pl.BlockSpec((n_e, 128), lambda i: (0, 0))],
        out_shape=[jax.ShapeDtypeStruct((m, d // 2), jnp.uint32),
                   tok(jnp.int32), tok(F32), tok(jnp.int32),
                   jax.ShapeDtypeStruct((n_e, 128), jnp.int32)],
        scratch_shapes=[pltpu.VMEM((n_e, 1), F32)],
        compiler_params=_cparams(1),
        name="router",
    )(h2, g.reshape(1, d).astype(F32), w_router.T.astype(F32), b_router.reshape(n_e, 1).astype(F32), valid)


def _dispatch_kernel(dest_ref, te_ref, nu_ref, u_ref, xs_ref, zero_sc, sem, zsem, *, tm, n_tiles):
    i = pl.program_id(0)

    @pl.when(i == 0)
    def _():
        zero_sc[...] = jnp.zeros(zero_sc.shape, zero_sc.dtype)
        nu = nu_ref[0]

        def partly_filled(t):
            nxt = te_ref[jnp.minimum(t + 1, n_tiles - 1)]
            return (t >= nu - 1) | (te_ref[t] != nxt)

        def tile_copy(t):
            return pltpu.make_async_copy(zero_sc, xs_ref.at[pl.ds(t * MOE_TILE, MOE_TILE), :], zsem)

        def start(t, c):
            @pl.when(partly_filled(t))
            def _():
                tile_copy(t).start()
            return c

        def wait(t, c):
            @pl.when(partly_filled(t))
            def _():
                tile_copy(t).wait()
            return c

        lax.fori_loop(0, n_tiles, start, 0)
        lax.fori_loop(0, n_tiles, wait, 0)

    def start_row(r, c):
        for k in range(TOP_K):
            d = dest_ref[(i * tm + r) * TOP_K + k]
            pltpu.make_async_copy(u_ref.at[pl.ds(r, 1), :], xs_ref.at[pl.ds(d, 1), :], sem).start(priority=k % 2)
        return c

    lax.fori_loop(0, tm, start_row, 0, unroll=8)
    for k in range(TOP_K):
        pltpu.make_async_copy(u_ref, xs_ref.at[pl.ds(0, tm), :], sem).wait()


def _dispatch(dest_flat, tile_e, n_used, u_packed, n_rows, *, tm, n_tiles):
    m, w = u_packed.shape
    grid_spec = pltpu.PrefetchScalarGridSpec(
        num_scalar_prefetch=3,
        grid=(m // tm,),
        in_specs=[pl.BlockSpec((tm, w), lambda i, dest, te, nu: (i, 0))],
        out_specs=pl.BlockSpec(memory_space=pl.ANY),
        scratch_shapes=[pltpu.VMEM((MOE_TILE, w), jnp.uint32),
                        pltpu.SemaphoreType.DMA(()), pltpu.SemaphoreType.DMA(())],
    )
    return pl.pallas_call(
        functools.partial(_dispatch_kernel, tm=tm, n_tiles=n_tiles),
        grid_spec=grid_spec,
        out_shape=jax.ShapeDtypeStruct((n_rows, w), jnp.uint32),
        compiler_params=_cparams(1),
        name="moe_dispatch",
    )(dest_flat, tile_e, n_used, u_packed)


def _stream_expert_tiles(ts_ref, tc_ref, g_ref, obuf, in_copy, out_copy, compute, *, nj, n_e, n_tiles):
    j = pl.program_id(0)
    e = pl.program_id(1)
    n_used = ts_ref[n_e - 1] + tc_ref[n_e - 1]
    t0 = ts_ref[e]

    @pl.when((j == 0) & (e == 0))
    def _():
        g_ref[0] = 0
        in_copy(0, 0).start()
        in_copy(1, 1).start()

    def body(i, c):
        g = g_ref[0]
        t = t0 + i
        t2 = t + 2
        wraps = t2 >= n_used
        t2 = jnp.where(wraps, t2 - n_used, t2)

        @pl.when(jnp.logical_not(wraps & (j == nj - 1)))
        def _():
            in_copy(t2, (g + 2) % 3).start()

        in_copy(t, g % 3).wait()

        @pl.when(g >= 2)
        def _():
            out_copy(j, t, g % 2).wait()

        compute(g % 3, g % 2)
        out_copy(j, t, g % 2).start()
        g_ref[0] = g + 1
        return c

    lax.fori_loop(0, tc_ref[e], body, 0)

    @pl.when((j == nj - 1) & (e == n_e - 1))
    def _():
        g = g_ref[0]

        @pl.when(g >= 2)
        def _():
            out_copy(j, 0, g % 2).wait()

        @pl.when(g >= 1)
        def _():
            out_copy(j, 0, (g - 1) % 2).wait()

        obuf[0] = jnp.zeros(obuf.shape[1:], obuf.dtype)

        def zero_tile(t, c):
            for jj in range(nj):
                cp = out_copy(jj, t, 0)
                cp.start()
                cp.wait()
            return c

        lax.fori_loop(n_used, n_tiles, zero_tile, 0)


def _gate_up_kernel(ts_ref, tc_ref, xs_ref, wg_ref, wu_ref, bg_ref, bu_ref, act_ref,
                    wgb_sc, wub_sc, xbuf, obuf, g_ref, xsem, osem, *, tf, nj, n_e, n_tiles):
    def in_copy(t, slot):
        return pltpu.make_async_copy(xs_ref.at[pl.ds(t * MOE_TILE, MOE_TILE), :], xbuf.at[slot], xsem.at[slot])

    def out_copy(jj, t, slot):
        return pltpu.make_async_copy(obuf.at[slot],
                                     act_ref.at[pl.ds(t * MOE_TILE, MOE_TILE), pl.ds(jj * tf, tf)], osem.at[slot])

    @pl.when(tc_ref[pl.program_id(1)] > 0)
    def _():
        wgb_sc[...] = wg_ref[...].astype(BF16)
        wub_sc[...] = wu_ref[...].astype(BF16)

    def compute(in_slot, out_slot):
        w = xbuf[in_slot]
        half = w.shape[1]
        lo = lax.bitcast_convert_type(w << 16, F32).astype(BF16)
        hi = lax.bitcast_convert_type(w & jnp.uint32(0xFFFF0000), F32).astype(BF16)
        g = (jnp.dot(lo, wgb_sc[:half, :], preferred_element_type=F32)
             + jnp.dot(hi, wgb_sc[half:, :], preferred_element_type=F32) + bg_ref[...])
        u = (jnp.dot(lo, wub_sc[:half, :], preferred_element_type=F32)
             + jnp.dot(hi, wub_sc[half:, :], preferred_element_type=F32) + bu_ref[...])
        g = jnp.minimum(g, SWIGLU_LIMIT)
        u = jnp.clip(u, -SWIGLU_LIMIT, SWIGLU_LIMIT)
        obuf[out_slot] = ((u + 1.0) * (g * jax.nn.sigmoid(SWIGLU_ALPHA * g))).astype(obuf.dtype)

    _stream_expert_tiles(ts_ref, tc_ref, g_ref, obuf, in_copy, out_copy, compute, nj=nj, n_e=n_e, n_tiles=n_tiles)


def _gate_up(tile_start, tile_count, xs, w_gu, b_gu, *, n_tiles, tf):
    n_e, d, f2 = w_gu.shape
    f = f2 // 2
    nj = f // tf
    grid_spec = pltpu.PrefetchScalarGridSpec(
        num_scalar_prefetch=2,
        grid=(nj, n_e),
        in_specs=[pl.BlockSpec(memory_space=pl.ANY),
                  pl.BlockSpec((None, d, tf), lambda j, e, ts, tc: (e, 0, j)),
                  pl.BlockSpec((None, d, tf), lambda j, e, ts, tc: (e, 0, nj + j)),
                  pl.BlockSpec((None, 1, tf), lambda j, e, ts, tc: (e, 0, j)),
                  pl.BlockSpec((None, 1, tf), lambda j, e, ts, tc: (e, 0, nj + j))],
        out_specs=pl.BlockSpec(memory_space=pl.ANY),
        scratch_shapes=[pltpu.VMEM((d, tf), BF16), pltpu.VMEM((d, tf), BF16),
                        pltpu.VMEM((3, MOE_TILE, d // 2), jnp.uint32), pltpu.VMEM((2, MOE_TILE, tf), BF16),
                        pltpu.SMEM((1,), jnp.int32),
                        pltpu.SemaphoreType.DMA((3,)), pltpu.SemaphoreType.DMA((2,))],
    )
    return pl.pallas_call(
        functools.partial(_gate_up_kernel, tf=tf, nj=nj, n_e=n_e, n_tiles=n_tiles),
        grid_spec=grid_spec,
        out_shape=jax.ShapeDtypeStruct((n_tiles * MOE_TILE, f), BF16),
        compiler_params=_cparams(2),
        name="moe_gate_up",
    )(tile_start, tile_count, xs, w_gu, w_gu, b_gu, b_gu)


def _down_kernel(ts_ref, tc_ref, act_ref, wd_ref, bd_ref, y_ref, wdb_sc, abuf, obuf, g_ref, asem, osem,
                 *, tn, nj, n_e, n_tiles):
    def in_copy(t, slot):
        return pltpu.make_async_copy(act_ref.at[pl.ds(t * MOE_TILE, MOE_TILE), :], abuf.at[slot], asem.at[slot])

    half = tn // 2

    def out_copy(jj, t, slot):
        return pltpu.make_async_copy(obuf.at[slot],
                                     y_ref.at[pl.ds(t * MOE_TILE, MOE_TILE), pl.ds(jj * half, half)], osem.at[slot])

    @pl.when(tc_ref[pl.program_id(1)] > 0)
    def _():
        wdb_sc[...] = wd_ref[...].astype(BF16)

    def compute(in_slot, out_slot):
        y = jnp.dot(abuf[in_slot], wdb_sc[...], preferred_element_type=F32) + bd_ref[...]
        bits = lax.bitcast_convert_type(y.astype(BF16).astype(F32), jnp.uint32)
        obuf[out_slot] = (bits[:, half:] & jnp.uint32(0xFFFF0000)) | (bits[:, :half] >> 16)

    _stream_expert_tiles(ts_ref, tc_ref, g_ref, obuf, in_copy, out_copy, compute, nj=nj, n_e=n_e, n_tiles=n_tiles)


def _down(tile_start, tile_count, act, w_d, b_d, *, n_tiles, tn):
    n_e, f, d = w_d.shape
    grid_spec = pltpu.PrefetchScalarGridSpec(
        num_scalar_prefetch=2,
        grid=(d // tn, n_e),
        in_specs=[pl.BlockSpec(memory_space=pl.ANY),
                  pl.BlockSpec((None, f, tn), lambda j, e, ts, tc: (e, 0, j)),
                  pl.BlockSpec((None, 1, tn), lambda j, e, ts, tc: (e, 0, j))],
        out_specs=pl.BlockSpec(memory_space=pl.ANY),
        scratch_shapes=[pltpu.VMEM((f, tn), BF16),
                        pltpu.VMEM((3, MOE_TILE, f), BF16), pltpu.VMEM((2, MOE_TILE, tn // 2), jnp.uint32),
                        pltpu.SMEM((1,), jnp.int32),
                        pltpu.SemaphoreType.DMA((3,)), pltpu.SemaphoreType.DMA((2,))],
    )
    return pl.pallas_call(
        functools.partial(_down_kernel, tn=tn, nj=d // tn, n_e=n_e, n_tiles=n_tiles),
        grid_spec=grid_spec,
        out_shape=jax.ShapeDtypeStruct((n_tiles * MOE_TILE, d // 2), jnp.uint32),
        compiler_params=_cparams(2),
        name="moe_down",
    )(tile_start, tile_count, act, w_d, b_d)


def _combine_kernel(dest_ref, h_ref, gate_ref, g_ref, ys_ref, out_ref, buf, sem, *, tm, lp, row0, nst, pack_w):
    s = pl.program_id(0)
    n_steps = pl.num_programs(0)

    def start_gather(step, slot):
        t0 = (step // nst) * lp + row0 + (step % nst) * tm

        def start_row(r, c):
            for k in range(TOP_K):
                d = dest_ref[(t0 + r) * TOP_K + k]
                pltpu.make_async_copy(ys_ref.at[pl.ds(d, 1), :], buf.at[slot, k, pl.ds(r, 1), :],
                                      sem.at[slot]).start(priority=k % 2)
            return c

        lax.fori_loop(0, tm, start_row, 0, unroll=8)

    @pl.when(s == 0)
    def _():
        start_gather(0, 0)

    @pl.when(s + 1 < n_steps)
    def _():
        start_gather(s + 1, (s + 1) % 2)

    slot = s % 2
    for k in range(TOP_K):
        pltpu.make_async_copy(ys_ref.at[pl.ds(0, tm), :], buf.at[slot, k], sem.at[slot]).wait()

    gate = gate_ref[...]
    half = pack_w // 2
    acc = jnp.zeros(h_ref.shape, F32)
    for k in range(TOP_K):
        w = buf[slot, k]
        lo = lax.bitcast_convert_type(w << 16, F32)
        hi = lax.bitcast_convert_type(w & jnp.uint32(0xFFFF0000), F32)
        cols = []
        for c in range(w.shape[1] // half):
            cols += [lo[:, c * half:(c + 1) * half], hi[:, c * half:(c + 1) * half]]
        acc = acc + gate[:, k:k + 1] * jnp.concatenate(cols, axis=1)
    h = h_ref[...] + acc
    ms = jnp.mean(h * h, axis=-1, keepdims=True)
    out_ref[...] = (h * lax.rsqrt(ms + EPS) * g_ref[...]).astype(out_ref.dtype)


def _combine(dest_flat, h2, gate, g, ys, *, batch, seq, lp, row0, tm, pack_w):
    d = h2.shape[1]
    nb_b = lp // tm
    nb0 = row0 // tm
    nst = seq // tm
    blk = lambda s: (s // nst) * nb_b + nb0 + s % nst
    grid_spec = pltpu.PrefetchScalarGridSpec(
        num_scalar_prefetch=1,
        grid=(batch * nst,),
        in_specs=[pl.BlockSpec((tm, d), lambda s, dest: (blk(s), 0)),
                  pl.BlockSpec((tm, TOP_K), lambda s, dest: (blk(s), 0)),
                  pl.BlockSpec((1, d), lambda s, dest: (0, 0)),
                  pl.BlockSpec(memory_space=pl.ANY)],
        out_specs=pl.BlockSpec((None, tm, d), lambda s, dest: (s // nst, s % nst, 0)),
        scratch_shapes=[pltpu.VMEM((2, TOP_K, tm, d // 2), jnp.uint32), pltpu.SemaphoreType.DMA((2,))],
    )
    return pl.pallas_call(
        functools.partial(_combine_kernel, tm=tm, lp=lp, row0=row0, nst=nst, pack_w=pack_w),
        grid_spec=grid_spec,
        out_shape=jax.ShapeDtypeStruct((batch, seq, d), F32),
        compiler_params=_cparams(1),
        name="moe_combine",
    )(dest_flat, h2, gate, g.reshape(1, d).astype(F32), ys)


def _pick(pref, n):
    t = pref
    while n % t:
        t //= 2
    return t


def _row_tile(n, pref):
    t = pref // ROW_ALIGN * ROW_ALIGN
    while n % t:
        t -= ROW_ALIGN
    return t


def kernel(x, meta_tokens, norm_mix_g, w_in, b_gate, lambda_q1, lambda_k1, lambda_q2, lambda_k2, head_norm_g, w_attn_out, conv_w, conv_b, conv_ln_g, conv_ln_b, w_conv_out, b_conv_out, w_out, norm_ffn_g, w_router, b_router, w_gate_up, b_gate_up, w_down, b_down, final_norm_g):
    batch, seq, d = x.shape
    depth = w_in.shape[0]
    assert depth == 1 and seq % ROW_ALIGN == 0 and N_META <= ROW_ALIGN
    n_heads = d // 256
    hw = 2 * HEAD_DIM
    qk_w = n_heads * hw
    conv_ch = conv_w.shape[2]
    n_pad = ROW_ALIGN - N_META
    lp = n_pad + N_META + seq
    tp = batch * lp
    f = w_down.shape[2]
    layer = 0
    lam_init = 0.8 - 0.6 * math.exp(-0.3 * layer)

    h0, u = _embed_norm(x, meta_tokens, norm_mix_g[layer], n_pad=n_pad)

    proj = _matmul(u, w_in[layer], BF16, _row_tile(tp, IN_PROJ_ROWS), _pick(IN_PROJ_COLS, w_in.shape[2]),
                   name="in_proj")

    tq = ATTN_Q_TILE if lp % ATTN_Q_TILE == 0 else ROW_ALIGN
    o = _attention(proj, lambda_q1[layer], lambda_k1[layer], lambda_q2[layer], lambda_k2[layer],
                   head_norm_g[layer], batch=batch, lp=lp, n_heads=n_heads, tq=tq, lam_init=lam_init,
                   n_pad=n_pad)
    ca_col = 3 * qk_w
    z = _conv_branch(proj, conv_w[layer], conv_b[layer], conv_ln_g[layer], conv_ln_b[layer],
                     ca_blk=ca_col // conv_ch, cg_blk=ca_col // conv_ch + 1, tm=_pick(ROW_TILE, tp))
    merged = _merge(o, z, proj, w_attn_out[layer], w_conv_out[layer],
                    b_conv_out[layer], b_gate[layer], gate_col=ca_col + 2 * conv_ch,
                    tm=_row_tile(tp, MIX_OUT_ROWS), tn=_pick(MIX_OUT_COLS, d))
    h2 = _matmul(merged, w_out[layer], F32, _row_tile(tp, MIX_OUT_ROWS), _pick(MIX_OUT_COLS, d), res=h0,
                 name="out_proj")

    pos = np.arange(tp) % lp
    valid_np = pos >= n_pad
    valid = jnp.asarray(valid_np.astype(np.float32).reshape(1, tp))
    u_packed, top_e, gate_t, rank_t, cnt = _router(h2, norm_ffn_g[layer], w_router[layer], b_router[layer],
                                                   valid, tm=_pick(ROW_TILE, tp))
    counts = cnt[:, 0]
    padded = (counts + MOE_TILE - 1) // MOE_TILE * MOE_TILE
    e_ids = np.arange(N_EXPERTS)
    pad_end = jnp.sum(jnp.where(jnp.asarray(e_ids[None, :] <= e_ids[:, None]), padded[None, :], 0), axis=1)
    pad_start = pad_end - padded
    start_tok = jnp.sum(jnp.where(top_e[:, :, None] == jnp.asarray(e_ids, jnp.int32), pad_start, 0), axis=-1)
    n_real = int(valid_np.sum()) * TOP_K
    assert n_real >= 2 * MOE_TILE
    n_tiles = -(-(n_real + N_EXPERTS * (MOE_TILE - 1)) // MOE_TILE)
    n_slots = n_tiles * MOE_TILE
    dump = n_slots + (np.cumsum(~valid_np) - 1)[None, :] * TOP_K + np.arange(TOP_K)[:, None]
    dest_t = jnp.where(jnp.asarray(valid_np)[None, :], start_tok + rank_t, jnp.asarray(dump, jnp.int32))
    dest_flat = dest_t.T.reshape(-1).astype(jnp.int32)
    n_dump = int((~valid_np).sum()) * TOP_K
    n_used = (pad_end[-1] // MOE_TILE).astype(jnp.int32).reshape(1)
    tile_start = jnp.asarray(np.arange(n_tiles, dtype=np.int32) * MOE_TILE)
    tile_e = jnp.minimum(jnp.sum((pad_end[None, :] <= tile_start[:, None]).astype(jnp.int32), axis=1),
                         N_EXPERTS - 1).astype(jnp.int32)

    xs = _dispatch(dest_flat, tile_e, n_used, u_packed, n_slots + n_dump, tm=ROW_ALIGN, n_tiles=n_tiles)
    tile_first = (pad_start // MOE_TILE).astype(jnp.int32)
    tile_count = (padded // MOE_TILE).astype(jnp.int32)
    act = _gate_up(tile_first, tile_count, xs, w_gate_up[layer],
                   b_gate_up[layer].reshape(N_EXPERTS, 1, 2 * f).astype(F32), n_tiles=n_tiles,
                   tf=_pick(GATE_UP_COLS, f))
    down_tn = _pick(DOWN_COLS, d)
    ys = _down(tile_first, tile_count, act, w_down[layer],
               b_down[layer].reshape(N_EXPERTS, 1, d).astype(F32), n_tiles=n_tiles, tn=down_tn)
    return _combine(dest_flat, h2, gate_t.T, final_norm_g, ys, batch=batch, seq=seq, lp=lp,
                    row0=ROW_ALIGN, tm=ROW_ALIGN, pack_w=down_tn)
```
